```python
import math
import jax
import jax.numpy as jnp
from jax import lax
import numpy as np

D_MODEL = 1024
BATCH = 8
SEQ = 2048
DEPTH = 2

GRID_W = 64
CTX_LEN = 256
N_MIXERS = 2
N_DN_LAYERS = (DEPTH + N_MIXERS - 1) // N_MIXERS
N_DA_LAYERS = DEPTH // N_MIXERS

DN_HEADS = 8
DN_DK = 128
DN_DV = 128
DN_CONV = 5
DN_CHUNK = 64

DA_HEADS = 8
DA_DIM = 64
DA_QBLOCK = 128
ROPE_BASE = 10000.0

FFN_HIDDEN = -(-(8 * D_MODEL) // (3 * 256)) * 256

DEEPNORM_ALPHA = (2.0 * DEPTH) ** 0.25
DEEPNORM_BETA = (8.0 * DEPTH) ** -0.25
LN_EPS = 1e-5
RMS_EPS = 1e-6

kernel_name = 'hybrid_gdn_diffattn_prefix_trunk'


def _layernorm(x, g, b):
    xf = x.astype(jnp.float32)
    mu = jnp.mean(xf, axis=-1, keepdims=True)
    var = jnp.mean(jnp.square(xf - mu), axis=-1, keepdims=True)
    return ((xf - mu) * lax.rsqrt(var + LN_EPS)).astype(x.dtype) * g + b


def _rmsnorm(x, g):
    xf = x.astype(jnp.float32)
    y = xf * lax.rsqrt(jnp.mean(jnp.square(xf), axis=-1, keepdims=True) + RMS_EPS)
    return y.astype(x.dtype) * g


def _l2norm(x):
    xf = x.astype(jnp.float32)
    return xf * lax.rsqrt(jnp.sum(jnp.square(xf), axis=-1, keepdims=True) + RMS_EPS)


def _swiglu(h, w_gate, w_up, w_down):
    return (jax.nn.silu(h @ w_gate) * (h @ w_up)) @ w_down


def _centred_dwconv(x, w):
    pad = (DN_CONV - 1) // 2
    return lax.conv_general_dilated(
        x, w[:, None, :].astype(x.dtype), window_strides=(1,), padding=[(pad, pad)],
        dimension_numbers=('NWC', 'WIO', 'NWC'), feature_group_count=x.shape[-1])


def _axial_rope(rows, dtype):
    quarter = DA_DIM // 4
    inv_freq = ROPE_BASE ** (-jnp.arange(quarter, dtype=jnp.float32) / quarter)
    row = jnp.repeat(jnp.arange(rows, dtype=jnp.float32), GRID_W)
    col = jnp.tile(jnp.arange(GRID_W, dtype=jnp.float32), rows)
    ang_r = row[:, None] * inv_freq
    ang_c = col[:, None] * inv_freq
    ang = jnp.concatenate([ang_r, ang_r, ang_c, ang_c], axis=-1)
    return jnp.cos(ang).astype(dtype), jnp.sin(ang).astype(dtype)


def _apply_rope(x, cos, sin):
    quarter = DA_DIM // 4
    half = DA_DIM // 2

    def rot_half(u):
        return jnp.concatenate([-u[..., quarter:], u[..., :quarter]], axis=-1)

    x_rot = jnp.concatenate([rot_half(x[..., :half]), rot_half(x[..., half:])], axis=-1)
    return x * cos + x_rot * sin


def _flip_seq(t, direction):
    return jnp.flip(t, axis=2) if direction == 1 else t


def _chunked_gated_delta(q, k, v, g, beta, s0):
    bsz, nh, L, _ = q.shape
    dv = v.shape[-1]
    nc = L // DN_CHUNK

    def blocks(t):
        return t.reshape(bsz, nh, nc, DN_CHUNK, *t.shape[3:])

    q, k, v, g, beta = (blocks(t) for t in (q, k, v.astype(jnp.float32), g, beta))
    G = jnp.cumsum(g, axis=-1)
    idx = jnp.arange(DN_CHUNK)
    incl = idx[:, None] >= idx[None, :]
    strict = idx[:, None] > idx[None, :]
    gamma = jnp.exp(jnp.where(incl, G[..., :, None] - G[..., None, :], -jnp.inf))
    kb = k * beta[..., None]
    a_low = jnp.where(strict, jnp.einsum('bhnid,bhnjd->bhnij', kb, k) * gamma, 0.0)
    eye = jnp.eye(DN_CHUNK, dtype=jnp.float32)
    rhs = jnp.concatenate([v * beta[..., None], kb * jnp.exp(G)[..., None]], axis=-1)
    uw = lax.linalg.triangular_solve(eye + a_low, rhs, left_side=True, lower=True, unit_diagonal=True)
    u, w = uw[..., :dv], uw[..., dv:]
    qk = jnp.einsum('bhnid,bhnjd->bhnij', q, k) * gamma
    g_last = G[..., -1]
    q_dec = q * jnp.exp(G)[..., None]
    k_dec = k * jnp.exp(g_last[..., None] - G)[..., None]

    def step(s, inp):
        u_n, w_n, qk_n, qd_n, kd_n, gl_n = inp
        v_new = u_n - jnp.einsum('bhck,bhkv->bhcv', w_n, s)
        o_n = jnp.einsum('bhck,bhkv->bhcv', qd_n, s) + jnp.einsum('bhij,bhjv->bhiv', qk_n, v_new)
        s = s * jnp.exp(gl_n)[..., None, None] + jnp.einsum('bhck,bhcv->bhkv', kd_n, v_new)
        return s, o_n

    xs = tuple(jnp.moveaxis(t, 2, 0) for t in (u, w, qk, q_dec, k_dec, g_last))
    s_final, o = lax.scan(step, s0, xs)
    o = jnp.moveaxis(o, 0, 2).reshape(bsz, nh, L, dv)
    return o, s_final


def _gated_deltanet(h_lat, h_ctx, w_in, conv_w, a_log, dt_bias, norm_g, w_out, need_ctx):
    f32 = jnp.float32
    hk = DN_HEADS * DN_DK
    hv = DN_HEADS * DN_DV
    n_qkv = 2 * hk + hv

    def project(h):
        bsz, L, _ = h.shape
        p = h @ w_in
        qkv = jax.nn.silu(_centred_dwconv(p[..., :n_qkv], conv_w))
        q = _l2norm(qkv[..., :hk].reshape(bsz, L, DN_HEADS, DN_DK)) * (DN_DK ** -0.5)
        k = _l2norm(qkv[..., hk:2 * hk].reshape(bsz, L, DN_HEADS, DN_DK))
        v = qkv[..., 2 * hk:].reshape(bsz, L, DN_HEADS, DN_DV).astype(f32)
        z = p[..., n_qkv:n_qkv + hv]
        off = n_qkv + hv
        a = p[..., off:off + 2 * DN_HEADS].reshape(bsz, L, 2, DN_HEADS).astype(f32)
        b = p[..., off + 2 * DN_HEADS:].reshape(bsz, L, 2, DN_HEADS).astype(f32)
        g = -jnp.exp(a_log.astype(f32)) * jax.nn.softplus(a + dt_bias.astype(f32))
        beta = jax.nn.sigmoid(b)
        heads = lambda t: jnp.transpose(t, (0, 2, 1, 3))
        dirs = lambda t: jnp.transpose(t, (2, 0, 3, 1))
        return heads(q), heads(k), heads(v), z, dirs(g), dirs(beta)

    qc, kc, vc, zc, gc, bc = project(h_ctx)
    ql, kl, vl, zl, gl, bl = project(h_lat)
    bsz = h_lat.shape[0]
    outs_lat, outs_ctx = [], []
    for d in range(2):
        s0 = jnp.zeros((bsz, DN_HEADS, DN_DK, DN_DV), f32)
        oc, s_ctx = _chunked_gated_delta(_flip_seq(qc, d), _flip_seq(kc, d), _flip_seq(vc, d),
                                         _flip_seq(gc[d], d), _flip_seq(bc[d], d), s0)
        ol, _ = _chunked_gated_delta(_flip_seq(ql, d), _flip_seq(kl, d), _flip_seq(vl, d),
                                     _flip_seq(gl[d], d), _flip_seq(bl[d], d), s_ctx)
        outs_ctx.append(_flip_seq(oc, d))
        outs_lat.append(_flip_seq(ol, d))

    def out(o, z):
        bsz_, _, L, _ = o.shape
        o = _rmsnorm(jnp.transpose(o, (0, 2, 1, 3)), norm_g.astype(f32))
        o = o.astype(z.dtype) * jax.nn.silu(z.reshape(bsz_, L, DN_HEADS, DN_DV))
        return o.reshape(bsz_, L, hv) @ w_out

    y_ctx = out(outs_ctx[0] + outs_ctx[1], zc) if need_ctx else None
    return out(outs_lat[0] + outs_lat[1], zl), y_ctx


def _diff_attention(h_lat, h_ctx, w_in, lam, subln_g, w_out, lambda_init, rope, need_ctx):
    f32 = jnp.float32
    hq = DA_HEADS * 2 * DA_DIM

    def project(h, use_rope):
        bsz, L, _ = h.shape
        p = h @ w_in
        q = p[..., :hq].reshape(bsz, L, DA_HEADS, 2, DA_DIM).transpose(3, 0, 2, 1, 4)
        k = p[..., hq:2 * hq].reshape(bsz, L, DA_HEADS, 2, DA_DIM).transpose(3, 0, 2, 1, 4)
        v = p[..., 2 * hq:].reshape(bsz, L, DA_HEADS, 2 * DA_DIM).transpose(0, 2, 1, 3)
        if use_rope:
            q = _apply_rope(q, rope[0], rope[1])
            k = _apply_rope(k, rope[0], rope[1])
        return q * (DA_DIM ** -0.5), k, v

    lam_f = lam.astype(f32)
    lam_val = (jnp.exp(jnp.sum(lam_f[0] * lam_f[1])) - jnp.exp(jnp.sum(lam_f[2] * lam_f[3]))
               + lambda_init)

    def attend(q, k, v):
        s = jnp.einsum('cbhqd,cbhkd->cbhqk', q, k).astype(f32)
        p = jax.nn.softmax(s, axis=-1)
        a = p[0] - lam_val * p[1]
        return jnp.einsum('bhqk,bhkv->bhqv', a, v.astype(f32))

    def out(o):
        bsz_, _, L, _ = o.shape
        o = _rmsnorm(jnp.transpose(o, (0, 2, 1, 3)), subln_g.astype(f32)) * (1.0 - lambda_init)
        return o.reshape(bsz_, L, hq).astype(h_lat.dtype) @ w_out

    qc, kc, vc = project(h_ctx, False)
    ql, kl, vl = project(h_lat, True)
    k_all = jnp.concatenate([kl, kc], axis=3)
    v_all = jnp.concatenate([vl, vc], axis=2)
    bsz, L = h_lat.shape[:2]
    nb = L // DA_QBLOCK
    qb = jnp.moveaxis(ql.reshape(2, bsz, DA_HEADS, nb, DA_QBLOCK, DA_DIM), 3, 0)
    ob = lax.map(lambda q_blk: attend(q_blk, k_all, v_all), qb)
    o_lat = jnp.moveaxis(ob, 0, 2).reshape(bsz, DA_HEADS, L, 2 * DA_DIM)
    y_ctx = out(attend(qc, kc, vc)) if need_ctx else None
    return out(o_lat), y_ctx


def setup_inputs(seed: int = 0) -> dict:
    key = jax.random.key(seed)
    ks = iter(jax.random.split(key, 32))
    f32 = jnp.float32
    D = D_MODEL
    hk = DN_HEADS * DN_DK
    hv = DN_HEADS * DN_DV
    hq = DA_HEADS * 2 * DA_DIM

    def nrm(shape, scale):
        return jax.random.normal(next(ks), shape, f32) * scale

    inputs = {}
    inputs['x'] = nrm((BATCH, SEQ, D), 1.0)
    inputs['c'] = nrm((BATCH, D), 1.0)
    inputs['ctx'] = nrm((BATCH, CTX_LEN, D), 1.0)
    inputs['c_ctx'] = nrm((D,), 1.0)
    inputs['ada_w'] = nrm((DEPTH, D, 6 * D), D ** -0.5)
    inputs['ada_b'] = nrm((DEPTH, 6 * D), 0.01)
    inputs['ln1_g'] = 1.0 + nrm((DEPTH, D), 0.02)
    inputs['ln1_b'] = nrm((DEPTH, D), 0.02)
    inputs['ln2_g'] = 1.0 + nrm((DEPTH, D), 0.02)
    inputs['ln2_b'] = nrm((DEPTH, D), 0.02)
    inputs['ffn_w_gate'] = nrm((DEPTH, D, FFN_HIDDEN), D ** -0.5)
    inputs['ffn_w_up'] = nrm((DEPTH, D, FFN_HIDDEN), D ** -0.5)
    inputs['ffn_w_down'] = nrm((DEPTH, FFN_HIDDEN, D), FFN_HIDDEN ** -0.5 * DEEPNORM_BETA)
    inputs['dn_w_in'] = jnp.concatenate([
        nrm((N_DN_LAYERS, D, 2 * hk + 2 * hv), D ** -0.5),
        nrm((N_DN_LAYERS, D, 2 * DN_HEADS), 0.1 * D ** -0.5),
        nrm((N_DN_LAYERS, D, 2 * DN_HEADS), D ** -0.5),
    ], axis=-1)
    inputs['dn_conv'] = nrm((N_DN_LAYERS, DN_CONV, 2 * hk + hv), DN_CONV ** -0.5)
    inputs['dn_a_log'] = jnp.log(jax.random.uniform(next(ks), (N_DN_LAYERS, 2, DN_HEADS), f32, 1.0, 16.0))
    dt = jnp.exp(jax.random.uniform(next(ks), (N_DN_LAYERS, 2, DN_HEADS), f32,
                                    math.log(1e-3), math.log(1e-1)))
    inputs['dn_dt_bias'] = dt + jnp.log(-jnp.expm1(-dt))
    inputs['dn_norm_g'] = 1.0 + nrm((N_DN_LAYERS, DN_DV), 0.02)
    inputs['dn_w_out'] = nrm((N_DN_LAYERS, hv, D), hv ** -0.5 * DEEPNORM_BETA)
    inputs['da_w_in'] = nrm((N_DA_LAYERS, D, 3 * hq), D ** -0.5)
    inputs['da_lambda'] = nrm((N_DA_LAYERS, 4, DA_DIM), 0.1)
    inputs['da_subln_g'] = 1.0 + nrm((N_DA_LAYERS, 2 * DA_DIM), 0.02)
    inputs['da_w_out'] = nrm((N_DA_LAYERS, hq, D), hq ** -0.5 * DEEPNORM_BETA)
    return inputs


def reference(x, c, ctx, c_ctx, ada_w, ada_b, ln1_g, ln1_b, ln2_g, ln2_b,
              ffn_w_gate, ffn_w_up, ffn_w_down,
              dn_w_in, dn_conv, dn_a_log, dn_dt_bias, dn_norm_g, dn_w_out,
              da_w_in, da_lambda, da_subln_g, da_w_out):
    L = x.shape[1]
    rows = L // GRID_W
    rope = _axial_rope(rows, x.dtype)
    s_lat = jax.nn.silu(c)
    s_ctx = jax.nn.silu(c_ctx)
    for i in range(DEPTH):
        last = i == DEPTH - 1
        sh1, sc1, gt1, sh2, sc2, gt2 = jnp.split((s_lat @ ada_w[i] + ada_b[i])[:, None, :], 6, axis=-1)
        sh1c, sc1c, gt1c, sh2c, sc2c, gt2c = jnp.split(s_ctx @ ada_w[i] + ada_b[i], 6, axis=-1)
        h = x * (1.0 + sc1) + sh1
        hc = ctx * (1.0 + sc1c) + sh1c
        j = i // N_MIXERS
        if i % N_MIXERS == 0:
            y, yc = _gated_deltanet(h, hc, dn_w_in[j], dn_conv[j], dn_a_log[j], dn_dt_bias[j],
                                    dn_norm_g[j], dn_w_out[j], not last)
        else:
            y, yc = _diff_attention(h, hc, da_w_in[j], da_lambda[j], da_subln_g[j], da_w_out[j],
                                    0.8 - 0.6 * math.exp(-0.3 * i), rope, not last)
        x = _layernorm(DEEPNORM_ALPHA * x + gt1 * y, ln1_g[i], ln1_b[i])
        x = _layernorm(DEEPNORM_ALPHA * x
                       + gt2 * _swiglu(x * (1.0 + sc2) + sh2, ffn_w_gate[i], ffn_w_up[i], ffn_w_down[i]),
                       ln2_g[i], ln2_b[i])
        if not last:
            ctx = _layernorm(DEEPNORM_ALPHA * ctx + gt1c * yc, ln1_g[i], ln1_b[i])
            ctx = _layernorm(DEEPNORM_ALPHA * ctx
                             + gt2c * _swiglu(ctx * (1.0 + sc2c) + sh2c, ffn_w_gate[i], ffn_w_up[i], ffn_w_down[i]),
                             ln2_g[i], ln2_b[i])
    return x
```

```python
import functools
import math

import jax
import jax.numpy as jnp
from jax import lax
from jax.experimental import pallas as pl
from jax.experimental.pallas import tpu as pltpu

F32 = jnp.float32
BF16 = jnp.bfloat16
HIGHEST = lax.Precision.HIGHEST

DN_HEADS = 8
DN_DK = 128
DN_DV = 128
DN_CONV = 5
DN_CHUNK = 64
DA_HEADS = 8
DA_DIM = 64
GRID_W = 64
ROPE_BASE = 10000.0
LN_EPS = 1e-5
RMS_EPS = 1e-6

LANES = 128
SUBLANES = 8
MOD_ROWS = 16
ROW_TILE = 256
COL_TILE = 256
ADA_COL_TILE = 1536
SCAN_HEADS = 2
ATT_Q_TILE = 256


def _cparams(dims, vmem_mb):
    return pltpu.CompilerParams(dimension_semantics=dims, vmem_limit_bytes=vmem_mb * 1024 * 1024)


def _silu(x):
    return x * jax.nn.sigmoid(x)


def _dot(a, b, precision=None):
    return jnp.dot(a, b, preferred_element_type=F32, precision=precision)


def _dot_nt(a, b):
    return lax.dot_general(a, b, (((1,), (1,)), ((), ())), preferred_element_type=F32)


def _dot_tn(a, b):
    return lax.dot_general(a, b, (((0,), (0,)), ((), ())), preferred_element_type=F32)


def _layernorm(x, g, b):
    mu = jnp.mean(x, axis=-1, keepdims=True)
    xc = x - mu
    var = jnp.mean(xc * xc, axis=-1, keepdims=True)
    return xc * lax.rsqrt(var + LN_EPS) * g + b


def _ada_kernel(cs_ref, w_ref, b_ref, o_ref):
    s = _silu(cs_ref[...]).astype(BF16)
    o_ref[0] = _dot(s, w_ref[0].astype(BF16)) + b_ref[0]


def _ada_modulation(cs, ada_w, ada_b):
    depth, d, n = ada_w.shape
    tn = ADA_COL_TILE if n % ADA_COL_TILE == 0 else n
    return pl.pallas_call(
        _ada_kernel,
        grid=(depth, n // tn),
        in_specs=[
            pl.BlockSpec((MOD_ROWS, d), lambda i, j: (0, 0)),
            pl.BlockSpec((1, d, tn), lambda i, j: (i, 0, j)),
            pl.BlockSpec((1, 1, tn), lambda i, j: (i, 0, j)),
        ],
        out_specs=pl.BlockSpec((1, MOD_ROWS, tn), lambda i, j: (i, 0, j)),
        out_shape=jax.ShapeDtypeStruct((depth, MOD_ROWS, n), F32),
        compiler_params=_cparams(("arbitrary", "arbitrary"), 40),
        name="ada_modulation",
    )(cs, ada_w, ada_b.reshape(depth, 1, n))


def _modulate_into(xb_ref, x_ref, sc_l, sh_l, sc_c, sh_c, n_lat):
    xb_ref[:n_lat, :] = (x_ref[0, :n_lat, :] * (1.0 + sc_l[0]) + sh_l[0]).astype(BF16)
    xb_ref[n_lat:, :] = (x_ref[0, n_lat:, :] * (1.0 + sc_c[0]) + sh_c[0]).astype(BF16)


def _mod_specs(batch, d, sc_idx, sh_idx):
    return [
        pl.BlockSpec((1, 1, d), lambda b, j: (b, 0, sc_idx)),
        pl.BlockSpec((1, 1, d), lambda b, j: (b, 0, sh_idx)),
        pl.BlockSpec((1, 1, d), lambda b, j: (batch, 0, sc_idx)),
        pl.BlockSpec((1, 1, d), lambda b, j: (batch, 0, sh_idx)),
    ]


def _dn_proj_kernel(x_ref, sc_l, sh_l, sc_c, sh_c, w_ref, conv_ref, alog_ref, dt_ref,
                    qkv_ref, z_ref, gcol_ref, grow_ref, xb_ref, pad_ref,
                    *, n_lat, n_tot, n_qkv_tiles, n_z_tiles):
    j = pl.program_id(1)
    n_ctx = n_tot - n_lat
    halo = SUBLANES
    pad = (DN_CONV - 1) // 2
    lat0 = halo
    ctx0 = 2 * halo + n_lat

    @pl.when(j == 0)
    def _():
        _modulate_into(xb_ref, x_ref, sc_l, sh_l, sc_c, sh_c, n_lat)

    acc = _dot(xb_ref[...], w_ref[...])

    @pl.when(j < n_qkv_tiles)
    def _():
        zeros = jnp.zeros((halo, COL_TILE), F32)
        pad_ref[0:halo, :] = zeros
        pad_ref[lat0 + n_lat:ctx0, :] = zeros
        pad_ref[ctx0 + n_ctx:ctx0 + n_ctx + halo, :] = zeros
        pad_ref[lat0:lat0 + n_lat, :] = acc[:n_lat]
        pad_ref[ctx0:ctx0 + n_ctx, :] = acc[n_lat:]
        cw = conv_ref[...]
        n_qk_tiles = 2 * n_qkv_tiles // 3
        is_q = j < n_qk_tiles // 2
        is_qk = j < n_qk_tiles
        for base, n_rows, out0 in ((lat0, n_lat, 0), (ctx0, n_ctx, n_lat)):
            y = None
            for t in range(DN_CONV):
                term = pad_ref[base + t - pad:base + t - pad + n_rows, :] * cw[t:t + 1, :]
                y = term if y is None else y + term
            y = _silu(y)
            for h0 in range(0, COL_TILE, DN_DK):
                yh = y[:, h0:h0 + DN_DK]
                ss = jnp.sum(yh * yh, axis=-1, keepdims=True)
                fac = lax.rsqrt(ss + RMS_EPS)
                fac = jnp.where(is_q, fac * (DN_DK ** -0.5), fac)
                fac = jnp.where(is_qk, fac, jnp.ones_like(fac))
                qkv_ref[0, out0:out0 + n_rows, h0:h0 + DN_DK] = yh * fac

    @pl.when(jnp.logical_and(j >= n_qkv_tiles, j < n_qkv_tiles + n_z_tiles))
    def _():
        z_ref[0] = acc

    @pl.when(j == n_qkv_tiles + n_z_tiles)
    def _():
        t = acc[:, :LANES]
        lane = lax.broadcasted_iota(jnp.int32, t.shape, 1)
        x = t + dt_ref[...]
        softplus = jnp.maximum(x, 0.0) + jnp.log1p(jnp.exp(-jnp.abs(x)))
        decay = -jnp.exp(alog_ref[...]) * softplus
        gates = jnp.where(lane < 2 * DN_HEADS, decay, jax.nn.sigmoid(t))
        ri = lax.broadcasted_iota(jnp.int32, (DN_CHUNK, DN_CHUNK), 0)
        ci = lax.broadcasted_iota(jnp.int32, (DN_CHUNK, DN_CHUNK), 1)
        tri_f = (ri >= ci).astype(F32)
        tri_b = (ri <= ci).astype(F32)
        lane_c = lax.broadcasted_iota(jnp.int32, (DN_CHUNK, LANES), 1)
        for c in range(n_tot // DN_CHUNK):
            gc = gates[c * DN_CHUNK:(c + 1) * DN_CHUNK]
            cum_f = _dot(tri_f, gc, HIGHEST)
            cum_b = _dot(tri_b, gc, HIGHEST)
            col = jnp.where(lane_c < DN_HEADS, cum_f, jnp.where(lane_c < 2 * DN_HEADS, cum_b, gc))
            gcol_ref[0, c * DN_CHUNK:(c + 1) * DN_CHUNK, :] = col
            grow_ref[0, c] = col.T


def _dn_project(xc, mod, w16, conv_w, alog_row, dt_row, n_lat):
    batch, n_tot, d = xc.shape
    n_cols = w16.shape[1]
    n_tiles = n_cols // COL_TILE
    hk = DN_HEADS * DN_DK
    n_qkv_tiles = 3 * hk // COL_TILE
    n_z_tiles = hk // COL_TILE
    assert n_tiles == n_qkv_tiles + n_z_tiles + 1
    nc = n_tot // DN_CHUNK
    kern = functools.partial(_dn_proj_kernel, n_lat=n_lat, n_tot=n_tot,
                             n_qkv_tiles=n_qkv_tiles, n_z_tiles=n_z_tiles)
    return pl.pallas_call(
        kern,
        grid=(batch, n_tiles),
        in_specs=[pl.BlockSpec((1, n_tot, d), lambda b, j: (b, 0, 0))]
        + _mod_specs(batch, d, 1, 0)
        + [
            pl.BlockSpec((d, COL_TILE), lambda b, j: (0, j)),
            pl.BlockSpec((DN_CONV, COL_TILE), lambda b, j: (0, jnp.minimum(j, n_qkv_tiles - 1))),
            pl.BlockSpec((1, LANES), lambda b, j: (0, 0)),
            pl.BlockSpec((1, LANES), lambda b, j: (0, 0)),
        ],
        out_specs=[
            pl.BlockSpec((1, n_tot, COL_TILE), lambda b, j: (b, 0, jnp.minimum(j, n_qkv_tiles - 1))),
            pl.BlockSpec((1, n_tot, COL_TILE),
                         lambda b, j: (b, 0, jnp.clip(j - n_qkv_tiles, 0, n_z_tiles - 1))),
            pl.BlockSpec((1, n_tot, LANES), lambda b, j: (b, 0, 0)),
            pl.BlockSpec((1, nc, LANES, DN_CHUNK), lambda b, j: (b, 0, 0, 0)),
        ],
        out_shape=[
            jax.ShapeDtypeStruct((batch, n_tot, 3 * hk), F32),
            jax.ShapeDtypeStruct((batch, n_tot, hk), F32),
            jax.ShapeDtypeStruct((batch, n_tot, LANES), F32),
            jax.ShapeDtypeStruct((batch, nc, LANES, DN_CHUNK), F32),
        ],
        scratch_shapes=[
            pltpu.VMEM((n_tot, d), BF16),
            pltpu.VMEM((n_tot + 3 * SUBLANES, COL_TILE), F32),
        ],
        compiler_params=_cparams(("arbitrary", "arbitrary"), 56),
        name="dn_project",
    )(xc, mod, mod, mod, mod, w16, conv_w, alog_row, dt_row)


def _dn_scan_kernel(q_ref, k_ref, v_ref, gcol_ref, grow_ref, o_ref, s_ref, *, n_lat, n_tot):
    hb = pl.program_id(1)
    nc = n_tot // DN_CHUNK
    nc_lat = n_lat // DN_CHUNK
    nc_ctx = nc - nc_lat
    c = DN_CHUNK
    ri = lax.broadcasted_iota(jnp.int32, (c, c), 0)
    ci = lax.broadcasted_iota(jnp.int32, (c, c), 1)
    eye = (ri == ci).astype(F32)
    lane = lax.broadcasted_iota(jnp.int32, (c, LANES), 1)

    o_ref[...] = jnp.zeros(o_ref.shape, F32)
    s_ref[...] = jnp.zeros(s_ref.shape, F32)

    def advance(chunk, d, hh):
        r0 = pl.multiple_of(chunk * c, c)
        cols = slice(hh * DN_DK, (hh + 1) * DN_DK)
        head = hb * SCAN_HEADS + hh
        q = q_ref[0, pl.ds(r0, c), cols]
        k = k_ref[0, pl.ds(r0, c), cols]
        v = v_ref[0, pl.ds(r0, c), cols]
        gates = gcol_ref[0, pl.ds(r0, c), :]
        g_lane = d * DN_HEADS + head
        gcum = jnp.sum(jnp.where(lane == g_lane, gates, 0.0), axis=-1, keepdims=True)
        beta = jnp.sum(jnp.where(lane == 2 * DN_HEADS + g_lane, gates, 0.0), axis=-1, keepdims=True)
        grow = grow_ref[0, chunk, pl.ds(g_lane, 1), :]
        if d == 0:
            incl, strict = ri >= ci, ri > ci
            g_last = grow[:, c - 1:c]
        else:
            incl, strict = ri <= ci, ri < ci
            g_last = grow[:, 0:1]
        gamma = jnp.where(incl, jnp.exp(jnp.where(incl, gcum - grow, 0.0)), 0.0)
        k16 = k.astype(BF16)
        kb = k * beta
        a = jnp.where(strict, _dot_nt(kb.astype(BF16), k16) * gamma, 0.0)
        inv = eye - a
        pw = a
        for _ in range(int(math.log2(c)) - 1):
            pw = _dot(pw, pw, HIGHEST)
            inv = inv + _dot(inv, pw, HIGHEST)
        decay = jnp.exp(gcum)
        rhs = jnp.concatenate([v * beta, kb * decay], axis=-1)
        uw = _dot(inv, rhs, HIGHEST)
        u = uw[:, :DN_DV]
        w = uw[:, DN_DV:]
        qk = _dot_nt(q.astype(BF16), k16) * gamma
        q_dec = q * decay
        k_dec = k * jnp.exp(g_last - gcum)

        idx = d * SCAN_HEADS + hh
        s = s_ref[idx]
        s16 = s.astype(BF16)
        v_new = u - _dot(w.astype(BF16), s16)
        v_new16 = v_new.astype(BF16)
        o = _dot(q_dec.astype(BF16), s16) + _dot(qk.astype(BF16), v_new16)
        s_ref[idx] = s * jnp.exp(g_last) + _dot_tn(k_dec.astype(BF16), v_new16)
        o_ref[0, pl.ds(r0, c), cols] += o

    def body(i, carry):
        chunk_f = jnp.where(i < nc_ctx, i + nc_lat, i - nc_ctx)
        chunk_b = nc - 1 - i
        for hh in range(SCAN_HEADS):
            advance(chunk_f, 0, hh)
            advance(chunk_b, 1, hh)
        return carry

    lax.fori_loop(0, nc, body, 0)


def _dn_scan(qkv, gcol, grow, n_lat):
    batch, n_tot, _ = qkv.shape
    hk = DN_HEADS * DN_DK
    wblk = SCAN_HEADS * DN_DK
    nb = hk // wblk
    nc = n_tot // DN_CHUNK
    kern = functools.partial(_dn_scan_kernel, n_lat=n_lat, n_tot=n_tot)
    return pl.pallas_call(
        kern,
        grid=(batch, nb),
        in_specs=[
            pl.BlockSpec((1, n_tot, wblk), lambda b, h: (b, 0, h)),
            pl.BlockSpec((1, n_tot, wblk), lambda b, h: (b, 0, nb + h)),
            pl.BlockSpec((1, n_tot, wblk), lambda b, h: (b, 0, 2 * nb + h)),
            pl.BlockSpec((1, n_tot, LANES), lambda b, h: (b, 0, 0)),
            pl.BlockSpec((1, nc, LANES, DN_CHUNK), lambda b, h: (b, 0, 0, 0)),
        ],
        out_specs=pl.BlockSpec((1, n_tot, wblk), lambda b, h: (b, 0, h)),
        out_shape=jax.ShapeDtypeStruct((batch, n_tot, hk), F32),
        scratch_shapes=[pltpu.VMEM((2 * SCAN_HEADS, DN_DK, DN_DV), F32)],
        compiler_params=_cparams(("arbitrary", "arbitrary"), 48),
        name="dn_scan",
    )(qkv, qkv, qkv, gcol, grow)


def _out_ln_kernel(y_ref, z_ref, ng_ref, x_ref, gt_ref, w_ref, lg_ref, lb_ref, o_ref, yb_ref,
                   *, alpha, gated_norm):
    if gated_norm:
        for h0 in range(0, y_ref.shape[2], DN_DV):
            yh = y_ref[0, :, h0:h0 + DN_DV]
            ms = jnp.mean(yh * yh, axis=-1, keepdims=True)
            yh = yh * lax.rsqrt(ms + RMS_EPS) * ng_ref[...]
            yb_ref[:, h0:h0 + DN_DV] = (yh * _silu(z_ref[0, :, h0:h0 + DN_DV])).astype(BF16)
    else:
        yb_ref[...] = y_ref[0].astype(BF16)
    proj = _dot(yb_ref[...], w_ref[...])
    o_ref[0] = _layernorm(alpha * x_ref[0] + gt_ref[0] * proj, lg_ref[...], lb_ref[...])


def _out_ln(y, z, norm_g, xc, mod, gt_idx, w16, ln_g, ln_b, alpha, n_lat, n_out, gated_norm):
    batch, _, d = xc.shape
    hv = y.shape[2]
    n_lat_tiles = n_lat // ROW_TILE
    kern = functools.partial(_out_ln_kernel, alpha=alpha, gated_norm=gated_norm)
    tok = lambda width: pl.BlockSpec((1, ROW_TILE, width), lambda b, t: (b, t, 0))
    return pl.pallas_call(
        kern,
        grid=(batch, n_out // ROW_TILE),
        in_specs=[
            tok(hv), tok(hv),
            pl.BlockSpec((1, norm_g.shape[1]), lambda b, t: (0, 0)),
            tok(d),
            pl.BlockSpec((1, 1, d), lambda b, t: (jnp.where(t < n_lat_tiles, b, batch), 0, gt_idx)),
            pl.BlockSpec(memory_space=pltpu.VMEM),
            pl.BlockSpec((1, d), lambda b, t: (0, 0)),
            pl.BlockSpec((1, d), lambda b, t: (0, 0)),
        ],
        out_specs=tok(d),
        out_shape=jax.ShapeDtypeStruct((batch, n_out, d), F32),
        scratch_shapes=[pltpu.VMEM((ROW_TILE, hv), BF16)],
        compiler_params=_cparams(("arbitrary", "arbitrary"), 40),
        name="out_ln",
    )(y, z, norm_g, xc, mod, w16, ln_g, ln_b)


def _ffn_kernel(x_ref, sc_ref, sh_ref, gt_ref, wg_ref, wu_ref, wd_ref, lg_ref, lb_ref, o_ref, *, alpha):
    x = x_ref[0]
    h = (x * (1.0 + sc_ref[0]) + sh_ref[0]).astype(BF16)
    gate = _dot(h, wg_ref[...])
    up = _dot(h, wu_ref[...])
    act = (_silu(gate) * up).astype(BF16)
    y = _dot(act, wd_ref[...])
    o_ref[0] = _layernorm(alpha * x + gt_ref[0] * y, lg_ref[...], lb_ref[...])


def _ffn(xc, mod, wg16, wu16, wd16, ln_g, ln_b, alpha, n_lat, n_out):
    batch, _, d = xc.shape
    n_lat_tiles = n_lat // ROW_TILE
    row = lambda b, t: jnp.where(t < n_lat_tiles, b, batch)
    kern = functools.partial(_ffn_kernel, alpha=alpha)
    tok = pl.BlockSpec((1, ROW_TILE, d), lambda b, t: (b, t, 0))
    resident = pl.BlockSpec(memory_space=pltpu.VMEM)
    return pl.pallas_call(
        kern,
        grid=(batch, n_out // ROW_TILE),
        in_specs=[
            tok,
            pl.BlockSpec((1, 1, d), lambda b, t: (row(b, t), 0, 4)),
            pl.BlockSpec((1, 1, d), lambda b, t: (row(b, t), 0, 3)),
            pl.BlockSpec((1, 1, d), lambda b, t: (row(b, t), 0, 5)),
            resident, resident, resident,
            pl.BlockSpec((1, d), lambda b, t: (0, 0)),
            pl.BlockSpec((1, d), lambda b, t: (0, 0)),
        ],
        out_specs=tok,
        out_shape=jax.ShapeDtypeStruct((batch, n_out, d), F32),
        compiler_params=_cparams(("arbitrary", "arbitrary"), 52),
        name="ffn",
    )(xc, mod, mod, mod, wg16, wu16, wd16, ln_g, ln_b)


def _da_proj_kernel(x_ref, sc_l, sh_l, sc_c, sh_c, w_ref, cos_ref, sin_a_ref, sin_b_ref,
                    o_ref, xb_ref, *, n_lat, n_qk_tiles):
    j = pl.program_id(1)

    @pl.when(j == 0)
    def _():
        _modulate_into(xb_ref, x_ref, sc_l, sh_l, sc_c, sh_c, n_lat)

    acc = _dot(xb_ref[...], w_ref[...])

    @pl.when(j < n_qk_tiles)
    def _():
        scale = jnp.where(j < n_qk_tiles // 2, DA_DIM ** -0.5, 1.0)
        quarter = DA_DIM // 4
        for h0 in range(0, COL_TILE, LANES):
            xh = acc[:, h0:h0 + LANES]
            rot = (pltpu.roll(xh, quarter, 1) * sin_a_ref[...]
                   + pltpu.roll(xh, LANES - quarter, 1) * sin_b_ref[...])
            o_ref[0, :, h0:h0 + LANES] = ((xh * cos_ref[...] + rot) * scale).astype(BF16)

    @pl.when(j >= n_qk_tiles)
    def _():
        o_ref[0] = acc.astype(BF16)


def _da_project(xc, mod, w16, cos_t, sin_a, sin_b, n_lat):
    batch, n_tot, d = xc.shape
    n_cols = w16.shape[1]
    n_tiles = n_cols // COL_TILE
    n_qk_tiles = 2 * n_tiles // 3
    kern = functools.partial(_da_proj_kernel, n_lat=n_lat, n_qk_tiles=n_qk_tiles)
    table = pl.BlockSpec((n_tot, LANES), lambda b, j: (0, 0))
    return pl.pallas_call(
        kern,
        grid=(batch, n_tiles),
        in_specs=[pl.BlockSpec((1, n_tot, d), lambda b, j: (b, 0, 0))]
        + _mod_specs(batch, d, 1, 0)
        + [pl.BlockSpec((d, COL_TILE), lambda b, j: (0, j)), table, table, table],
        out_specs=pl.BlockSpec((1, n_tot, COL_TILE), lambda b, j: (b, 0, j)),
        out_shape=jax.ShapeDtypeStruct((batch, n_tot, n_cols), BF16),
        scratch_shapes=[pltpu.VMEM((n_tot, d), BF16)],
        compiler_params=_cparams(("arbitrary", "arbitrary"), 56),
        name="da_project",
    )(xc, mod, mod, mod, mod, w16, cos_t, sin_a, sin_b)


def _da_attn_kernel(q_ref, k_ref, v_ref, lam_ref, g_ref, o_ref, *, lambda_init):
    lam = lam_ref[...]
    lam_val = (jnp.exp(jnp.sum(lam[0:1] * lam[1:2], axis=-1, keepdims=True))
               - jnp.exp(jnp.sum(lam[2:3] * lam[3:4], axis=-1, keepdims=True)) + lambda_init)
    q = q_ref[0]
    k = k_ref[0]
    lane = lax.broadcasted_iota(jnp.int32, q.shape, 1)
    zero = jnp.zeros_like(q)
    probs = []
    for comp in range(2):
        in_comp = (lane < DA_DIM) if comp == 0 else (lane >= DA_DIM)
        s = _dot_nt(jnp.where(in_comp, q, zero), k)
        e = jnp.exp(s - jnp.max(s, axis=-1, keepdims=True))
        probs.append((e, jnp.sum(e, axis=-1, keepdims=True)))
    (e0, l0), (e1, l1) = probs
    a = e0 * (1.0 / l0) - e1 * (lam_val / l1)
    o = _dot(a.astype(BF16), v_ref[0])
    ms = jnp.mean(o * o, axis=-1, keepdims=True)
    o_ref[0] = o * lax.rsqrt(ms + RMS_EPS) * g_ref[...] * (1.0 - lambda_init)


def _da_attention(qkv16, lam, subln_g, lambda_init, n_lat):
    batch, n_tot, _ = qkv16.shape
    hw = 2 * DA_DIM
    kern = functools.partial(_da_attn_kernel, lambda_init=lambda_init)
    return pl.pallas_call(
        kern,
        grid=(batch, DA_HEADS, n_lat // ATT_Q_TILE),
        in_specs=[
            pl.BlockSpec((1, ATT_Q_TILE, hw), lambda b, h, i: (b, i, h)),
            pl.BlockSpec((1, n_tot, hw), lambda b, h, i: (b, 0, DA_HEADS + h)),
            pl.BlockSpec((1, n_tot, hw), lambda b, h, i: (b, 0, 2 * DA_HEADS + h)),
            pl.BlockSpec(lam.shape, lambda b, h, i: (0, 0)),
            pl.BlockSpec((1, hw), lambda b, h, i: (0, 0)),
        ],
        out_specs=pl.BlockSpec((1, ATT_Q_TILE, hw), lambda b, h, i: (b, i, h)),
        out_shape=jax.ShapeDtypeStruct((batch, n_lat, DA_HEADS * hw), F32),
        compiler_params=_cparams(("arbitrary", "arbitrary", "arbitrary"), 48),
        name="da_attention",
    )(qkv16, qkv16, qkv16, lam, subln_g)


def _rope_tables(n_lat, n_tot):
    quarter = DA_DIM // 4
    inv_freq = ROPE_BASE ** (-jnp.arange(quarter, dtype=F32) / quarter)
    rows = n_lat // GRID_W
    row = jnp.repeat(jnp.arange(rows, dtype=F32), GRID_W)
    col = jnp.tile(jnp.arange(GRID_W, dtype=F32), rows)
    ang_r = row[:, None] * inv_freq
    ang_c = col[:, None] * inv_freq
    ang = jnp.concatenate([ang_r, ang_r, ang_c, ang_c], axis=-1)
    reps = LANES // DA_DIM
    cos = jnp.tile(jnp.cos(ang), (1, reps))
    sin = jnp.tile(jnp.sin(ang), (1, reps))
    upper = (jnp.arange(LANES) % (2 * quarter)) >= quarter
    sin_a = jnp.where(upper, sin, 0.0)
    sin_b = jnp.where(upper, 0.0, -sin)
    n_ctx = n_tot - n_lat
    ext = lambda t, fill: jnp.concatenate([t, jnp.full((n_ctx, LANES), fill, F32)], axis=0)
    return ext(cos, 1.0), ext(sin_a, 0.0), ext(sin_b, 0.0)


def kernel(x, c, ctx, c_ctx, ada_w, ada_b, ln1_g, ln1_b, ln2_g, ln2_b, ffn_w_gate, ffn_w_up, ffn_w_down,
           dn_w_in, dn_conv, dn_a_log, dn_dt_bias, dn_norm_g, dn_w_out, da_w_in, da_lambda, da_subln_g,
           da_w_out):
    batch, n_lat, d = x.shape
    n_ctx = ctx.shape[1]
    n_tot = n_lat + n_ctx
    depth = ada_w.shape[0]
    assert depth == 2, "layer 0 is gated DeltaNet, layer 1 (the last) differential attention"
    assert batch < MOD_ROWS and n_lat % ROW_TILE == 0 and n_ctx % ROW_TILE == 0
    alpha = (2.0 * depth) ** 0.25
    hk = DN_HEADS * DN_DK

    cs = jnp.concatenate([c, c_ctx[None, :], jnp.zeros((MOD_ROWS - batch - 1, d), F32)], axis=0)
    mods = _ada_modulation(cs, ada_w, ada_b).reshape(depth, MOD_ROWS, 1, 6 * d)

    xc = jnp.concatenate([x, ctx], axis=1)

    mod = mods[0]
    gate_cols = dn_w_in[0][:, 4 * hk:]
    w_in = jnp.concatenate(
        [dn_w_in[0][:, :4 * hk], gate_cols, jnp.zeros((d, COL_TILE - gate_cols.shape[1]), F32)], axis=1)
    lane_pad = lambda v: jnp.concatenate([v.reshape(1, -1), jnp.zeros((1, LANES - v.size), F32)], axis=1)
    qkv, z, gcol, grow = _dn_project(xc, mod, w_in.astype(BF16), dn_conv[0],
                                     lane_pad(dn_a_log[0]), lane_pad(dn_dt_bias[0]), n_lat)
    o = _dn_scan(qkv, gcol, grow, n_lat)
    xc = _out_ln(o, z, dn_norm_g[0].reshape(1, -1), xc, mod, 2, dn_w_out[0].astype(BF16),
                 ln1_g[0].reshape(1, -1), ln1_b[0].reshape(1, -1), alpha, n_lat, n_tot, True)
    xc = _ffn(xc, mod, ffn_w_gate[0].astype(BF16), ffn_w_up[0].astype(BF16), ffn_w_down[0].astype(BF16),
              ln2_g[0].reshape(1, -1), ln2_b[0].reshape(1, -1), alpha, n_lat, n_tot)

    mod = mods[1]
    lambda_init = 0.8 - 0.6 * math.exp(-0.3 * 1)
    cos_t, sin_a, sin_b = _rope_tables(n_lat, n_tot)
    qkv16 = _da_project(xc, mod, da_w_in[0].astype(BF16), cos_t, sin_a, sin_b, n_lat)
    y = _da_attention(qkv16, da_lambda[0], da_subln_g[0].reshape(1, -1), lambda_init, n_lat)
    xl = _out_ln(y, y, da_subln_g[0].reshape(1, -1), xc, mod, 2, da_w_out[0].astype(BF16),
                 ln1_g[1].reshape(1, -1), ln1_b[1].reshape(1, -1), alpha, n_lat, n_lat, False)
    return _ffn(xl, mod, ffn_w_gate[1].astype(BF16), ffn_w_up[1].astype(BF16), ffn_w_down[1].astype(BF16),
                ln2_g[1].reshape(1, -1), ln2_b[1].reshape(1, -1), alpha, n_lat, n_lat)
```

```python
import functools
import math

import jax
import jax.numpy as jnp
from jax import lax
from jax.experimental import pallas as pl
from jax.experimental.pallas import tpu as pltpu

F32 = jnp.float32
BF16 = jnp.bfloat16
HIGHEST = lax.Precision.HIGHEST

DN_HEADS = 8
DN_DK = 128
DN_DV = 128
DN_CONV = 5
DN_CHUNK = 64
DA_HEADS = 8
DA_DIM = 64
GRID_W = 64
ROPE_BASE = 10000.0
LN_EPS = 1e-5
RMS_EPS = 1e-6

LANES = 128
SUBLANES = 8
MOD_ROWS = 16
ROW_TILE = 256
COL_TILE = 256
ADA_COL_TILE = 1536
SCAN_HEADS = 4
ATT_Q_TILE = 1024
ATT_SUB_ROWS = 256
LOG2E = math.log2(math.e)


def _cparams(dims, vmem_mb):
    return pltpu.CompilerParams(dimension_semantics=dims, vmem_limit_bytes=vmem_mb * 1024 * 1024)


def _silu(x):
    return x * jax.nn.sigmoid(x)


def _dot(a, b, precision=None):
    return jnp.dot(a, b, preferred_element_type=F32, precision=precision)


def _dot_nt(a, b):
    return lax.dot_general(a, b, (((1,), (1,)), ((), ())), preferred_element_type=F32)


def _dot_tn(a, b):
    return lax.dot_general(a, b, (((0,), (0,)), ((), ())), preferred_element_type=F32)


def _layernorm(x, g, b):
    mu = jnp.mean(x, axis=-1, keepdims=True)
    xc = x - mu
    var = jnp.mean(xc * xc, axis=-1, keepdims=True)
    return xc * lax.rsqrt(var + LN_EPS) * g + b


def _ada_kernel(cs_ref, w_ref, b_ref, o_ref):
    s = _silu(cs_ref[...]).astype(BF16)
    o_ref[0] = _dot(s, w_ref[0].astype(BF16)) + b_ref[0]


def _ada_modulation(cs, ada_w, ada_b):
    depth, d, n = ada_w.shape
    tn = ADA_COL_TILE if n % ADA_COL_TILE == 0 else n
    return pl.pallas_call(
        _ada_kernel,
        grid=(depth, n // tn),
        in_specs=[
            pl.BlockSpec((MOD_ROWS, d), lambda i, j: (0, 0)),
            pl.BlockSpec((1, d, tn), lambda i, j: (i, 0, j)),
            pl.BlockSpec((1, 1, tn), lambda i, j: (i, 0, j)),
        ],
        out_specs=pl.BlockSpec((1, MOD_ROWS, tn), lambda i, j: (i, 0, j)),
        out_shape=jax.ShapeDtypeStruct((depth, MOD_ROWS, n), F32),
        compiler_params=_cparams(("arbitrary", "arbitrary"), 40),
        name="ada_modulation",
    )(cs, ada_w, ada_b.reshape(depth, 1, n))


def _modulate_into(xb_ref, x_ref, sc_l, sh_l, sc_c, sh_c, n_lat):
    xb_ref[:n_lat, :] = (x_ref[0, :n_lat, :] * (1.0 + sc_l[0]) + sh_l[0]).astype(BF16)
    xb_ref[n_lat:, :] = (x_ref[0, n_lat:, :] * (1.0 + sc_c[0]) + sh_c[0]).astype(BF16)


def _mod_specs(batch, d, sc_idx, sh_idx):
    return [
        pl.BlockSpec((1, 1, d), lambda b, j: (b, 0, sc_idx)),
        pl.BlockSpec((1, 1, d), lambda b, j: (b, 0, sh_idx)),
        pl.BlockSpec((1, 1, d), lambda b, j: (batch, 0, sc_idx)),
        pl.BlockSpec((1, 1, d), lambda b, j: (batch, 0, sh_idx)),
    ]


def _dn_proj_kernel(x_ref, sc_l, sh_l, sc_c, sh_c, w_ref, conv_ref, alog_ref, dt_ref,
                    qkv_ref, z_ref, gcol_ref, grow_ref, xb_ref, pad_ref,
                    *, n_lat, n_tot, n_qkv_tiles, n_z_tiles):
    j = pl.program_id(1)
    n_ctx = n_tot - n_lat
    halo = SUBLANES
    pad = (DN_CONV - 1) // 2
    lat0 = halo
    ctx0 = 2 * halo + n_lat

    @pl.when(j == 0)
    def _():
        _modulate_into(xb_ref, x_ref, sc_l, sh_l, sc_c, sh_c, n_lat)

    acc = _dot(xb_ref[...], w_ref[...])

    @pl.when(j < n_qkv_tiles)
    def _():
        zeros = jnp.zeros((halo, COL_TILE), F32)
        pad_ref[0:halo, :] = zeros
        pad_ref[lat0 + n_lat:ctx0, :] = zeros
        pad_ref[ctx0 + n_ctx:ctx0 + n_ctx + halo, :] = zeros
        pad_ref[lat0:lat0 + n_lat, :] = acc[:n_lat]
        pad_ref[ctx0:ctx0 + n_ctx, :] = acc[n_lat:]
        cw = conv_ref[...]
        n_qk_tiles = 2 * n_qkv_tiles // 3
        is_q = j < n_qk_tiles // 2
        is_qk = j < n_qk_tiles
        for base, n_rows, out0 in ((lat0, n_lat, 0), (ctx0, n_ctx, n_lat)):
            y = None
            for t in range(DN_CONV):
                term = pad_ref[base + t - pad:base + t - pad + n_rows, :] * cw[t:t + 1, :]
                y = term if y is None else y + term
            y = _silu(y)
            for h0 in range(0, COL_TILE, DN_DK):
                yh = y[:, h0:h0 + DN_DK]
                ss = jnp.sum(yh * yh, axis=-1, keepdims=True)
                fac = lax.rsqrt(ss + RMS_EPS)
                fac = jnp.where(is_q, fac * (DN_DK ** -0.5), fac)
                fac = jnp.where(is_qk, fac, jnp.ones_like(fac))
                qkv_ref[0, out0:out0 + n_rows, h0:h0 + DN_DK] = yh * fac

    @pl.when(jnp.logical_and(j >= n_qkv_tiles, j < n_qkv_tiles + n_z_tiles))
    def _():
        z_ref[0] = acc

    @pl.when(j == n_qkv_tiles + n_z_tiles)
    def _():
        t = acc[:, :LANES]
        lane = lax.broadcasted_iota(jnp.int32, t.shape, 1)
        x = t + dt_ref[...]
        softplus = jnp.maximum(x, 0.0) + jnp.log1p(jnp.exp(-jnp.abs(x)))
        decay = -jnp.exp(alog_ref[...]) * softplus
        gates = jnp.where(lane < 2 * DN_HEADS, decay, jax.nn.sigmoid(t))
        ri = lax.broadcasted_iota(jnp.int32, (DN_CHUNK, DN_CHUNK), 0)
        ci = lax.broadcasted_iota(jnp.int32, (DN_CHUNK, DN_CHUNK), 1)
        tri_f = (ri >= ci).astype(F32)
        tri_b = (ri <= ci).astype(F32)
        lane_c = lax.broadcasted_iota(jnp.int32, (DN_CHUNK, LANES), 1)
        for c in range(n_tot // DN_CHUNK):
            gc = gates[c * DN_CHUNK:(c + 1) * DN_CHUNK]
            cum_f = _dot(tri_f, gc, HIGHEST)
            cum_b = _dot(tri_b, gc, HIGHEST)
            col = jnp.where(lane_c < DN_HEADS, cum_f, jnp.where(lane_c < 2 * DN_HEADS, cum_b, gc))
            gcol_ref[0, c * DN_CHUNK:(c + 1) * DN_CHUNK, :] = col
            grow_ref[0, c] = col.T


def _dn_project(xc, mod, w16, conv_w, alog_row, dt_row, n_lat):
    batch, n_tot, d = xc.shape
    n_cols = w16.shape[1]
    n_tiles = n_cols // COL_TILE
    hk = DN_HEADS * DN_DK
    n_qkv_tiles = 3 * hk // COL_TILE
    n_z_tiles = hk // COL_TILE
    assert n_tiles == n_qkv_tiles + n_z_tiles + 1
    nc = n_tot // DN_CHUNK
    kern = functools.partial(_dn_proj_kernel, n_lat=n_lat, n_tot=n_tot,
                             n_qkv_tiles=n_qkv_tiles, n_z_tiles=n_z_tiles)
    return pl.pallas_call(
        kern,
        grid=(batch, n_tiles),
        in_specs=[pl.BlockSpec((1, n_tot, d), lambda b, j: (b, 0, 0))]
        + _mod_specs(batch, d, 1, 0)
        + [
            pl.BlockSpec((d, COL_TILE), lambda b, j: (0, j)),
            pl.BlockSpec((DN_CONV, COL_TILE), lambda b, j: (0, jnp.minimum(j, n_qkv_tiles - 1))),
            pl.BlockSpec((1, LANES), lambda b, j: (0, 0)),
            pl.BlockSpec((1, LANES), lambda b, j: (0, 0)),
        ],
        out_specs=[
            pl.BlockSpec((1, n_tot, COL_TILE), lambda b, j: (b, 0, jnp.minimum(j, n_qkv_tiles - 1))),
            pl.BlockSpec((1, n_tot, COL_TILE),
                         lambda b, j: (b, 0, jnp.clip(j - n_qkv_tiles, 0, n_z_tiles - 1))),
            pl.BlockSpec((1, n_tot, LANES), lambda b, j: (b, 0, 0)),
            pl.BlockSpec((1, nc, LANES, DN_CHUNK), lambda b, j: (b, 0, 0, 0)),
        ],
        out_shape=[
            jax.ShapeDtypeStruct((batch, n_tot, 3 * hk), F32),
            jax.ShapeDtypeStruct((batch, n_tot, hk), F32),
            jax.ShapeDtypeStruct((batch, n_tot, LANES), F32),
            jax.ShapeDtypeStruct((batch, nc, LANES, DN_CHUNK), F32),
        ],
        scratch_shapes=[
            pltpu.VMEM((n_tot, d), BF16),
            pltpu.VMEM((n_tot + 3 * SUBLANES, COL_TILE), F32),
        ],
        compiler_params=_cparams(("arbitrary", "arbitrary"), 56),
        name="dn_project",
    )(xc, mod, mod, mod, mod, w16, conv_w, alog_row, dt_row)


SCAN_INSTANCES = tuple((hh, d) for hh in range(SCAN_HEADS) for d in range(2))
INV_BASE = 16


def _dot16(a, b):
    return _dot(a.astype(BF16), b.astype(BF16))


def _dn_local_stages(q_ref, k_ref, v_ref, gcol_ref, grow_ref, slot, chunks, hb):
    u_ref, wq_ref, m_ref, e_ref = slot
    c = DN_CHUNK
    st = [dict() for _ in SCAN_INSTANCES]

    def each(fn):
        def run():
            for (hh, d), t in zip(SCAN_INSTANCES, st):
                fn(t, hh, d)
        return run

    def idx2():
        return (lax.broadcasted_iota(jnp.int32, (c, c), 0), lax.broadcasted_iota(jnp.int32, (c, c), 1))

    def load(t, hh, d):
        chunk = chunks[d]
        ri, ci = idx2()
        lane = lax.broadcasted_iota(jnp.int32, (c, LANES), 1)
        r0 = pl.multiple_of(chunk * c, c)
        cols = slice(hh * DN_DK, (hh + 1) * DN_DK)
        q = q_ref[0, pl.ds(r0, c), cols]
        k = k_ref[0, pl.ds(r0, c), cols]
        v = v_ref[0, pl.ds(r0, c), cols]
        gates = gcol_ref[0, pl.ds(r0, c), :]
        g_lane = d * DN_HEADS + hb * SCAN_HEADS + hh
        gcum = jnp.sum(jnp.where(lane == g_lane, gates, 0.0), axis=-1, keepdims=True)
        beta = jnp.sum(jnp.where(lane == 2 * DN_HEADS + g_lane, gates, 0.0), axis=-1, keepdims=True)
        grow = grow_ref[0, chunk, pl.ds(g_lane, 1), :]
        incl = (ri >= ci) if d == 0 else (ri <= ci)
        g_last = grow[:, c - 1:c] if d == 0 else grow[:, 0:1]
        t["gamma"] = jnp.where(incl, jnp.exp(jnp.where(incl, gcum - grow, 0.0)), 0.0)
        kb = k * beta
        decay = jnp.exp(gcum)
        t["k16"] = k.astype(BF16)
        t["kbq16"] = jnp.concatenate([kb.astype(BF16), q.astype(BF16)], axis=0)
        t["rhs16"] = jnp.concatenate([v * beta, kb * decay], axis=-1).astype(BF16)
        idx = d * SCAN_HEADS + hh
        wq_ref[idx, c:, :] = (q * decay).astype(BF16)
        m_ref[idx, c:, :] = (k * jnp.exp(g_last - gcum)).T.astype(BF16)
        e_ref[idx] = jnp.broadcast_to(jnp.exp(g_last), (SUBLANES, LANES))

    def gram(t, hh, d):
        t["kq"] = _dot_nt(t.pop("kbq16"), t.pop("k16"))

    def split(t, hh, d):
        ri, ci = idx2()
        strict = (ri > ci) if d == 0 else (ri < ci)
        kq = t.pop("kq")
        gamma = t.pop("gamma")
        a = jnp.where(strict, kq[:c] * gamma, 0.0)
        m_ref[d * SCAN_HEADS + hh, :c, :] = (kq[c:] * gamma).astype(BF16)
        same = (ri // INV_BASE) == (ci // INV_BASE)
        a_diag = jnp.where(same, a, 0.0)
        t["a"] = a
        t["inv"] = (ri == ci).astype(F32) - a_diag
        t["pw"] = _dot16(a_diag, a_diag)

    def series(t, hh, d):
        pw16 = t["pw"].astype(BF16)
        both = _dot(jnp.concatenate([pw16, t["inv"].astype(BF16)], axis=0), pw16)
        t["pw"] = both[:c]
        t["inv"] = t["inv"] + both[c:]

    def series_last(t, hh, d):
        t["inv"] = t["inv"] + _dot16(t["inv"], t.pop("pw"))

    def merge_a(blk):
        def fn(t, hh, d):
            ri, ci = idx2()
            off = jnp.logical_and((ri // (2 * blk)) == (ci // (2 * blk)), (ri // blk) != (ci // blk))
            t["y"] = _dot16(jnp.where(off, t["a"], 0.0), t["inv"])
        return fn

    def merge_b(t, hh, d):
        t["inv"] = t["inv"] - _dot16(t["inv"], t.pop("y"))

    def solve(t, hh, d):
        t["uw"] = _dot(t.pop("inv").astype(BF16), t.pop("rhs16"))
        t.pop("a")

    def store(t, hh, d):
        idx = d * SCAN_HEADS + hh
        uw = t.pop("uw")
        u_ref[idx] = uw[:, :DN_DV]
        wq_ref[idx, :c, :] = uw[:, DN_DV:].astype(BF16)

    stages = [load, gram, split] + [series] * (int(math.log2(INV_BASE)) - 2) + [series_last]
    blk = INV_BASE
    while blk < c:
        stages += [merge_a(blk), merge_b]
        blk *= 2
    stages += [solve, store]
    return [each(fn) for fn in stages]


def _dn_serial_stages(o_ref, s_ref, slot, chunks):
    u_ref, wq_ref, m_ref, e_ref = slot
    c = DN_CHUNK
    st = [dict() for _ in SCAN_INSTANCES]

    def each(fn):
        def run():
            for (hh, d), t in zip(SCAN_INSTANCES, st):
                fn(t, d * SCAN_HEADS + hh, hh, d)
        return run

    def state_dot(t, idx, hh, d):
        t["s"] = s_ref[idx]
        t["ws"] = _dot(wq_ref[idx], t["s"].astype(BF16))

    def value_dot(t, idx, hh, d):
        v_new = u_ref[idx] - t["ws"][:c]
        t["mv"] = _dot(m_ref[idx], v_new.astype(BF16))

    def update(t, idx, hh, d):
        mv = t.pop("mv")
        s_ref[idx] = t.pop("s") * e_ref[idx][0:1, :] + mv[c:]
        r0 = pl.multiple_of(chunks[d] * c, c)
        o_ref[0, pl.ds(r0, c), hh * DN_DV:(hh + 1) * DN_DV] += t.pop("ws")[c:] + mv[:c]

    return [each(fn) for fn in (state_dot, value_dot, update)]


def _dn_scan_kernel(q_ref, k_ref, v_ref, gcol_ref, grow_ref, o_ref, s_ref, *slots, n_lat, n_tot):
    hb = pl.program_id(1)
    nc = n_tot // DN_CHUNK
    nc_lat = n_lat // DN_CHUNK
    nc_ctx = nc - nc_lat
    slot_a, slot_b = slots[:4], slots[4:]

    o_ref[...] = jnp.zeros(o_ref.shape, F32)
    s_ref[...] = jnp.zeros(s_ref.shape, F32)

    def chunk_of(i, d):
        i = jnp.minimum(i, nc - 1)
        return jnp.where(i < nc_ctx, i + nc_lat, i - nc_ctx) if d == 0 else nc - 1 - i

    def local(i, slot):
        chunks = (chunk_of(i, 0), chunk_of(i, 1))
        return _dn_local_stages(q_ref, k_ref, v_ref, gcol_ref, grow_ref, slot, chunks, hb)

    def serial(i, slot):
        return _dn_serial_stages(o_ref, s_ref, slot, (chunk_of(i, 0), chunk_of(i, 1)))

    def step(i, cur, nxt):
        loc, ser = local(i + 1, nxt), serial(i, cur)
        gap = len(loc) // len(ser)
        for n, stage in enumerate(loc):
            if n % gap == 0 and n // gap < len(ser):
                ser[n // gap]()
            stage()

    for stage in local(0, slot_a):
        stage()

    def body(ii, carry):
        step(2 * ii, slot_a, slot_b)
        step(2 * ii + 1, slot_b, slot_a)
        return carry

    assert nc % 2 == 0
    lax.fori_loop(0, nc // 2, body, 0)


def _dn_scan(qkv, gcol, grow, n_lat):
    batch, n_tot, _ = qkv.shape
    hk = DN_HEADS * DN_DK
    wblk = SCAN_HEADS * DN_DK
    nb = hk // wblk
    nc = n_tot // DN_CHUNK
    ni = 2 * SCAN_HEADS
    slot = [pltpu.VMEM((ni, DN_CHUNK, DN_DV), F32), pltpu.VMEM((ni, 2 * DN_CHUNK, DN_DK), BF16),
            pltpu.VMEM((ni, DN_CHUNK + DN_DK, DN_CHUNK), BF16), pltpu.VMEM((ni, SUBLANES, LANES), F32)]
    kern = functools.partial(_dn_scan_kernel, n_lat=n_lat, n_tot=n_tot)
    return pl.pallas_call(
        kern,
        grid=(batch, nb),
        in_specs=[
            pl.BlockSpec((1, n_tot, wblk), lambda b, h: (b, 0, h)),
            pl.BlockSpec((1, n_tot, wblk), lambda b, h: (b, 0, nb + h)),
            pl.BlockSpec((1, n_tot, wblk), lambda b, h: (b, 0, 2 * nb + h)),
            pl.BlockSpec((1, n_tot, LANES), lambda b, h: (b, 0, 0)),
            pl.BlockSpec((1, nc, LANES, DN_CHUNK), lambda b, h: (b, 0, 0, 0)),
        ],
        out_specs=pl.BlockSpec((1, n_tot, wblk), lambda b, h: (b, 0, h)),
        out_shape=jax.ShapeDtypeStruct((batch, n_tot, hk), F32),
        scratch_shapes=[pltpu.VMEM((ni, DN_DK, DN_DV), F32)] + slot + slot,
        compiler_params=_cparams(("arbitrary", "arbitrary"), 56),
        name="dn_scan",
    )(qkv, qkv, qkv, gcol, grow)


def _out_ln_kernel(y_ref, z_ref, ng_ref, x_ref, gt_ref, w_ref, lg_ref, lb_ref, o_ref, yb_ref,
                   *, alpha, gated_norm):
    if gated_norm:
        for h0 in range(0, y_ref.shape[2], DN_DV):
            yh = y_ref[0, :, h0:h0 + DN_DV]
            ms = jnp.mean(yh * yh, axis=-1, keepdims=True)
            yh = yh * lax.rsqrt(ms + RMS_EPS) * ng_ref[...]
            yb_ref[:, h0:h0 + DN_DV] = (yh * _silu(z_ref[0, :, h0:h0 + DN_DV])).astype(BF16)
    else:
        yb_ref[...] = y_ref[0].astype(BF16)
    proj = _dot(yb_ref[...], w_ref[...])
    o_ref[0] = _layernorm(alpha * x_ref[0] + gt_ref[0] * proj, lg_ref[...], lb_ref[...])


def _out_ln(y, z, norm_g, xc, mod, gt_idx, w16, ln_g, ln_b, alpha, n_lat, n_out, gated_norm):
    batch, _, d = xc.shape
    hv = y.shape[2]
    n_lat_tiles = n_lat // ROW_TILE
    kern = functools.partial(_out_ln_kernel, alpha=alpha, gated_norm=gated_norm)
    tok = lambda width: pl.BlockSpec((1, ROW_TILE, width), lambda b, t: (b, t, 0))
    return pl.pallas_call(
        kern,
        grid=(batch, n_out // ROW_TILE),
        in_specs=[
            tok(hv), tok(hv),
            pl.BlockSpec((1, norm_g.shape[1]), lambda b, t: (0, 0)),
            tok(d),
            pl.BlockSpec((1, 1, d), lambda b, t: (jnp.where(t < n_lat_tiles, b, batch), 0, gt_idx)),
            pl.BlockSpec(memory_space=pltpu.VMEM),
            pl.BlockSpec((1, d), lambda b, t: (0, 0)),
            pl.BlockSpec((1, d), lambda b, t: (0, 0)),
        ],
        out_specs=tok(d),
        out_shape=jax.ShapeDtypeStruct((batch, n_out, d), F32),
        scratch_shapes=[pltpu.VMEM((ROW_TILE, hv), BF16)],
        compiler_params=_cparams(("arbitrary", "arbitrary"), 40),
        name="out_ln",
    )(y, z, norm_g, xc, mod, w16, ln_g, ln_b)


def _ffn_kernel(x_ref, sc_ref, sh_ref, gt_ref, wg_ref, wu_ref, wd_ref, lg_ref, lb_ref, o_ref, *, alpha):
    x = x_ref[0]
    h = (x * (1.0 + sc_ref[0]) + sh_ref[0]).astype(BF16)
    gate = _dot(h, wg_ref[...])
    up = _dot(h, wu_ref[...])
    act = (_silu(gate) * up).astype(BF16)
    y = _dot(act, wd_ref[...])
    o_ref[0] = _layernorm(alpha * x + gt_ref[0] * y, lg_ref[...], lb_ref[...])


def _ffn(xc, mod, wg16, wu16, wd16, ln_g, ln_b, alpha, n_lat, n_out):
    batch, _, d = xc.shape
    n_lat_tiles = n_lat // ROW_TILE
    row = lambda b, t: jnp.where(t < n_lat_tiles, b, batch)
    kern = functools.partial(_ffn_kernel, alpha=alpha)
    tok = pl.BlockSpec((1, ROW_TILE, d), lambda b, t: (b, t, 0))
    resident = pl.BlockSpec(memory_space=pltpu.VMEM)
    return pl.pallas_call(
        kern,
        grid=(batch, n_out // ROW_TILE),
        in_specs=[
            tok,
            pl.BlockSpec((1, 1, d), lambda b, t: (row(b, t), 0, 4)),
            pl.BlockSpec((1, 1, d), lambda b, t: (row(b, t), 0, 3)),
            pl.BlockSpec((1, 1, d), lambda b, t: (row(b, t), 0, 5)),
            resident, resident, resident,
            pl.BlockSpec((1, d), lambda b, t: (0, 0)),
            pl.BlockSpec((1, d), lambda b, t: (0, 0)),
        ],
        out_specs=tok,
        out_shape=jax.ShapeDtypeStruct((batch, n_out, d), F32),
        compiler_params=_cparams(("arbitrary", "arbitrary"), 52),
        name="ffn",
    )(xc, mod, mod, mod, wg16, wu16, wd16, ln_g, ln_b)


def _da_proj_kernel(x_ref, sc_l, sh_l, sc_c, sh_c, w_ref, cos_ref, sin_a_ref, sin_b_ref,
                    o_ref, xb_ref, *, n_lat, n_qk_tiles):
    j = pl.program_id(1)

    @pl.when(j == 0)
    def _():
        _modulate_into(xb_ref, x_ref, sc_l, sh_l, sc_c, sh_c, n_lat)

    acc = _dot(xb_ref[...], w_ref[...])

    @pl.when(j < n_qk_tiles)
    def _():
        scale = jnp.where(j < n_qk_tiles // 2, DA_DIM ** -0.5 * LOG2E, 1.0)
        quarter = DA_DIM // 4
        for h0 in range(0, COL_TILE, LANES):
            xh = acc[:, h0:h0 + LANES]
            rot = (pltpu.roll(xh, quarter, 1) * sin_a_ref[...]
                   + pltpu.roll(xh, LANES - quarter, 1) * sin_b_ref[...])
            o_ref[0, :, h0:h0 + LANES] = ((xh * cos_ref[...] + rot) * scale).astype(BF16)

    @pl.when(j >= n_qk_tiles)
    def _():
        o_ref[0] = acc.astype(BF16)


def _da_project(xc, mod, w16, cos_t, sin_a, sin_b, n_lat):
    batch, n_tot, d = xc.shape
    n_cols = w16.shape[1]
    n_tiles = n_cols // COL_TILE
    n_qk_tiles = 2 * n_tiles // 3
    kern = functools.partial(_da_proj_kernel, n_lat=n_lat, n_qk_tiles=n_qk_tiles)
    table = pl.BlockSpec((n_tot, LANES), lambda b, j: (0, 0))
    return pl.pallas_call(
        kern,
        grid=(batch, n_tiles),
        in_specs=[pl.BlockSpec((1, n_tot, d), lambda b, j: (b, 0, 0))]
        + _mod_specs(batch, d, 1, 0)
        + [pl.BlockSpec((d, COL_TILE), lambda b, j: (0, j)), table, table, table],
        out_specs=pl.BlockSpec((1, n_tot, COL_TILE), lambda b, j: (b, 0, j)),
        out_shape=jax.ShapeDtypeStruct((batch, n_tot, n_cols), BF16),
        scratch_shapes=[pltpu.VMEM((n_tot, d), BF16)],
        compiler_params=_cparams(("arbitrary", "arbitrary"), 56),
        name="da_project",
    )(xc, mod, mod, mod, mod, w16, cos_t, sin_a, sin_b)


def _da_attn_kernel(q_ref, k_ref, v_ref, lam_ref, g_ref, o_ref, *, lambda_init):
    lam = lam_ref[...]
    lam_val = (jnp.exp(jnp.sum(lam[0:1] * lam[1:2], axis=-1, keepdims=True))
               - jnp.exp(jnp.sum(lam[2:3] * lam[3:4], axis=-1, keepdims=True)) + lambda_init)
    k = k_ref[0]
    n_sub = q_ref.shape[1] // ATT_SUB_ROWS
    lane = lax.broadcasted_iota(jnp.int32, (ATT_SUB_ROWS, 2 * DA_DIM), 1)

    def scores(r):
        q = q_ref[0, r * ATT_SUB_ROWS:(r + 1) * ATT_SUB_ROWS, :]
        zero = jnp.zeros_like(q)
        return [_dot_nt(jnp.where((lane < DA_DIM) if comp == 0 else (lane >= DA_DIM), q, zero), k)
                for comp in range(2)]

    def weights(s):
        e0, e1 = (jnp.exp2(sc - jnp.max(sc, axis=-1, keepdims=True)) for sc in s)
        l0 = jnp.sum(e0, axis=-1, keepdims=True)
        l1 = jnp.sum(e1, axis=-1, keepdims=True)
        return (e0 - e1 * (lam_val * l0 / l1)).astype(BF16), 1.0 / l0

    def values(r, a16, inv_l0):
        o = _dot(a16, v_ref[0]) * inv_l0
        ms = jnp.mean(o * o, axis=-1, keepdims=True)
        o_ref[0, r * ATT_SUB_ROWS:(r + 1) * ATT_SUB_ROWS, :] = (
            o * lax.rsqrt(ms + RMS_EPS) * g_ref[...] * (1.0 - lambda_init))

    s_next = scores(0)
    for r in range(n_sub):
        s_cur = s_next
        if r + 1 < n_sub:
            s_next = scores(r + 1)
        values(r, *weights(s_cur))


def _da_attention(qkv16, lam, subln_g, lambda_init, n_lat):
    batch, n_tot, _ = qkv16.shape
    hw = 2 * DA_DIM
    kern = functools.partial(_da_attn_kernel, lambda_init=lambda_init)
    tq = ATT_Q_TILE if n_lat % ATT_Q_TILE == 0 else ROW_TILE
    return pl.pallas_call(
        kern,
        grid=(batch, DA_HEADS, n_lat // tq),
        in_specs=[
            pl.BlockSpec((1, tq, hw), lambda b, h, i: (b, i, h)),
            pl.BlockSpec((1, n_tot, hw), lambda b, h, i: (b, 0, DA_HEADS + h)),
            pl.BlockSpec((1, n_tot, hw), lambda b, h, i: (b, 0, 2 * DA_HEADS + h)),
            pl.BlockSpec(lam.shape, lambda b, h, i: (0, 0)),
            pl.BlockSpec((1, hw), lambda b, h, i: (0, 0)),
        ],
        out_specs=pl.BlockSpec((1, tq, hw), lambda b, h, i: (b, i, h)),
        out_shape=jax.ShapeDtypeStruct((batch, n_lat, DA_HEADS * hw), F32),
        compiler_params=_cparams(("arbitrary", "arbitrary", "arbitrary"), 48),
        name="da_attention",
    )(qkv16, qkv16, qkv16, lam, subln_g)


def _rope_tables(n_lat, n_tot):
    quarter = DA_DIM // 4
    inv_freq = ROPE_BASE ** (-jnp.arange(quarter, dtype=F32) / quarter)
    rows = n_lat // GRID_W
    row = jnp.repeat(jnp.arange(rows, dtype=F32), GRID_W)
    col = jnp.tile(jnp.arange(GRID_W, dtype=F32), rows)
    ang_r = row[:, None] * inv_freq
    ang_c = col[:, None] * inv_freq
    ang = jnp.concatenate([ang_r, ang_r, ang_c, ang_c], axis=-1)
    reps = LANES // DA_DIM
    cos = jnp.tile(jnp.cos(ang), (1, reps))
    sin = jnp.tile(jnp.sin(ang), (1, reps))
    upper = (jnp.arange(LANES) % (2 * quarter)) >= quarter
    sin_a = jnp.where(upper, sin, 0.0)
    sin_b = jnp.where(upper, 0.0, -sin)
    n_ctx = n_tot - n_lat
    ext = lambda t, fill: jnp.concatenate([t, jnp.full((n_ctx, LANES), fill, F32)], axis=0)
    return ext(cos, 1.0), ext(sin_a, 0.0), ext(sin_b, 0.0)


def kernel(x, c, ctx, c_ctx, ada_w, ada_b, ln1_g, ln1_b, ln2_g, ln2_b, ffn_w_gate, ffn_w_up, ffn_w_down,
           dn_w_in, dn_conv, dn_a_log, dn_dt_bias, dn_norm_g, dn_w_out, da_w_in, da_lambda, da_subln_g,
           da_w_out):
    batch, n_lat, d = x.shape
    n_ctx = ctx.shape[1]
    n_tot = n_lat + n_ctx
    depth = ada_w.shape[0]
    assert depth == 2, "layer 0 is gated DeltaNet, layer 1 (the last) differential attention"
    assert batch < MOD_ROWS and n_lat % ROW_TILE == 0 and n_ctx % ROW_TILE == 0
    alpha = (2.0 * depth) ** 0.25
    hk = DN_HEADS * DN_DK

    cs = jnp.concatenate([c, c_ctx[None, :], jnp.zeros((MOD_ROWS - batch - 1, d), F32)], axis=0)
    mods = _ada_modulation(cs, ada_w, ada_b).reshape(depth, MOD_ROWS, 1, 6 * d)

    xc = jnp.concatenate([x, ctx], axis=1)

    mod = mods[0]
    gate_cols = dn_w_in[0][:, 4 * hk:]
    w_in = jnp.concatenate(
        [dn_w_in[0][:, :4 * hk], gate_cols, jnp.zeros((d, COL_TILE - gate_cols.shape[1]), F32)], axis=1)
    lane_pad = lambda v: jnp.concatenate([v.reshape(1, -1), jnp.zeros((1, LANES - v.size), F32)], axis=1)
    qkv, z, gcol, grow = _dn_project(xc, mod, w_in.astype(BF16), dn_conv[0],
                                     lane_pad(dn_a_log[0]), lane_pad(dn_dt_bias[0]), n_lat)
    o = _dn_scan(qkv, gcol, grow, n_lat)
    xc = _out_ln(o, z, dn_norm_g[0].reshape(1, -1), xc, mod, 2, dn_w_out[0].astype(BF16),
                 ln1_g[0].reshape(1, -1), ln1_b[0].reshape(1, -1), alpha, n_lat, n_tot, True)
    xc = _ffn(xc, mod, ffn_w_gate[0].astype(BF16), ffn_w_up[0].astype(BF16), ffn_w_down[0].astype(BF16),
              ln2_g[0].reshape(1, -1), ln2_b[0].reshape(1, -1), alpha, n_lat, n_tot)

    mod = mods[1]
    lambda_init = 0.8 - 0.6 * math.exp(-0.3 * 1)
    cos_t, sin_a, sin_b = _rope_tables(n_lat, n_tot)
    qkv16 = _da_project(xc, mod, da_w_in[0].astype(BF16), cos_t, sin_a, sin_b, n_lat)
    y = _da_attention(qkv16, da_lambda[0], da_subln_g[0].reshape(1, -1), lambda_init, n_lat)
    xl = _out_ln(y, y, da_subln_g[0].reshape(1, -1), xc, mod, 2, da_w_out[0].astype(BF16),
                 ln1_g[1].reshape(1, -1), ln1_b[1].reshape(1, -1), alpha, n_lat, n_lat, False)
    return _ffn(xl, mod, ffn_w_gate[1].astype(BF16), ffn_w_up[1].astype(BF16), ffn_w_down[1].astype(BF16),
                ln2_g[1].reshape(1, -1), ln2_b[1].reshape(1, -1), alpha, n_lat, n_lat)
```

```python
import functools
import math

import jax
import jax.numpy as jnp
from jax import lax
from jax.experimental import pallas as pl
from jax.experimental.pallas import tpu as pltpu

F32 = jnp.float32
BF16 = jnp.bfloat16
HIGHEST = lax.Precision.HIGHEST

DN_HEADS = 8
DN_DK = 128
DN_DV = 128
DN_CONV = 5
DN_CHUNK = 64
DA_HEADS = 8
DA_DIM = 64
GRID_W = 64
ROPE_BASE = 10000.0
LN_EPS = 1e-5
RMS_EPS = 1e-6

LANES = 128
SUBLANES = 8
MOD_ROWS = 16
ROW_TILE = 256
PROJ_ROW_BLOCK = 512
COL_TILE = 256
ADA_COL_TILE = 1536
ATT_Q_TILE = 1024
ATT_SUB_ROWS = 256
LOG2E = math.log2(math.e)


def _cparams(dims, vmem_mb):
    return pltpu.CompilerParams(dimension_semantics=dims, vmem_limit_bytes=vmem_mb * 1024 * 1024)


def _silu(x):
    return x * jax.nn.sigmoid(x)


def _dot(a, b, precision=None):
    return jnp.dot(a, b, preferred_element_type=F32, precision=precision)


def _dot_nt(a, b):
    return lax.dot_general(a, b, (((1,), (1,)), ((), ())), preferred_element_type=F32)


def _layernorm(x, g, b):
    mu = jnp.mean(x, axis=-1, keepdims=True)
    xc = x - mu
    var = jnp.mean(xc * xc, axis=-1, keepdims=True)
    return xc * lax.rsqrt(var + LN_EPS) * g + b


def _ada_kernel(cs_ref, w_ref, b_ref, o_ref):
    s = _silu(cs_ref[...]).astype(BF16)
    o_ref[0] = _dot(s, w_ref[0].astype(BF16)) + b_ref[0]


def _ada_modulation(cs, ada_w, ada_b):
    depth, d, n = ada_w.shape
    tn = ADA_COL_TILE if n % ADA_COL_TILE == 0 else n
    return pl.pallas_call(
        _ada_kernel,
        grid=(depth, n // tn),
        in_specs=[
            pl.BlockSpec((MOD_ROWS, d), lambda i, j: (0, 0)),
            pl.BlockSpec((1, d, tn), lambda i, j: (i, 0, j)),
            pl.BlockSpec((1, 1, tn), lambda i, j: (i, 0, j)),
        ],
        out_specs=pl.BlockSpec((1, MOD_ROWS, tn), lambda i, j: (i, 0, j)),
        out_shape=jax.ShapeDtypeStruct((depth, MOD_ROWS, n), F32),
        compiler_params=_cparams(("arbitrary", "arbitrary"), 40),
        name="ada_modulation",
    )(cs, ada_w, ada_b.reshape(depth, 1, n))


def _modulate_into(xb_ref, x_ref, sc_l, sh_l, sc_c, sh_c, n_lat):
    xb_ref[:n_lat, :] = (x_ref[0, :n_lat, :] * (1.0 + sc_l[0]) + sh_l[0]).astype(BF16)
    xb_ref[n_lat:, :] = (x_ref[0, n_lat:, :] * (1.0 + sc_c[0]) + sh_c[0]).astype(BF16)


def _row_blocks(n_lat, n_tot):
    blocks = []
    for seg0, seg in ((0, n_lat), (n_lat, n_tot - n_lat)):
        size = PROJ_ROW_BLOCK if seg % PROJ_ROW_BLOCK == 0 else seg
        blocks += [(seg0 + r, size) for r in range(0, seg, size)]
    return blocks


def _pipelined(blocks, produce, consume):
    pending = None
    for blk in blocks:
        cur = produce(*blk)
        if pending is not None:
            consume(*pending)
        pending = blk + (cur,)
    consume(*pending)


def _mod_specs(batch, d, sc_idx, sh_idx):
    return [
        pl.BlockSpec((1, 1, d), lambda b, j: (b, 0, sc_idx)),
        pl.BlockSpec((1, 1, d), lambda b, j: (b, 0, sh_idx)),
        pl.BlockSpec((1, 1, d), lambda b, j: (batch, 0, sc_idx)),
        pl.BlockSpec((1, 1, d), lambda b, j: (batch, 0, sh_idx)),
    ]


def _dn_proj_kernel(x_ref, sc_l, sh_l, sc_c, sh_c, w_ref, conv_ref, alog_ref, dt_ref,
                    qkv_ref, z_ref, gcol_ref, grow_ref, xb_ref, pad_ref,
                    *, n_lat, n_tot, n_qkv_tiles, n_z_tiles):
    j = pl.program_id(1)
    n_ctx = n_tot - n_lat
    halo = SUBLANES
    pad = (DN_CONV - 1) // 2
    lat0 = halo
    ctx0 = 2 * halo + n_lat

    @pl.when(j == 0)
    def _():
        _modulate_into(xb_ref, x_ref, sc_l, sh_l, sc_c, sh_c, n_lat)

    blocks = _row_blocks(n_lat, n_tot)

    def project(r0, rows):
        return _dot(xb_ref[r0:r0 + rows, :], w_ref[...])

    @pl.when(j < n_qkv_tiles)
    def _():
        zeros = jnp.zeros((halo, COL_TILE), F32)
        pad_ref[0:halo, :] = zeros
        pad_ref[lat0 + n_lat:ctx0, :] = zeros
        pad_ref[ctx0 + n_ctx:ctx0 + n_ctx + halo, :] = zeros
        cw = conv_ref[...]
        n_qk_tiles = 2 * n_qkv_tiles // 3
        is_q = j < n_qk_tiles // 2
        is_qk = j < n_qk_tiles

        def padded(r0):
            return r0 + (lat0 if r0 < n_lat else ctx0 - n_lat)

        def stage(r0, rows):
            pad_ref[padded(r0):padded(r0) + rows, :] = project(r0, rows)

        def conv_norm(r0, rows, _):
            base = padded(r0)
            ext = pad_ref[base - halo:base + rows + halo, :]
            y = None
            for t in range(DN_CONV):
                shifted = ext if t == pad else pltpu.roll(ext, (pad - t) % ext.shape[0], 0)
                term = shifted[halo:halo + rows, :] * cw[t:t + 1, :]
                y = term if y is None else y + term
            y = _silu(y)
            for h0 in range(0, COL_TILE, DN_DK):
                yh = y[:, h0:h0 + DN_DK]
                ss = jnp.sum(yh * yh, axis=-1, keepdims=True)
                fac = lax.rsqrt(ss + RMS_EPS)
                fac = jnp.where(is_q, fac * (DN_DK ** -0.5), fac)
                fac = jnp.where(is_qk, fac, jnp.ones_like(fac))
                qkv_ref[0, r0:r0 + rows, h0:h0 + DN_DK] = yh * fac

        _pipelined(blocks, stage, conv_norm)

    @pl.when(jnp.logical_and(j >= n_qkv_tiles, j < n_qkv_tiles + n_z_tiles))
    def _():
        for r0, rows in blocks:
            z_ref[0, r0:r0 + rows, :] = project(r0, rows)

    @pl.when(j == n_qkv_tiles + n_z_tiles)
    def _():
        t = _dot(xb_ref[...], w_ref[:, :LANES])
        lane = lax.broadcasted_iota(jnp.int32, t.shape, 1)
        x = t + dt_ref[...]
        softplus = jnp.maximum(x, 0.0) + jnp.log1p(jnp.exp(-jnp.abs(x)))
        decay = -jnp.exp(alog_ref[...]) * softplus
        gates = jnp.where(lane < 2 * DN_HEADS, decay, jax.nn.sigmoid(t))
        ri = lax.broadcasted_iota(jnp.int32, (DN_CHUNK, DN_CHUNK), 0)
        ci = lax.broadcasted_iota(jnp.int32, (DN_CHUNK, DN_CHUNK), 1)
        tri_f = (ri >= ci).astype(F32)
        tri_b = (ri <= ci).astype(F32)
        lane_c = lax.broadcasted_iota(jnp.int32, (DN_CHUNK, LANES), 1)
        for c in range(n_tot // DN_CHUNK):
            gc = gates[c * DN_CHUNK:(c + 1) * DN_CHUNK]
            cum_f = _dot(tri_f, gc, HIGHEST)
            cum_b = _dot(tri_b, gc, HIGHEST)
            col = jnp.where(lane_c < DN_HEADS, cum_f, jnp.where(lane_c < 2 * DN_HEADS, cum_b, gc))
            gcol_ref[0, c * DN_CHUNK:(c + 1) * DN_CHUNK, :] = col
            grow_ref[0, c] = col.T


def _dn_project(xc, mod, w16, conv_w, alog_row, dt_row, n_lat):
    batch, n_tot, d = xc.shape
    n_cols = w16.shape[1]
    n_tiles = n_cols // COL_TILE
    hk = DN_HEADS * DN_DK
    n_qkv_tiles = 3 * hk // COL_TILE
    n_z_tiles = hk // COL_TILE
    assert n_tiles == n_qkv_tiles + n_z_tiles + 1
    nc = n_tot // DN_CHUNK
    kern = functools.partial(_dn_proj_kernel, n_lat=n_lat, n_tot=n_tot,
                             n_qkv_tiles=n_qkv_tiles, n_z_tiles=n_z_tiles)
    return pl.pallas_call(
        kern,
        grid=(batch, n_tiles),
        in_specs=[pl.BlockSpec((1, n_tot, d), lambda b, j: (b, 0, 0))]
        + _mod_specs(batch, d, 1, 0)
        + [
            pl.BlockSpec((d, COL_TILE), lambda b, j: (0, j)),
            pl.BlockSpec((DN_CONV, COL_TILE), lambda b, j: (0, jnp.minimum(j, n_qkv_tiles - 1))),
            pl.BlockSpec((1, LANES), lambda b, j: (0, 0)),
            pl.BlockSpec((1, LANES), lambda b, j: (0, 0)),
        ],
        out_specs=[
            pl.BlockSpec((1, n_tot, COL_TILE), lambda b, j: (b, 0, jnp.minimum(j, n_qkv_tiles - 1))),
            pl.BlockSpec((1, n_tot, COL_TILE),
                         lambda b, j: (b, 0, jnp.clip(j - n_qkv_tiles, 0, n_z_tiles - 1))),
            pl.BlockSpec((1, n_tot, LANES), lambda b, j: (b, 0, 0)),
            pl.BlockSpec((1, nc, LANES, DN_CHUNK), lambda b, j: (b, 0, 0, 0)),
        ],
        out_shape=[
            jax.ShapeDtypeStruct((batch, n_tot, 3 * hk), F32),
            jax.ShapeDtypeStruct((batch, n_tot, hk), F32),
            jax.ShapeDtypeStruct((batch, n_tot, LANES), F32),
            jax.ShapeDtypeStruct((batch, nc, LANES, DN_CHUNK), F32),
        ],
        scratch_shapes=[
            pltpu.VMEM((n_tot, d), BF16),
            pltpu.VMEM((n_tot + 3 * SUBLANES, COL_TILE), F32),
        ],
        compiler_params=_cparams(("arbitrary", "arbitrary"), 56),
        name="dn_project",
    )(xc, mod, mod, mod, mod, w16, conv_w, alog_row, dt_row)


SCAN_INSTANCES = tuple((hh, d) for hh in range(DN_HEADS) for d in range(2))
INV_BASE = 16


def _dot16(a, b):
    return _dot(a.astype(BF16), b.astype(BF16))


def _dn_local_stages(ins, slot, sidx, half):
    u_ref, wq_ref, m_ref, e_ref = slot
    c = DN_CHUNK
    st = [dict() for _ in SCAN_INSTANCES]

    def each(fn):
        def run():
            for (hh, d), t in zip(SCAN_INSTANCES, st):
                fn(t, hh, d)
        return run

    def idx2():
        return (lax.broadcasted_iota(jnp.int32, (c, c), 0), lax.broadcasted_iota(jnp.int32, (c, c), 1))

    def load(t, hh, d):
        q_ref, k_ref, v_ref, gcol_ref, grow_ref = ins[d]
        part = half if d == 0 else 1 - half
        rows = slice(part * c, (part + 1) * c)
        ri, ci = idx2()
        lane = lax.broadcasted_iota(jnp.int32, (c, LANES), 1)
        cols = slice(hh * DN_DK, (hh + 1) * DN_DK)
        q = q_ref[0, rows, cols]
        k = k_ref[0, rows, cols]
        v = v_ref[0, rows, cols]
        gates = gcol_ref[0, rows, :]
        g_lane = d * DN_HEADS + hh
        gcum = jnp.sum(jnp.where(lane == g_lane, gates, 0.0), axis=-1, keepdims=True)
        beta = jnp.sum(jnp.where(lane == 2 * DN_HEADS + g_lane, gates, 0.0), axis=-1, keepdims=True)
        grow = grow_ref[0, part, g_lane:g_lane + 1, :]
        incl = (ri >= ci) if d == 0 else (ri <= ci)
        g_last = grow[:, c - 1:c] if d == 0 else grow[:, 0:1]
        t["gamma"] = jnp.where(incl, jnp.exp(jnp.where(incl, gcum - grow, 0.0)), 0.0)
        kb = k * beta
        decay = jnp.exp(gcum)
        t["k16"] = k.astype(BF16)
        t["kbq16"] = jnp.concatenate([kb.astype(BF16), q.astype(BF16)], axis=0)
        t["rhs16"] = jnp.concatenate([v * beta, kb * decay], axis=-1).astype(BF16)
        idx = d * DN_HEADS + hh
        wq_ref[sidx, idx, c:, :] = (q * decay).astype(BF16)
        m_ref[sidx, idx, c:, :] = (k * jnp.exp(g_last - gcum)).T.astype(BF16)
        e_ref[sidx, idx] = jnp.broadcast_to(jnp.exp(g_last), (SUBLANES, LANES))

    def gram(t, hh, d):
        t["kq"] = _dot_nt(t.pop("kbq16"), t.pop("k16"))

    def split(t, hh, d):
        ri, ci = idx2()
        strict = (ri > ci) if d == 0 else (ri < ci)
        kq = t.pop("kq")
        gamma = t.pop("gamma")
        a = jnp.where(strict, kq[:c] * gamma, 0.0)
        m_ref[sidx, d * DN_HEADS + hh, :c, :] = (kq[c:] * gamma).astype(BF16)
        same = (ri // INV_BASE) == (ci // INV_BASE)
        a_diag = jnp.where(same, a, 0.0)
        t["a"] = a
        t["inv"] = (ri == ci).astype(F32) - a_diag
        t["pw"] = _dot16(a_diag, a_diag)

    def series(t, hh, d):
        pw16 = t["pw"].astype(BF16)
        both = _dot(jnp.concatenate([pw16, t["inv"].astype(BF16)], axis=0), pw16)
        t["pw"] = both[:c]
        t["inv"] = t["inv"] + both[c:]

    def series_last(t, hh, d):
        t["inv"] = t["inv"] + _dot16(t["inv"], t.pop("pw"))

    def merge_a(blk):
        def fn(t, hh, d):
            ri, ci = idx2()
            off = jnp.logical_and((ri // (2 * blk)) == (ci // (2 * blk)), (ri // blk) != (ci // blk))
            t["y"] = _dot16(jnp.where(off, t["a"], 0.0), t["inv"])
        return fn

    def merge_b(t, hh, d):
        t["inv"] = t["inv"] - _dot16(t["inv"], t.pop("y"))

    def solve(t, hh, d):
        t["uw"] = _dot(t.pop("inv").astype(BF16), t.pop("rhs16"))
        t.pop("a")

    def store(t, hh, d):
        idx = d * DN_HEADS + hh
        uw = t.pop("uw")
        u_ref[sidx, idx] = uw[:, :DN_DV]
        wq_ref[sidx, idx, :c, :] = uw[:, DN_DV:].astype(BF16)

    stages = [load, gram, split] + [series] * (int(math.log2(INV_BASE)) - 2) + [series_last]
    blk = INV_BASE
    while blk < c:
        stages += [merge_a(blk), merge_b]
        blk *= 2
    stages += [solve, store]
    return [each(fn) for fn in stages]


def _dn_serial_stages(outs, s_ref, slot, sidx, half):
    u_ref, wq_ref, m_ref, e_ref = slot
    c = DN_CHUNK
    st = [dict() for _ in SCAN_INSTANCES]

    def each(fn):
        def run():
            for (hh, d), t in zip(SCAN_INSTANCES, st):
                fn(t, d * DN_HEADS + hh, hh, d)
        return run

    def state_dot(t, idx, hh, d):
        t["s"] = s_ref[idx]
        t["ws"] = _dot(wq_ref[sidx, idx], t["s"].astype(BF16))

    def value_dot(t, idx, hh, d):
        v_new = u_ref[sidx, idx] - t["ws"][:c]
        t["mv"] = _dot(m_ref[sidx, idx], v_new.astype(BF16))

    def update(t, idx, hh, d):
        mv = t.pop("mv")
        s_ref[idx] = t.pop("s") * e_ref[sidx, idx][0:1, :] + mv[c:]
        part = half if d == 0 else 1 - half
        outs[d][0, part * c:(part + 1) * c, hh * DN_DV:(hh + 1) * DN_DV] = t.pop("ws")[c:] + mv[:c]

    return [each(fn) for fn in (state_dot, value_dot, update)]


def _dn_scan_kernel(*refs):
    ins = (refs[0:5], refs[5:10])
    outs = refs[10:12]
    s_ref = refs[12]
    slot = refs[13:17]
    g = pl.program_id(1)
    par = g % 2

    @pl.when(g == 0)
    def _():
        for ref in (s_ref,) + tuple(slot):
            ref[...] = jnp.zeros(ref.shape, ref.dtype)

    for half in range(2):
        loc = _dn_local_stages(ins, slot, 2 * par + half, half)
        ser = _dn_serial_stages(outs, s_ref, slot, 2 * (1 - par) + half, half)
        gap = len(loc) // len(ser)
        for n, stage in enumerate(loc):
            if n % gap == 0 and n // gap < len(ser):
                ser[n // gap]()
            stage()


def _dn_scan(qkv, gcol, grow, n_lat):
    batch, n_tot, _ = qkv.shape
    hk = DN_HEADS * DN_DK
    pair = 2 * DN_CHUNK
    n_pairs = n_tot // pair
    lat_pairs = n_lat // pair
    ctx_pairs = n_pairs - lat_pairs
    assert n_lat % pair == 0 and n_tot % pair == 0
    ni = len(SCAN_INSTANCES)

    def fwd(p):
        return jnp.where(p < ctx_pairs, p + lat_pairs, p - ctx_pairs)

    def bwd(p):
        return n_pairs - 1 - p

    def p_in(g):
        return jnp.minimum(g, n_pairs - 1)

    def p_out(g):
        return jnp.maximum(g - 1, 0)

    def in_specs(where):
        return [pl.BlockSpec((1, pair, hk), lambda b, g, col=col: (b, where(p_in(g)), col)) for col in range(3)] + [
            pl.BlockSpec((1, pair, LANES), lambda b, g: (b, where(p_in(g)), 0)),
            pl.BlockSpec((1, 2, LANES, DN_CHUNK), lambda b, g: (b, where(p_in(g)), 0, 0)),
        ]

    out_spec = lambda where: pl.BlockSpec((1, pair, hk), lambda b, g: (b, where(p_out(g)), 0))
    out_shape = jax.ShapeDtypeStruct((batch, n_tot, hk), F32)
    slot = [pltpu.VMEM((4, ni, DN_CHUNK, DN_DV), F32), pltpu.VMEM((4, ni, 2 * DN_CHUNK, DN_DK), BF16),
            pltpu.VMEM((4, ni, DN_CHUNK + DN_DK, DN_CHUNK), BF16), pltpu.VMEM((4, ni, SUBLANES, LANES), F32)]
    per_dir = (qkv, qkv, qkv, gcol, grow)
    return pl.pallas_call(
        _dn_scan_kernel,
        grid=(batch, n_pairs + 1),
        in_specs=in_specs(fwd) + in_specs(bwd),
        out_specs=[out_spec(fwd), out_spec(bwd)],
        out_shape=[out_shape, out_shape],
        scratch_shapes=[pltpu.VMEM((ni, DN_DK, DN_DV), F32)] + slot,
        compiler_params=_cparams(("arbitrary", "arbitrary"), 48),
        name="dn_scan",
    )(*per_dir, *per_dir)


def _out_ln_kernel(*refs, alpha, gated_norm):
    if gated_norm:
        y_ref, y2_ref, z_ref, ng_ref = refs[:4]
    else:
        y_ref = refs[0]
    x_ref, gt_ref, w_ref, lg_ref, lb_ref, o_ref, yb_ref = refs[-7:]
    if gated_norm:
        for h0 in range(0, y_ref.shape[2], DN_DV):
            yh = y_ref[0, :, h0:h0 + DN_DV] + y2_ref[0, :, h0:h0 + DN_DV]
            ms = jnp.mean(yh * yh, axis=-1, keepdims=True)
            yh = yh * lax.rsqrt(ms + RMS_EPS) * ng_ref[...]
            yb_ref[:, h0:h0 + DN_DV] = (yh * _silu(z_ref[0, :, h0:h0 + DN_DV])).astype(BF16)
    else:
        yb_ref[...] = y_ref[0].astype(BF16)
    proj = _dot(yb_ref[...], w_ref[...])
    o_ref[0] = _layernorm(alpha * x_ref[0] + gt_ref[0] * proj, lg_ref[...], lb_ref[...])


def _out_ln(ys, xc, mod, gt_idx, w16, ln_g, ln_b, alpha, n_lat, n_out, gated_norm):
    batch, _, d = xc.shape
    hv = ys[0].shape[2]
    n_lat_tiles = n_lat // ROW_TILE
    kern = functools.partial(_out_ln_kernel, alpha=alpha, gated_norm=gated_norm)
    tok = lambda width: pl.BlockSpec((1, ROW_TILE, width), lambda b, t: (b, t, 0))
    return pl.pallas_call(
        kern,
        grid=(batch, n_out // ROW_TILE),
        in_specs=([tok(hv)] * 3 + [pl.BlockSpec((1, ys[3].shape[1]), lambda b, t: (0, 0))]
                  if gated_norm else [tok(hv)]) + [
            tok(d),
            pl.BlockSpec((1, 1, d), lambda b, t: (jnp.where(t < n_lat_tiles, b, batch), 0, gt_idx)),
            pl.BlockSpec(memory_space=pltpu.VMEM),
            pl.BlockSpec((1, d), lambda b, t: (0, 0)),
            pl.BlockSpec((1, d), lambda b, t: (0, 0)),
        ],
        out_specs=tok(d),
        out_shape=jax.ShapeDtypeStruct((batch, n_out, d), F32),
        scratch_shapes=[pltpu.VMEM((ROW_TILE, hv), BF16)],
        compiler_params=_cparams(("arbitrary", "arbitrary"), 40),
        name="out_ln",
    )(*ys, xc, mod, w16, ln_g, ln_b)


def _ffn_kernel(x_ref, sc_ref, sh_ref, gt_ref, wg_ref, wu_ref, wd_ref, lg_ref, lb_ref, o_ref, *, alpha):
    x = x_ref[0]
    h = (x * (1.0 + sc_ref[0]) + sh_ref[0]).astype(BF16)
    gate = _dot(h, wg_ref[...])
    up = _dot(h, wu_ref[...])
    act = (_silu(gate) * up).astype(BF16)
    y = _dot(act, wd_ref[...])
    o_ref[0] = _layernorm(alpha * x + gt_ref[0] * y, lg_ref[...], lb_ref[...])


def _ffn(xc, mod, wg16, wu16, wd16, ln_g, ln_b, alpha, n_lat, n_out):
    batch, _, d = xc.shape
    n_lat_tiles = n_lat // ROW_TILE
    row = lambda b, t: jnp.where(t < n_lat_tiles, b, batch)
    kern = functools.partial(_ffn_kernel, alpha=alpha)
    tok = pl.BlockSpec((1, ROW_TILE, d), lambda b, t: (b, t, 0))
    resident = pl.BlockSpec(memory_space=pltpu.VMEM)
    return pl.pallas_call(
        kern,
        grid=(batch, n_out // ROW_TILE),
        in_specs=[
            tok,
            pl.BlockSpec((1, 1, d), lambda b, t: (row(b, t), 0, 4)),
            pl.BlockSpec((1, 1, d), lambda b, t: (row(b, t), 0, 3)),
            pl.BlockSpec((1, 1, d), lambda b, t: (row(b, t), 0, 5)),
            resident, resident, resident,
            pl.BlockSpec((1, d), lambda b, t: (0, 0)),
            pl.BlockSpec((1, d), lambda b, t: (0, 0)),
        ],
        out_specs=tok,
        out_shape=jax.ShapeDtypeStruct((batch, n_out, d), F32),
        compiler_params=_cparams(("arbitrary", "arbitrary"), 52),
        name="ffn",
    )(xc, mod, mod, mod, wg16, wu16, wd16, ln_g, ln_b)


def _da_proj_kernel(x_ref, sc_l, sh_l, sc_c, sh_c, w_ref, cos_ref, sin_a_ref, sin_b_ref,
                    o_ref, xb_ref, *, n_lat, n_qk_tiles):
    j = pl.program_id(1)

    @pl.when(j == 0)
    def _():
        _modulate_into(xb_ref, x_ref, sc_l, sh_l, sc_c, sh_c, n_lat)

    blocks = _row_blocks(n_lat, x_ref.shape[1])

    def project(r0, rows):
        return _dot(xb_ref[r0:r0 + rows, :], w_ref[...])

    @pl.when(j < n_qk_tiles)
    def _():
        scale = jnp.where(j < n_qk_tiles // 2, DA_DIM ** -0.5 * LOG2E, 1.0)
        quarter = DA_DIM // 4

        def rope(r0, rows, acc):
            cos, sin_a, sin_b = (t[r0:r0 + rows, :] for t in (cos_ref, sin_a_ref, sin_b_ref))
            for h0 in range(0, COL_TILE, LANES):
                xh = acc[:, h0:h0 + LANES]
                rot = pltpu.roll(xh, quarter, 1) * sin_a + pltpu.roll(xh, LANES - quarter, 1) * sin_b
                o_ref[0, r0:r0 + rows, h0:h0 + LANES] = ((xh * cos + rot) * scale).astype(BF16)

        _pipelined(blocks, project, rope)

    @pl.when(j >= n_qk_tiles)
    def _():
        for r0, rows in blocks:
            o_ref[0, r0:r0 + rows, :] = project(r0, rows).astype(BF16)


def _da_project(xc, mod, w16, cos_t, sin_a, sin_b, n_lat):
    batch, n_tot, d = xc.shape
    n_cols = w16.shape[1]
    n_tiles = n_cols // COL_TILE
    n_qk_tiles = 2 * n_tiles // 3
    kern = functools.partial(_da_proj_kernel, n_lat=n_lat, n_qk_tiles=n_qk_tiles)
    table = pl.BlockSpec((n_tot, LANES), lambda b, j: (0, 0))
    return pl.pallas_call(
        kern,
        grid=(batch, n_tiles),
        in_specs=[pl.BlockSpec((1, n_tot, d), lambda b, j: (b, 0, 0))]
        + _mod_specs(batch, d, 1, 0)
        + [pl.BlockSpec((d, COL_TILE), lambda b, j: (0, j)), table, table, table],
        out_specs=pl.BlockSpec((1, n_tot, COL_TILE), lambda b, j: (b, 0, j)),
        out_shape=jax.ShapeDtypeStruct((batch, n_tot, n_cols), BF16),
        scratch_shapes=[pltpu.VMEM((n_tot, d), BF16)],
        compiler_params=_cparams(("arbitrary", "arbitrary"), 56),
        name="da_project",
    )(xc, mod, mod, mod, mod, w16, cos_t, sin_a, sin_b)


def _da_attn_kernel(q_ref, k_ref, v_ref, lam_ref, g_ref, o_ref, *, lambda_init):
    lam = lam_ref[...]
    lam_val = (jnp.exp(jnp.sum(lam[0:1] * lam[1:2], axis=-1, keepdims=True))
               - jnp.exp(jnp.sum(lam[2:3] * lam[3:4], axis=-1, keepdims=True)) + lambda_init)
    k = k_ref[0]
    n_sub = q_ref.shape[1] // ATT_SUB_ROWS
    lane = lax.broadcasted_iota(jnp.int32, (ATT_SUB_ROWS, 2 * DA_DIM), 1)

    def scores(r):
        q = q_ref[0, r * ATT_SUB_ROWS:(r + 1) * ATT_SUB_ROWS, :]
        zero = jnp.zeros_like(q)
        return [_dot_nt(jnp.where((lane < DA_DIM) if comp == 0 else (lane >= DA_DIM), q, zero), k)
                for comp in range(2)]

    def weights(s):
        e0, e1 = (jnp.exp2(sc - jnp.max(sc, axis=-1, keepdims=True)) for sc in s)
        l0 = jnp.sum(e0, axis=-1, keepdims=True)
        l1 = jnp.sum(e1, axis=-1, keepdims=True)
        return (e0 - e1 * (lam_val * l0 / l1)).astype(BF16), 1.0 / l0

    def values(r, a16, inv_l0):
        o = _dot(a16, v_ref[0]) * inv_l0
        ms = jnp.mean(o * o, axis=-1, keepdims=True)
        o_ref[0, r * ATT_SUB_ROWS:(r + 1) * ATT_SUB_ROWS, :] = (
            o * lax.rsqrt(ms + RMS_EPS) * g_ref[...] * (1.0 - lambda_init))

    s_next = scores(0)
    for r in range(n_sub):
        s_cur = s_next
        if r + 1 < n_sub:
            s_next = scores(r + 1)
        values(r, *weights(s_cur))


def _da_attention(qkv16, lam, subln_g, lambda_init, n_lat):
    batch, n_tot, _ = qkv16.shape
    hw = 2 * DA_DIM
    kern = functools.partial(_da_attn_kernel, lambda_init=lambda_init)
    tq = ATT_Q_TILE if n_lat % ATT_Q_TILE == 0 else ROW_TILE
    return pl.pallas_call(
        kern,
        grid=(batch, DA_HEADS, n_lat // tq),
        in_specs=[
            pl.BlockSpec((1, tq, hw), lambda b, h, i: (b, i, h)),
            pl.BlockSpec((1, n_tot, hw), lambda b, h, i: (b, 0, DA_HEADS + h)),
            pl.BlockSpec((1, n_tot, hw), lambda b, h, i: (b, 0, 2 * DA_HEADS + h)),
            pl.BlockSpec(lam.shape, lambda b, h, i: (0, 0)),
            pl.BlockSpec((1, hw), lambda b, h, i: (0, 0)),
        ],
        out_specs=pl.BlockSpec((1, tq, hw), lambda b, h, i: (b, i, h)),
        out_shape=jax.ShapeDtypeStruct((batch, n_lat, DA_HEADS * hw), F32),
        compiler_params=_cparams(("arbitrary", "arbitrary", "arbitrary"), 48),
        name="da_attention",
    )(qkv16, qkv16, qkv16, lam, subln_g)


def _rope_tables(n_lat, n_tot):
    quarter = DA_DIM // 4
    inv_freq = ROPE_BASE ** (-jnp.arange(quarter, dtype=F32) / quarter)
    rows = n_lat // GRID_W
    row = jnp.repeat(jnp.arange(rows, dtype=F32), GRID_W)
    col = jnp.tile(jnp.arange(GRID_W, dtype=F32), rows)
    ang_r = row[:, None] * inv_freq
    ang_c = col[:, None] * inv_freq
    ang = jnp.concatenate([ang_r, ang_r, ang_c, ang_c], axis=-1)
    reps = LANES // DA_DIM
    cos = jnp.tile(jnp.cos(ang), (1, reps))
    sin = jnp.tile(jnp.sin(ang), (1, reps))
    upper = (jnp.arange(LANES) % (2 * quarter)) >= quarter
    sin_a = jnp.where(upper, sin, 0.0)
    sin_b = jnp.where(upper, 0.0, -sin)
    n_ctx = n_tot - n_lat
    ext = lambda t, fill: jnp.concatenate([t, jnp.full((n_ctx, LANES), fill, F32)], axis=0)
    return ext(cos, 1.0), ext(sin_a, 0.0), ext(sin_b, 0.0)


def kernel(x, c, ctx, c_ctx, ada_w, ada_b, ln1_g, ln1_b, ln2_g, ln2_b, ffn_w_gate, ffn_w_up, ffn_w_down,
           dn_w_in, dn_conv, dn_a_log, dn_dt_bias, dn_norm_g, dn_w_out, da_w_in, da_lambda, da_subln_g,
           da_w_out):
    batch, n_lat, d = x.shape
    n_ctx = ctx.shape[1]
    n_tot = n_lat + n_ctx
    depth = ada_w.shape[0]
    assert depth == 2, "layer 0 is gated DeltaNet, layer 1 (the last) differential attention"
    assert batch < MOD_ROWS and n_lat % ROW_TILE == 0 and n_ctx % ROW_TILE == 0
    alpha = (2.0 * depth) ** 0.25
    hk = DN_HEADS * DN_DK

    cs = jnp.concatenate([c, c_ctx[None, :], jnp.zeros((MOD_ROWS - batch - 1, d), F32)], axis=0)
    mods = _ada_modulation(cs, ada_w, ada_b).reshape(depth, MOD_ROWS, 1, 6 * d)

    xc = jnp.concatenate([x, ctx], axis=1)

    mod = mods[0]
    gate_cols = dn_w_in[0][:, 4 * hk:]
    w_in = jnp.concatenate(
        [dn_w_in[0][:, :4 * hk], gate_cols, jnp.zeros((d, COL_TILE - gate_cols.shape[1]), F32)], axis=1)
    lane_pad = lambda v: jnp.concatenate([v.reshape(1, -1), jnp.zeros((1, LANES - v.size), F32)], axis=1)
    qkv, z, gcol, grow = _dn_project(xc, mod, w_in.astype(BF16), dn_conv[0],
                                     lane_pad(dn_a_log[0]), lane_pad(dn_dt_bias[0]), n_lat)
    o_fwd, o_bwd = _dn_scan(qkv, gcol, grow, n_lat)
    xc = _out_ln((o_fwd, o_bwd, z, dn_norm_g[0].reshape(1, -1)), xc, mod, 2, dn_w_out[0].astype(BF16),
                 ln1_g[0].reshape(1, -1), ln1_b[0].reshape(1, -1), alpha, n_lat, n_tot, True)
    xc = _ffn(xc, mod, ffn_w_gate[0].astype(BF16), ffn_w_up[0].astype(BF16), ffn_w_down[0].astype(BF16),
              ln2_g[0].reshape(1, -1), ln2_b[0].reshape(1, -1), alpha, n_lat, n_tot)

    mod = mods[1]
    lambda_init = 0.8 - 0.6 * math.exp(-0.3 * 1)
    cos_t, sin_a, sin_b = _rope_tables(n_lat, n_tot)
    qkv16 = _da_project(xc, mod, da_w_in[0].astype(BF16), cos_t, sin_a, sin_b, n_lat)
    y = _da_attention(qkv16, da_lambda[0], da_subln_g[0].reshape(1, -1), lambda_init, n_lat)
    xl = _out_ln((y,), xc, mod, 2, da_w_out[0].astype(BF16),
                 ln1_g[1].reshape(1, -1), ln1_b[1].reshape(1, -1), alpha, n_lat, n_lat, False)
    return _ffn(xl, mod, ffn_w_gate[1].astype(BF16), ffn_w_up[1].astype(BF16), ffn_w_down[1].astype(BF16),
                ln2_g[1].reshape(1, -1), ln2_b[1].reshape(1, -1), alpha, n_lat, n_lat)
```

```python
import functools
import math

import jax
import jax.numpy as jnp
from jax import lax
from jax.experimental import pallas as pl
from jax.experimental.pallas import tpu as pltpu

F32 = jnp.float32
BF16 = jnp.bfloat16
HIGHEST = lax.Precision.HIGHEST

DN_HEADS = 8
DN_DK = 128
DN_DV = 128
DN_CONV = 5
DN_CHUNK = 64
DA_HEADS = 8
DA_DIM = 64
GRID_W = 64
ROPE_BASE = 10000.0
LN_EPS = 1e-5
RMS_EPS = 1e-6

LANES = 128
SUBLANES = 8
MOD_ROWS = 16
ROW_TILE = 256
PROJ_ROW_BLOCK = 512
COL_TILE = 256
ADA_COL_TILE = 1536
ATT_Q_TILE = 2048
ATT_SUB_ROWS = 256
ATT_LOOKAHEAD = 2
LOG2E = math.log2(math.e)


def _cparams(dims, vmem_mb):
    return pltpu.CompilerParams(dimension_semantics=dims, vmem_limit_bytes=vmem_mb * 1024 * 1024)


def _silu(x):
    return x * jax.nn.sigmoid(x)


def _dot(a, b, precision=None):
    return jnp.dot(a, b, preferred_element_type=F32, precision=precision)


def _dot_nt(a, b):
    return lax.dot_general(a, b, (((1,), (1,)), ((), ())), preferred_element_type=F32)


def _layernorm(x, g, b):
    mu = jnp.mean(x, axis=-1, keepdims=True)
    xc = x - mu
    var = jnp.mean(xc * xc, axis=-1, keepdims=True)
    return xc * lax.rsqrt(var + LN_EPS) * g + b


def _ada_kernel(cs_ref, w_ref, b_ref, o_ref):
    s = _silu(cs_ref[...]).astype(BF16)
    o_ref[0] = _dot(s, w_ref[0].astype(BF16)) + b_ref[0]


def _ada_modulation(cs, ada_w, ada_b):
    depth, d, n = ada_w.shape
    tn = ADA_COL_TILE if n % ADA_COL_TILE == 0 else n
    return pl.pallas_call(
        _ada_kernel,
        grid=(depth, n // tn),
        in_specs=[
            pl.BlockSpec((MOD_ROWS, d), lambda i, j: (0, 0)),
            pl.BlockSpec((1, d, tn), lambda i, j: (i, 0, j)),
            pl.BlockSpec((1, 1, tn), lambda i, j: (i, 0, j)),
        ],
        out_specs=pl.BlockSpec((1, MOD_ROWS, tn), lambda i, j: (i, 0, j)),
        out_shape=jax.ShapeDtypeStruct((depth, MOD_ROWS, n), F32),
        compiler_params=_cparams(("arbitrary", "arbitrary"), 40),
        name="ada_modulation",
    )(cs, ada_w, ada_b.reshape(depth, 1, n))


def _modulate_into(xb_ref, x_ref, sc_l, sh_l, sc_c, sh_c, n_lat):
    xb_ref[:n_lat, :] = (x_ref[0, :n_lat, :] * (1.0 + sc_l[0]) + sh_l[0]).astype(BF16)
    xb_ref[n_lat:, :] = (x_ref[0, n_lat:, :] * (1.0 + sc_c[0]) + sh_c[0]).astype(BF16)


def _row_blocks(n_lat, n_tot):
    blocks = []
    for seg0, seg in ((0, n_lat), (n_lat, n_tot - n_lat)):
        size = PROJ_ROW_BLOCK if seg % PROJ_ROW_BLOCK == 0 else seg
        blocks += [(seg0 + r, size) for r in range(0, seg, size)]
    return blocks


def _pipelined(blocks, produce, consume):
    pending = None
    for blk in blocks:
        cur = produce(*blk)
        if pending is not None:
            consume(*pending)
        pending = blk + (cur,)
    consume(*pending)


def _mod_specs(batch, d, sc_idx, sh_idx):
    return [
        pl.BlockSpec((1, 1, d), lambda b, j: (b, 0, sc_idx)),
        pl.BlockSpec((1, 1, d), lambda b, j: (b, 0, sh_idx)),
        pl.BlockSpec((1, 1, d), lambda b, j: (batch, 0, sc_idx)),
        pl.BlockSpec((1, 1, d), lambda b, j: (batch, 0, sh_idx)),
    ]


def _dn_proj_kernel(x_ref, sc_l, sh_l, sc_c, sh_c, w_ref, conv_ref, alog_ref, dt_ref,
                    qkv_ref, z_ref, gcol_ref, grow_ref, xb_ref, pad_ref,
                    *, n_lat, n_tot, n_qkv_tiles, n_z_tiles):
    j = pl.program_id(1)
    n_ctx = n_tot - n_lat
    halo = SUBLANES
    pad = (DN_CONV - 1) // 2
    lat0 = halo
    ctx0 = 2 * halo + n_lat

    @pl.when(j == 0)
    def _():
        _modulate_into(xb_ref, x_ref, sc_l, sh_l, sc_c, sh_c, n_lat)

    blocks = _row_blocks(n_lat, n_tot)

    def project(r0, rows):
        return _dot(xb_ref[r0:r0 + rows, :], w_ref[...])

    @pl.when(j < n_qkv_tiles)
    def _():
        zeros = jnp.zeros((halo, COL_TILE), F32)
        pad_ref[0:halo, :] = zeros
        pad_ref[lat0 + n_lat:ctx0, :] = zeros
        pad_ref[ctx0 + n_ctx:ctx0 + n_ctx + halo, :] = zeros
        cw = conv_ref[...]
        n_qk_tiles = 2 * n_qkv_tiles // 3
        is_q = j < n_qk_tiles // 2
        is_qk = j < n_qk_tiles

        def padded(r0):
            return r0 + (lat0 if r0 < n_lat else ctx0 - n_lat)

        def stage(r0, rows):
            pad_ref[padded(r0):padded(r0) + rows, :] = project(r0, rows)

        def conv_norm(r0, rows, _):
            base = padded(r0)
            ext = pad_ref[base - halo:base + rows + halo, :]
            y = None
            for t in range(DN_CONV):
                shifted = ext if t == pad else pltpu.roll(ext, (pad - t) % ext.shape[0], 0)
                term = shifted[halo:halo + rows, :] * cw[t:t + 1, :]
                y = term if y is None else y + term
            y = _silu(y)
            for h0 in range(0, COL_TILE, DN_DK):
                yh = y[:, h0:h0 + DN_DK]
                ss = jnp.sum(yh * yh, axis=-1, keepdims=True)
                fac = lax.rsqrt(ss + RMS_EPS)
                fac = jnp.where(is_q, fac * (DN_DK ** -0.5), fac)
                fac = jnp.where(is_qk, fac, jnp.ones_like(fac))
                qkv_ref[0, r0:r0 + rows, h0:h0 + DN_DK] = yh * fac

        _pipelined(blocks, stage, conv_norm)

    @pl.when(jnp.logical_and(j >= n_qkv_tiles, j < n_qkv_tiles + n_z_tiles))
    def _():
        for r0, rows in blocks:
            z_ref[0, r0:r0 + rows, :] = project(r0, rows)

    @pl.when(j == n_qkv_tiles + n_z_tiles)
    def _():
        t = _dot(xb_ref[...], w_ref[:, :LANES])
        lane = lax.broadcasted_iota(jnp.int32, t.shape, 1)
        x = t + dt_ref[...]
        softplus = jnp.maximum(x, 0.0) + jnp.log1p(jnp.exp(-jnp.abs(x)))
        decay = -jnp.exp(alog_ref[...]) * softplus
        gates = jnp.where(lane < 2 * DN_HEADS, decay, jax.nn.sigmoid(t))
        ri = lax.broadcasted_iota(jnp.int32, (DN_CHUNK, DN_CHUNK), 0)
        ci = lax.broadcasted_iota(jnp.int32, (DN_CHUNK, DN_CHUNK), 1)
        tri_f = (ri >= ci).astype(F32)
        tri_b = (ri <= ci).astype(F32)
        lane_c = lax.broadcasted_iota(jnp.int32, (DN_CHUNK, LANES), 1)
        for c in range(n_tot // DN_CHUNK):
            gc = gates[c * DN_CHUNK:(c + 1) * DN_CHUNK]
            cum_f = _dot(tri_f, gc, HIGHEST)
            cum_b = _dot(tri_b, gc, HIGHEST)
            col = jnp.where(lane_c < DN_HEADS, cum_f, jnp.where(lane_c < 2 * DN_HEADS, cum_b, gc))
            gcol_ref[0, c * DN_CHUNK:(c + 1) * DN_CHUNK, :] = col
            grow_ref[0, c] = jnp.concatenate([col, col], axis=0).T


def _dn_project(xc, mod, w16, conv_w, alog_row, dt_row, n_lat):
    batch, n_tot, d = xc.shape
    n_cols = w16.shape[1]
    n_tiles = n_cols // COL_TILE
    hk = DN_HEADS * DN_DK
    n_qkv_tiles = 3 * hk // COL_TILE
    n_z_tiles = hk // COL_TILE
    assert n_tiles == n_qkv_tiles + n_z_tiles + 1
    nc = n_tot // DN_CHUNK
    kern = functools.partial(_dn_proj_kernel, n_lat=n_lat, n_tot=n_tot,
                             n_qkv_tiles=n_qkv_tiles, n_z_tiles=n_z_tiles)
    return pl.pallas_call(
        kern,
        grid=(batch, n_tiles),
        in_specs=[pl.BlockSpec((1, n_tot, d), lambda b, j: (b, 0, 0))]
        + _mod_specs(batch, d, 1, 0)
        + [
            pl.BlockSpec((d, COL_TILE), lambda b, j: (0, j)),
            pl.BlockSpec((DN_CONV, COL_TILE), lambda b, j: (0, jnp.minimum(j, n_qkv_tiles - 1))),
            pl.BlockSpec((1, LANES), lambda b, j: (0, 0)),
            pl.BlockSpec((1, LANES), lambda b, j: (0, 0)),
        ],
        out_specs=[
            pl.BlockSpec((1, n_tot, COL_TILE), lambda b, j: (b, 0, jnp.minimum(j, n_qkv_tiles - 1))),
            pl.BlockSpec((1, n_tot, COL_TILE),
                         lambda b, j: (b, 0, jnp.clip(j - n_qkv_tiles, 0, n_z_tiles - 1))),
            pl.BlockSpec((1, n_tot, LANES), lambda b, j: (b, 0, 0)),
            pl.BlockSpec((1, nc, LANES, LANES), lambda b, j: (b, 0, 0, 0)),
        ],
        out_shape=[
            jax.ShapeDtypeStruct((batch, n_tot, 3 * hk), F32),
            jax.ShapeDtypeStruct((batch, n_tot, hk), F32),
            jax.ShapeDtypeStruct((batch, n_tot, LANES), F32),
            jax.ShapeDtypeStruct((batch, nc, LANES, LANES), F32),
        ],
        scratch_shapes=[
            pltpu.VMEM((n_tot, d), BF16),
            pltpu.VMEM((n_tot + 3 * SUBLANES, COL_TILE), F32),
        ],
        compiler_params=_cparams(("arbitrary", "arbitrary"), 56),
        name="dn_project",
    )(xc, mod, mod, mod, mod, w16, conv_w, alog_row, dt_row)


INV_BASE = 16


def _side_by_side(fwd, bwd):
    return jnp.concatenate([fwd, bwd], axis=1)


def _block_diag16(top, bottom):
    top, bottom = top.astype(BF16), bottom.astype(BF16)
    return jnp.concatenate([_side_by_side(top, jnp.zeros((top.shape[0], bottom.shape[1]), BF16)),
                            _side_by_side(jnp.zeros((bottom.shape[0], top.shape[1]), BF16), bottom)], axis=0)


def _dn_local_stages(ins, slot, sidx, half):
    u_ref, wq_ref, m_ref, e_ref = slot
    c = DN_CHUNK
    st = [dict() for _ in range(DN_HEADS)]

    def each(fn):
        def run():
            for hh, t in enumerate(st):
                fn(t, hh)
        return run

    def grids():
        ri = lax.broadcasted_iota(jnp.int32, (c, 2 * c), 0)
        lane = lax.broadcasted_iota(jnp.int32, (c, 2 * c), 1)
        return ri, lane & (c - 1), lane < c

    def pair_dot(x, y):
        _, _, fwd_half = grids()
        rhs = jnp.concatenate([jnp.where(fwd_half, y, 0.0).astype(BF16),
                               jnp.where(fwd_half, 0.0, y).astype(BF16)], axis=0)
        return _dot(x.astype(BF16), rhs)

    def load(t, hh):
        ri, cj, fwd_half = grids()
        lane = lax.broadcasted_iota(jnp.int32, (c, LANES), 1)
        cols = slice(hh * DN_DK, (hh + 1) * DN_DK)
        per_dir = []
        for d in range(2):
            q_ref, k_ref, v_ref, gcol_ref, grow_ref = ins[d]
            part = half if d == 0 else 1 - half
            rows = slice(part * c, (part + 1) * c)
            gates = gcol_ref[0, rows, :]
            g_lane = d * DN_HEADS + hh
            gcum = jnp.sum(jnp.where(lane == g_lane, gates, 0.0), axis=-1, keepdims=True)
            beta = jnp.sum(jnp.where(lane == 2 * DN_HEADS + g_lane, gates, 0.0), axis=-1, keepdims=True)
            grow = grow_ref[0, part, g_lane:g_lane + 1, :]
            g_last = grow[:, c - 1:c] if d == 0 else grow[:, 0:1]
            per_dir.append((q_ref[0, rows, cols], k_ref[0, rows, cols], v_ref[0, rows, cols],
                            gcum, beta, grow, g_last))
        (qf, kf, vf, gcf, bf, grf, glf), (qb, kb, vb, gcb, bb, grb, glb) = per_dir
        incl = jnp.logical_or(ri == cj, (ri > cj) == fwd_half)
        diff = jnp.where(fwd_half, gcf, gcb) - jnp.where(fwd_half[0:1], grf, grb)
        t["gamma"] = jnp.where(incl, jnp.exp(jnp.where(incl, diff, 0.0)), 0.0)
        kbf, kbb = kf * bf, kb * bb
        dec_f, dec_b = jnp.exp(gcf), jnp.exp(gcb)
        t["lhs16"] = _side_by_side(jnp.concatenate([kbf, qf], axis=0).astype(BF16),
                                   jnp.concatenate([kbb, qb], axis=0).astype(BF16))
        t["keys16"] = _block_diag16(kf, kb)
        t["rhs16"] = _block_diag16(_side_by_side(vf * bf, kbf * dec_f), _side_by_side(vb * bb, kbb * dec_b))
        wq_ref[sidx, hh, c:, :DN_DK] = (qf * dec_f).astype(BF16)
        wq_ref[sidx, hh, c:, DN_DK:] = (qb * dec_b).astype(BF16)
        k_dec = jnp.concatenate([kf * jnp.exp(glf - gcf), kb * jnp.exp(glb - gcb)], axis=0)
        m_ref[sidx, hh, c:, :] = k_dec.T.astype(BF16)
        e_ref[sidx, hh] = _side_by_side(jnp.broadcast_to(jnp.exp(glf), (SUBLANES, DN_DV)),
                                        jnp.broadcast_to(jnp.exp(glb), (SUBLANES, DN_DV)))

    def gram(t, hh):
        t["kq"] = _dot_nt(t.pop("lhs16"), t.pop("keys16"))

    def split(t, hh):
        ri, cj, fwd_half = grids()
        strict = jnp.logical_and(ri != cj, (ri > cj) == fwd_half)
        kq = t.pop("kq")
        gamma = t.pop("gamma")
        a = jnp.where(strict, kq[:c] * gamma, 0.0)
        m_ref[sidx, hh, :c, :] = (kq[c:] * gamma).astype(BF16)
        a_diag = jnp.where((ri // INV_BASE) == (cj // INV_BASE), a, 0.0)
        t["a"] = a
        t["inv"] = (ri == cj).astype(F32) - a_diag
        t["pw"] = pair_dot(a_diag, a_diag)

    def series(t, hh):
        both = pair_dot(jnp.concatenate([t["pw"], t["inv"]], axis=0), t["pw"])
        t["pw"] = both[:c]
        t["inv"] = t["inv"] + both[c:]

    def series_last(t, hh):
        t["inv"] = t["inv"] + pair_dot(t["inv"], t.pop("pw"))

    def merge_a(blk):
        def fn(t, hh):
            ri, cj, _ = grids()
            off = jnp.logical_and((ri // (2 * blk)) == (cj // (2 * blk)), (ri // blk) != (cj // blk))
            t["y"] = pair_dot(jnp.where(off, t["a"], 0.0), t["inv"])
        return fn

    def merge_b(t, hh):
        t["inv"] = t["inv"] - pair_dot(t["inv"], t.pop("y"))

    def solve(t, hh):
        t["uw"] = _dot(t.pop("inv").astype(BF16), t.pop("rhs16"))
        t.pop("a")

    def store(t, hh):
        uw = t.pop("uw")
        for d in range(2):
            base = d * (DN_DV + DN_DK)
            u_ref[sidx, hh, :, d * DN_DV:(d + 1) * DN_DV] = uw[:, base:base + DN_DV]
            wq_ref[sidx, hh, :c, d * DN_DK:(d + 1) * DN_DK] = uw[:, base + DN_DV:base + DN_DV + DN_DK].astype(BF16)

    stages = [load, gram, split] + [series] * (int(math.log2(INV_BASE)) - 2) + [series_last]
    blk = INV_BASE
    while blk < c:
        stages += [merge_a(blk), merge_b]
        blk *= 2
    stages += [solve, store]
    return [each(fn) for fn in stages]


def _dn_serial_stages(outs, s_ref, slot, sidx, half):
    u_ref, wq_ref, m_ref, e_ref = slot
    c = DN_CHUNK
    st = [dict() for _ in range(DN_HEADS)]

    def each(fn):
        def run():
            for hh, t in enumerate(st):
                fn(t, hh)
        return run

    def state_dot(t, hh):
        t["s"] = (s_ref[hh], s_ref[DN_HEADS + hh])
        t["ws"] = _dot(wq_ref[sidx, hh], _block_diag16(*t["s"]))

    def value_dot(t, hh):
        v_new = u_ref[sidx, hh] - t["ws"][:c]
        t["mv"] = _dot(m_ref[sidx, hh], _block_diag16(v_new[:, :DN_DV], v_new[:, DN_DV:]))

    def update(t, hh):
        mv, ws, s = t.pop("mv"), t.pop("ws"), t.pop("s")
        gain = e_ref[sidx, hh][0:1, :]
        for d in range(2):
            lanes = slice(d * DN_DV, (d + 1) * DN_DV)
            s_ref[d * DN_HEADS + hh] = s[d] * gain[:, lanes] + mv[c:, lanes]
            part = half if d == 0 else 1 - half
            outs[d][0, part * c:(part + 1) * c, hh * DN_DV:(hh + 1) * DN_DV] = ws[c:, lanes] + mv[:c, lanes]

    return [each(fn) for fn in (state_dot, value_dot, update)]


def _dn_scan_kernel(*refs):
    ins = (refs[0:5], refs[5:10])
    outs = refs[10:12]
    s_ref = refs[12]
    slot_sets = (refs[13:17], refs[17:21])
    g = pl.program_id(1)

    @pl.when(g == 0)
    def _():
        for ref in (s_ref,) + tuple(slot_sets[0]) + tuple(slot_sets[1]):
            ref[...] = jnp.zeros(ref.shape, ref.dtype)

    def step(written, read):
        loc = [_dn_local_stages(ins, written, half, half) for half in range(2)]
        ser = [stage for half in range(2) for stage in _dn_serial_stages(outs, s_ref, read, half, half)]
        gap = len(loc[0]) // len(ser)
        for n, stages in enumerate(zip(*loc)):
            if n % gap == 0 and n // gap < len(ser):
                ser[n // gap]()
            for stage in stages:
                stage()

    for parity in range(2):
        pl.when(g % 2 == parity)(functools.partial(step, slot_sets[parity], slot_sets[1 - parity]))


def _dn_scan(qkv, gcol, grow, n_lat):
    batch, n_tot, _ = qkv.shape
    hk = DN_HEADS * DN_DK
    pair = 2 * DN_CHUNK
    n_pairs = n_tot // pair
    lat_pairs = n_lat // pair
    ctx_pairs = n_pairs - lat_pairs
    assert n_lat % pair == 0 and n_tot % pair == 0

    def fwd(p):
        return jnp.where(p < ctx_pairs, p + lat_pairs, p - ctx_pairs)

    def bwd(p):
        return n_pairs - 1 - p

    def p_in(g):
        return jnp.minimum(g, n_pairs - 1)

    def p_out(g):
        return jnp.maximum(g - 1, 0)

    def in_specs(where):
        return [pl.BlockSpec((1, pair, hk), lambda b, g, col=col: (b, where(p_in(g)), col)) for col in range(3)] + [
            pl.BlockSpec((1, pair, LANES), lambda b, g: (b, where(p_in(g)), 0)),
            pl.BlockSpec((1, 2, LANES, LANES), lambda b, g: (b, where(p_in(g)), 0, 0)),
        ]

    out_spec = lambda where: pl.BlockSpec((1, pair, hk), lambda b, g: (b, where(p_out(g)), 0))
    out_shape = jax.ShapeDtypeStruct((batch, n_tot, hk), F32)
    slot = [pltpu.VMEM((2, DN_HEADS, DN_CHUNK, 2 * DN_DV), F32),
            pltpu.VMEM((2, DN_HEADS, 2 * DN_CHUNK, 2 * DN_DK), BF16),
            pltpu.VMEM((2, DN_HEADS, DN_CHUNK + DN_DK, 2 * DN_CHUNK), BF16),
            pltpu.VMEM((2, DN_HEADS, SUBLANES, 2 * DN_DV), F32)]
    per_dir = (qkv, qkv, qkv, gcol, grow)
    return pl.pallas_call(
        _dn_scan_kernel,
        grid=(batch, n_pairs + 1),
        in_specs=in_specs(fwd) + in_specs(bwd),
        out_specs=[out_spec(fwd), out_spec(bwd)],
        out_shape=[out_shape, out_shape],
        scratch_shapes=[pltpu.VMEM((2 * DN_HEADS, DN_DK, DN_DV), F32)] + slot + slot,
        compiler_params=_cparams(("arbitrary", "arbitrary"), 48),
        name="dn_scan",
    )(*per_dir, *per_dir)


def _out_ln_kernel(*refs, alpha, gated_norm):
    if gated_norm:
        y_ref, y2_ref, z_ref, ng_ref = refs[:4]
    else:
        y_ref = refs[0]
    x_ref, gt_ref, w_ref, lg_ref, lb_ref, o_ref, yb_ref = refs[-7:]
    if gated_norm:
        for h0 in range(0, y_ref.shape[2], DN_DV):
            yh = y_ref[0, :, h0:h0 + DN_DV] + y2_ref[0, :, h0:h0 + DN_DV]
            ms = jnp.mean(yh * yh, axis=-1, keepdims=True)
            yh = yh * lax.rsqrt(ms + RMS_EPS) * ng_ref[...]
            yb_ref[:, h0:h0 + DN_DV] = (yh * _silu(z_ref[0, :, h0:h0 + DN_DV])).astype(BF16)
    else:
        yb_ref[...] = y_ref[0].astype(BF16)
    proj = _dot(yb_ref[...], w_ref[...])
    o_ref[0] = _layernorm(alpha * x_ref[0] + gt_ref[0] * proj, lg_ref[...], lb_ref[...])


def _out_ln(ys, xc, mod, gt_idx, w16, ln_g, ln_b, alpha, n_lat, n_out, gated_norm):
    batch, _, d = xc.shape
    hv = ys[0].shape[2]
    n_lat_tiles = n_lat // ROW_TILE
    kern = functools.partial(_out_ln_kernel, alpha=alpha, gated_norm=gated_norm)
    tok = lambda width: pl.BlockSpec((1, ROW_TILE, width), lambda b, t: (b, t, 0))
    return pl.pallas_call(
        kern,
        grid=(batch, n_out // ROW_TILE),
        in_specs=([tok(hv)] * 3 + [pl.BlockSpec((1, ys[3].shape[1]), lambda b, t: (0, 0))]
                  if gated_norm else [tok(hv)]) + [
            tok(d),
            pl.BlockSpec((1, 1, d), lambda b, t: (jnp.where(t < n_lat_tiles, b, batch), 0, gt_idx)),
            pl.BlockSpec(memory_space=pltpu.VMEM),
            pl.BlockSpec((1, d), lambda b, t: (0, 0)),
            pl.BlockSpec((1, d), lambda b, t: (0, 0)),
        ],
        out_specs=tok(d),
        out_shape=jax.ShapeDtypeStruct((batch, n_out, d), F32),
        scratch_shapes=[pltpu.VMEM((ROW_TILE, hv), BF16)],
        compiler_params=_cparams(("arbitrary", "arbitrary"), 40),
        name="out_ln",
    )(*ys, xc, mod, w16, ln_g, ln_b)


def _ffn_kernel(x_ref, sc_ref, sh_ref, gt_ref, wg_ref, wu_ref, wd_ref, lg_ref, lb_ref, o_ref, *, alpha):
    x = x_ref[0]
    h = (x * (1.0 + sc_ref[0]) + sh_ref[0]).astype(BF16)
    gate = _dot(h, wg_ref[...])
    up = _dot(h, wu_ref[...])
    act = (_silu(gate) * up).astype(BF16)
    y = _dot(act, wd_ref[...])
    o_ref[0] = _layernorm(alpha * x + gt_ref[0] * y, lg_ref[...], lb_ref[...])


def _ffn(xc, mod, wg16, wu16, wd16, ln_g, ln_b, alpha, n_lat, n_out):
    batch, _, d = xc.shape
    n_lat_tiles = n_lat // ROW_TILE
    row = lambda b, t: jnp.where(t < n_lat_tiles, b, batch)
    kern = functools.partial(_ffn_kernel, alpha=alpha)
    tok = pl.BlockSpec((1, ROW_TILE, d), lambda b, t: (b, t, 0))
    resident = pl.BlockSpec(memory_space=pltpu.VMEM)
    return pl.pallas_call(
        kern,
        grid=(batch, n_out // ROW_TILE),
        in_specs=[
            tok,
            pl.BlockSpec((1, 1, d), lambda b, t: (row(b, t), 0, 4)),
            pl.BlockSpec((1, 1, d), lambda b, t: (row(b, t), 0, 3)),
            pl.BlockSpec((1, 1, d), lambda b, t: (row(b, t), 0, 5)),
            resident, resident, resident,
            pl.BlockSpec((1, d), lambda b, t: (0, 0)),
            pl.BlockSpec((1, d), lambda b, t: (0, 0)),
        ],
        out_specs=tok,
        out_shape=jax.ShapeDtypeStruct((batch, n_out, d), F32),
        compiler_params=_cparams(("arbitrary", "arbitrary"), 52),
        name="ffn",
    )(xc, mod, mod, mod, wg16, wu16, wd16, ln_g, ln_b)


def _da_proj_kernel(x_ref, sc_l, sh_l, sc_c, sh_c, w_ref, cos_ref, sin_a_ref, sin_b_ref,
                    o_ref, xb_ref, *, n_lat, n_qk_tiles):
    j = pl.program_id(1)

    @pl.when(j == 0)
    def _():
        _modulate_into(xb_ref, x_ref, sc_l, sh_l, sc_c, sh_c, n_lat)

    blocks = _row_blocks(n_lat, x_ref.shape[1])

    def project(r0, rows):
        return _dot(xb_ref[r0:r0 + rows, :], w_ref[...])

    @pl.when(j < n_qk_tiles)
    def _():
        scale = jnp.where(j < n_qk_tiles // 2, DA_DIM ** -0.5 * LOG2E, 1.0)
        quarter = DA_DIM // 4

        def rope(r0, rows, acc):
            cos, sin_a, sin_b = (t[r0:r0 + rows, :] for t in (cos_ref, sin_a_ref, sin_b_ref))
            for h0 in range(0, COL_TILE, LANES):
                xh = acc[:, h0:h0 + LANES]
                rot = pltpu.roll(xh, quarter, 1) * sin_a + pltpu.roll(xh, LANES - quarter, 1) * sin_b
                o_ref[0, r0:r0 + rows, h0:h0 + LANES] = ((xh * cos + rot) * scale).astype(BF16)

        _pipelined(blocks, project, rope)

    @pl.when(j >= n_qk_tiles)
    def _():
        for r0, rows in blocks:
            o_ref[0, r0:r0 + rows, :] = project(r0, rows).astype(BF16)


def _da_project(xc, mod, w16, cos_t, sin_a, sin_b, n_lat):
    batch, n_tot, d = xc.shape
    n_cols = w16.shape[1]
    n_tiles = n_cols // COL_TILE
    n_qk_tiles = 2 * n_tiles // 3
    kern = functools.partial(_da_proj_kernel, n_lat=n_lat, n_qk_tiles=n_qk_tiles)
    table = pl.BlockSpec((n_tot, LANES), lambda b, j: (0, 0))
    return pl.pallas_call(
        kern,
        grid=(batch, n_tiles),
        in_specs=[pl.BlockSpec((1, n_tot, d), lambda b, j: (b, 0, 0))]
        + _mod_specs(batch, d, 1, 0)
        + [pl.BlockSpec((d, COL_TILE), lambda b, j: (0, j)), table, table, table],
        out_specs=pl.BlockSpec((1, n_tot, COL_TILE), lambda b, j: (b, 0, j)),
        out_shape=jax.ShapeDtypeStruct((batch, n_tot, n_cols), BF16),
        scratch_shapes=[pltpu.VMEM((n_tot, d), BF16)],
        compiler_params=_cparams(("arbitrary", "arbitrary"), 56),
        name="da_project",
    )(xc, mod, mod, mod, mod, w16, cos_t, sin_a, sin_b)


def _da_attn_kernel(q_ref, k_ref, v_ref, lam_ref, g_ref, o_ref, *, lambda_init):
    lam = lam_ref[...]
    lam_val = (jnp.exp(jnp.sum(lam[0:1] * lam[1:2], axis=-1, keepdims=True))
               - jnp.exp(jnp.sum(lam[2:3] * lam[3:4], axis=-1, keepdims=True)) + lambda_init)
    k = k_ref[0]
    n_sub = q_ref.shape[1] // ATT_SUB_ROWS
    lane = lax.broadcasted_iota(jnp.int32, (ATT_SUB_ROWS, 2 * DA_DIM), 1)

    def scores(r):
        q = q_ref[0, r * ATT_SUB_ROWS:(r + 1) * ATT_SUB_ROWS, :]
        zero = jnp.zeros_like(q)
        return [_dot_nt(jnp.where((lane < DA_DIM) if comp == 0 else (lane >= DA_DIM), q, zero), k)
                for comp in range(2)]

    def weights(s):
        e0, e1 = (jnp.exp2(sc - jnp.max(sc, axis=-1, keepdims=True)) for sc in s)
        l0 = jnp.sum(e0, axis=-1, keepdims=True)
        l1 = jnp.sum(e1, axis=-1, keepdims=True)
        return (e0 - e1 * (lam_val * l0 / l1)).astype(BF16), 1.0 / l0

    def values(r, a16, inv_l0):
        o = _dot(a16, v_ref[0]) * inv_l0
        ms = jnp.mean(o * o, axis=-1, keepdims=True)
        o_ref[0, r * ATT_SUB_ROWS:(r + 1) * ATT_SUB_ROWS, :] = (
            o * lax.rsqrt(ms + RMS_EPS) * g_ref[...] * (1.0 - lambda_init))

    pending = [scores(r) for r in range(min(ATT_LOOKAHEAD, n_sub))]
    for r in range(n_sub):
        if r + ATT_LOOKAHEAD < n_sub:
            pending.append(scores(r + ATT_LOOKAHEAD))
        values(r, *weights(pending.pop(0)))


def _da_attention(qkv16, lam, subln_g, lambda_init, n_lat):
    batch, n_tot, _ = qkv16.shape
    hw = 2 * DA_DIM
    kern = functools.partial(_da_attn_kernel, lambda_init=lambda_init)
    tq = ATT_Q_TILE if n_lat % ATT_Q_TILE == 0 else ROW_TILE
    return pl.pallas_call(
        kern,
        grid=(batch, DA_HEADS, n_lat // tq),
        in_specs=[
            pl.BlockSpec((1, tq, hw), lambda b, h, i: (b, i, h)),
            pl.BlockSpec((1, n_tot, hw), lambda b, h, i: (b, 0, DA_HEADS + h)),
            pl.BlockSpec((1, n_tot, hw), lambda b, h, i: (b, 0, 2 * DA_HEADS + h)),
            pl.BlockSpec(lam.shape, lambda b, h, i: (0, 0)),
            pl.BlockSpec((1, hw), lambda b, h, i: (0, 0)),
        ],
        out_specs=pl.BlockSpec((1, tq, hw), lambda b, h, i: (b, i, h)),
        out_shape=jax.ShapeDtypeStruct((batch, n_lat, DA_HEADS * hw), F32),
        compiler_params=_cparams(("arbitrary", "arbitrary", "arbitrary"), 48),
        name="da_attention",
    )(qkv16, qkv16, qkv16, lam, subln_g)


def _rope_tables(n_lat, n_tot):
    quarter = DA_DIM // 4
    inv_freq = ROPE_BASE ** (-jnp.arange(quarter, dtype=F32) / quarter)
    rows = n_lat // GRID_W
    row = jnp.repeat(jnp.arange(rows, dtype=F32), GRID_W)
    col = jnp.tile(jnp.arange(GRID_W, dtype=F32), rows)
    ang_r = row[:, None] * inv_freq
    ang_c = col[:, None] * inv_freq
    ang = jnp.concatenate([ang_r, ang_r, ang_c, ang_c], axis=-1)
    reps = LANES // DA_DIM
    cos = jnp.tile(jnp.cos(ang), (1, reps))
    sin = jnp.tile(jnp.sin(ang), (1, reps))
    upper = (jnp.arange(LANES) % (2 * quarter)) >= quarter
    sin_a = jnp.where(upper, sin, 0.0)
    sin_b = jnp.where(upper, 0.0, -sin)
    n_ctx = n_tot - n_lat
    ext = lambda t, fill: jnp.concatenate([t, jnp.full((n_ctx, LANES), fill, F32)], axis=0)
    return ext(cos, 1.0), ext(sin_a, 0.0), ext(sin_b, 0.0)


def kernel(x, c, ctx, c_ctx, ada_w, ada_b, ln1_g, ln1_b, ln2_g, ln2_b, ffn_w_gate, ffn_w_up, ffn_w_down,
           dn_w_in, dn_conv, dn_a_log, dn_dt_bias, dn_norm_g, dn_w_out, da_w_in, da_lambda, da_subln_g,
           da_w_out):
    batch, n_lat, d = x.shape
    n_ctx = ctx.shape[1]
    n_tot = n_lat + n_ctx
    depth = ada_w.shape[0]
    assert depth == 2, "layer 0 is gated DeltaNet, layer 1 (the last) differential attention"
    assert batch < MOD_ROWS and n_lat % ROW_TILE == 0 and n_ctx % ROW_TILE == 0
    alpha = (2.0 * depth) ** 0.25
    hk = DN_HEADS * DN_DK

    cs = jnp.concatenate([c, c_ctx[None, :], jnp.zeros((MOD_ROWS - batch - 1, d), F32)], axis=0)
    mods = _ada_modulation(cs, ada_w, ada_b).reshape(depth, MOD_ROWS, 1, 6 * d)

    xc = jnp.concatenate([x, ctx], axis=1)

    mod = mods[0]
    gate_cols = dn_w_in[0][:, 4 * hk:]
    w_in = jnp.concatenate(
        [dn_w_in[0][:, :4 * hk], gate_cols, jnp.zeros((d, COL_TILE - gate_cols.shape[1]), F32)], axis=1)
    lane_pad = lambda v: jnp.concatenate([v.reshape(1, -1), jnp.zeros((1, LANES - v.size), F32)], axis=1)
    qkv, z, gcol, grow = _dn_project(xc, mod, w_in.astype(BF16), dn_conv[0],
                                     lane_pad(dn_a_log[0]), lane_pad(dn_dt_bias[0]), n_lat)
    o_fwd, o_bwd = _dn_scan(qkv, gcol, grow, n_lat)
    xc = _out_ln((o_fwd, o_bwd, z, dn_norm_g[0].reshape(1, -1)), xc, mod, 2, dn_w_out[0].astype(BF16),
                 ln1_g[0].reshape(1, -1), ln1_b[0].reshape(1, -1), alpha, n_lat, n_tot, True)
    xc = _ffn(xc, mod, ffn_w_gate[0].astype(BF16), ffn_w_up[0].astype(BF16), ffn_w_down[0].astype(BF16),
              ln2_g[0].reshape(1, -1), ln2_b[0].reshape(1, -1), alpha, n_lat, n_tot)

    mod = mods[1]
    lambda_init = 0.8 - 0.6 * math.exp(-0.3 * 1)
    cos_t, sin_a, sin_b = _rope_tables(n_lat, n_tot)
    qkv16 = _da_project(xc, mod, da_w_in[0].astype(BF16), cos_t, sin_a, sin_b, n_lat)
    y = _da_attention(qkv16, da_lambda[0], da_subln_g[0].reshape(1, -1), lambda_init, n_lat)
    xl = _out_ln((y,), xc, mod, 2, da_w_out[0].astype(BF16),
                 ln1_g[1].reshape(1, -1), ln1_b[1].reshape(1, -1), alpha, n_lat, n_lat, False)
    return _ffn(xl, mod, ffn_w_gate[1].astype(BF16), ffn_w_up[1].astype(BF16), ffn_w_down[1].astype(BF16),
                ln2_g[1].reshape(1, -1), ln2_b[1].reshape(1, -1), alpha, n_lat, n_lat)
```

```python
import functools
import math

import jax
import jax.numpy as jnp
from jax import lax
from jax.experimental import pallas as pl
from jax.experimental.pallas import tpu as pltpu

F32 = jnp.float32
BF16 = jnp.bfloat16
HIGHEST = lax.Precision.HIGHEST

DN_HEADS = 8
DN_DK = 128
DN_DV = 128
DN_CONV = 5
DN_CHUNK = 64
DA_HEADS = 8
DA_DIM = 64
GRID_W = 64
ROPE_BASE = 10000.0
LN_EPS = 1e-5
RMS_EPS = 1e-6

LANES = 128
SUBLANES = 8
MOD_ROWS = 16
ROW_TILE = 256
FFN_ROW_TILE = 512
PROJ_ROW_BLOCK = 512
COL_TILE = 256
ADA_COL_TILE = 1536
ATT_Q_TILE = 2048
ATT_SUB_ROWS = 256
ATT_LOOKAHEAD = 2
LOG2E = math.log2(math.e)


def _cparams(dims, vmem_mb):
    return pltpu.CompilerParams(dimension_semantics=dims, vmem_limit_bytes=vmem_mb * 1024 * 1024)


def _silu(x):
    return x * jax.nn.sigmoid(x)


def _dot(a, b, precision=None):
    return jnp.dot(a, b, preferred_element_type=F32, precision=precision)


def _dot_nt(a, b):
    return lax.dot_general(a, b, (((1,), (1,)), ((), ())), preferred_element_type=F32)


def _layernorm(x, g, b):
    mu = jnp.mean(x, axis=-1, keepdims=True)
    xc = x - mu
    var = jnp.mean(xc * xc, axis=-1, keepdims=True)
    return xc * lax.rsqrt(var + LN_EPS) * g + b


def _ada_kernel(cs_ref, w_ref, b_ref, o_ref):
    s = _silu(cs_ref[...]).astype(BF16)
    o_ref[0] = _dot(s, w_ref[0].astype(BF16)) + b_ref[0]


def _ada_modulation(cs, ada_w, ada_b):
    depth, d, n = ada_w.shape
    tn = ADA_COL_TILE if n % ADA_COL_TILE == 0 else n
    return pl.pallas_call(
        _ada_kernel,
        grid=(depth, n // tn),
        in_specs=[
            pl.BlockSpec((MOD_ROWS, d), lambda i, j: (0, 0)),
            pl.BlockSpec((1, d, tn), lambda i, j: (i, 0, j)),
            pl.BlockSpec((1, 1, tn), lambda i, j: (i, 0, j)),
        ],
        out_specs=pl.BlockSpec((1, MOD_ROWS, tn), lambda i, j: (i, 0, j)),
        out_shape=jax.ShapeDtypeStruct((depth, MOD_ROWS, n), F32),
        compiler_params=_cparams(("arbitrary", "arbitrary"), 40),
        name="ada_modulation",
    )(cs, ada_w, ada_b.reshape(depth, 1, n))


def _modulate_into(xb_ref, lat, ctx, sc_l, sh_l, sc_c, sh_c):
    n_lat = lat.shape[0]
    xb_ref[:n_lat, :] = (lat * (1.0 + sc_l[0]) + sh_l[0]).astype(BF16)
    xb_ref[n_lat:, :] = (ctx * (1.0 + sc_c[0]) + sh_c[0]).astype(BF16)


def _row_blocks(n_lat, n_tot):
    blocks = []
    for seg0, seg in ((0, n_lat), (n_lat, n_tot - n_lat)):
        size = PROJ_ROW_BLOCK if seg % PROJ_ROW_BLOCK == 0 else seg
        blocks += [(seg0 + r, size) for r in range(0, seg, size)]
    return blocks


def _pipelined(blocks, produce, consume):
    pending = None
    for blk in blocks:
        cur = produce(*blk)
        if pending is not None:
            consume(*pending)
        pending = blk + (cur,)
    consume(*pending)


def _mod_specs(batch, d, sc_idx, sh_idx):
    return [
        pl.BlockSpec((1, 1, d), lambda b, j: (b, 0, sc_idx)),
        pl.BlockSpec((1, 1, d), lambda b, j: (b, 0, sh_idx)),
        pl.BlockSpec((1, 1, d), lambda b, j: (batch, 0, sc_idx)),
        pl.BlockSpec((1, 1, d), lambda b, j: (batch, 0, sh_idx)),
    ]


def _dn_proj_kernel(x_ref, c_ref, sc_l, sh_l, sc_c, sh_c, w_ref, wg_ref, conv_ref, alog_ref, dt_ref,
                    qkv_ref, z_ref, gcol_ref, grow_ref, xb_ref, pad_ref,
                    *, n_lat, n_tot, n_qkv_tiles, n_z_tiles):
    j = pl.program_id(1)
    n_ctx = n_tot - n_lat
    halo = SUBLANES
    pad = (DN_CONV - 1) // 2
    lat0 = halo
    ctx0 = 2 * halo + n_lat

    @pl.when(j == 0)
    def _():
        _modulate_into(xb_ref, x_ref[0], c_ref[0], sc_l, sh_l, sc_c, sh_c)

    blocks = _row_blocks(n_lat, n_tot)

    def projector():
        w16 = w_ref[...].astype(BF16)
        return lambda r0, rows: _dot(xb_ref[r0:r0 + rows, :], w16)

    @pl.when(j < n_qkv_tiles)
    def _():
        project = projector()
        zeros = jnp.zeros((halo, COL_TILE), F32)
        pad_ref[0:halo, :] = zeros
        pad_ref[lat0 + n_lat:ctx0, :] = zeros
        pad_ref[ctx0 + n_ctx:ctx0 + n_ctx + halo, :] = zeros
        cw = conv_ref[...]
        n_qk_tiles = 2 * n_qkv_tiles // 3
        is_q = j < n_qk_tiles // 2
        is_qk = j < n_qk_tiles

        def padded(r0):
            return r0 + (lat0 if r0 < n_lat else ctx0 - n_lat)

        def stage(r0, rows):
            pad_ref[padded(r0):padded(r0) + rows, :] = project(r0, rows)

        def conv_norm(r0, rows, _):
            base = padded(r0)
            ext = pad_ref[base - halo:base + rows + halo, :]
            y = None
            for t in range(DN_CONV):
                shifted = ext if t == pad else pltpu.roll(ext, (pad - t) % ext.shape[0], 0)
                term = shifted[halo:halo + rows, :] * cw[t:t + 1, :]
                y = term if y is None else y + term
            y = _silu(y)
            for h0 in range(0, COL_TILE, DN_DK):
                yh = y[:, h0:h0 + DN_DK]
                ss = jnp.sum(yh * yh, axis=-1, keepdims=True)
                fac = lax.rsqrt(ss + RMS_EPS)
                fac = jnp.where(is_q, fac * (DN_DK ** -0.5), fac)
                fac = jnp.where(is_qk, fac, jnp.ones_like(fac))
                qkv_ref[0, r0:r0 + rows, h0:h0 + DN_DK] = yh * fac

        _pipelined(blocks, stage, conv_norm)

    @pl.when(jnp.logical_and(j >= n_qkv_tiles, j < n_qkv_tiles + n_z_tiles))
    def _():
        project = projector()
        for r0, rows in blocks:
            z_ref[0, r0:r0 + rows, :] = project(r0, rows)

    @pl.when(j == n_qkv_tiles + n_z_tiles)
    def _():
        t = _dot(xb_ref[...], wg_ref[...].astype(BF16))
        lane = lax.broadcasted_iota(jnp.int32, t.shape, 1)
        x = t + dt_ref[...]
        softplus = jnp.maximum(x, 0.0) + jnp.log1p(jnp.exp(-jnp.abs(x)))
        decay = -jnp.exp(alog_ref[...]) * softplus
        gates = jnp.where(lane < 2 * DN_HEADS, decay, jax.nn.sigmoid(t))
        ri = lax.broadcasted_iota(jnp.int32, (DN_CHUNK, DN_CHUNK), 0)
        ci = lax.broadcasted_iota(jnp.int32, (DN_CHUNK, DN_CHUNK), 1)
        tri = jnp.concatenate([(ri >= ci).astype(F32), (ri <= ci).astype(F32)], axis=0)
        lane_c = lax.broadcasted_iota(jnp.int32, (DN_CHUNK, LANES), 1)
        for c in range(n_tot // DN_CHUNK):
            gc = gates[c * DN_CHUNK:(c + 1) * DN_CHUNK]
            cum = _dot(tri, gc, HIGHEST)
            col = jnp.where(lane_c < DN_HEADS, cum[:DN_CHUNK],
                            jnp.where(lane_c < 2 * DN_HEADS, cum[DN_CHUNK:], gc))
            gcol_ref[0, c * DN_CHUNK:(c + 1) * DN_CHUNK, :] = col
            grow_ref[0, c] = jnp.concatenate([col, col], axis=0).T


def _dn_project(x, ctx, mod, w_in, w_gates, conv_w, alog_row, dt_row):
    batch, n_lat, d = x.shape
    n_tot = n_lat + ctx.shape[1]
    hk = DN_HEADS * DN_DK
    n_qkv_tiles = 3 * hk // COL_TILE
    n_z_tiles = hk // COL_TILE
    n_tiles = n_qkv_tiles + n_z_tiles + 1
    nc = n_tot // DN_CHUNK
    kern = functools.partial(_dn_proj_kernel, n_lat=n_lat, n_tot=n_tot,
                             n_qkv_tiles=n_qkv_tiles, n_z_tiles=n_z_tiles)
    return pl.pallas_call(
        kern,
        grid=(batch, n_tiles),
        in_specs=[pl.BlockSpec((1, n_lat, d), lambda b, j: (b, 0, 0)),
                  pl.BlockSpec((1, n_tot - n_lat, d), lambda b, j: (b, 0, 0))]
        + _mod_specs(batch, d, 1, 0)
        + [
            pl.BlockSpec((d, COL_TILE), lambda b, j: (0, jnp.minimum(j, n_tiles - 2))),
            pl.BlockSpec((d, LANES), lambda b, j: (0, 0)),
            pl.BlockSpec((DN_CONV, COL_TILE), lambda b, j: (0, jnp.minimum(j, n_qkv_tiles - 1))),
            pl.BlockSpec((1, LANES), lambda b, j: (0, 0)),
            pl.BlockSpec((1, LANES), lambda b, j: (0, 0)),
        ],
        out_specs=[
            pl.BlockSpec((1, n_tot, COL_TILE), lambda b, j: (b, 0, jnp.minimum(j, n_qkv_tiles - 1))),
            pl.BlockSpec((1, n_tot, COL_TILE),
                         lambda b, j: (b, 0, jnp.clip(j - n_qkv_tiles, 0, n_z_tiles - 1))),
            pl.BlockSpec((1, n_tot, LANES), lambda b, j: (b, 0, 0)),
            pl.BlockSpec((1, nc, LANES, LANES), lambda b, j: (b, 0, 0, 0)),
        ],
        out_shape=[
            jax.ShapeDtypeStruct((batch, n_tot, 3 * hk), F32),
            jax.ShapeDtypeStruct((batch, n_tot, hk), F32),
            jax.ShapeDtypeStruct((batch, n_tot, LANES), F32),
            jax.ShapeDtypeStruct((batch, nc, LANES, LANES), F32),
        ],
        scratch_shapes=[
            pltpu.VMEM((n_tot, d), BF16),
            pltpu.VMEM((n_tot + 3 * SUBLANES, COL_TILE), F32),
        ],
        compiler_params=_cparams(("arbitrary", "arbitrary"), 56),
        name="dn_project",
    )(x, ctx, mod, mod, mod, mod, w_in, w_gates, conv_w, alog_row, dt_row)


INV_BASE = 16


def _side_by_side(fwd, bwd):
    return jnp.concatenate([fwd, bwd], axis=1)


def _block_diag16(top, bottom):
    top, bottom = top.astype(BF16), bottom.astype(BF16)
    return jnp.concatenate([_side_by_side(top, jnp.zeros((top.shape[0], bottom.shape[1]), BF16)),
                            _side_by_side(jnp.zeros((bottom.shape[0], top.shape[1]), BF16), bottom)], axis=0)


def _dn_local_stages(ins, slot, sidx, half):
    u_ref, wq_ref, m_ref, e_ref = slot
    c = DN_CHUNK
    st = [dict() for _ in range(DN_HEADS)]

    def each(fn):
        def run():
            for hh, t in enumerate(st):
                fn(t, hh)
        return run

    def grids():
        ri = lax.broadcasted_iota(jnp.int32, (c, 2 * c), 0)
        lane = lax.broadcasted_iota(jnp.int32, (c, 2 * c), 1)
        return ri, lane & (c - 1), lane < c

    def pair_dot(x, y):
        _, _, fwd_half = grids()
        rhs = jnp.concatenate([jnp.where(fwd_half, y, 0.0).astype(BF16),
                               jnp.where(fwd_half, 0.0, y).astype(BF16)], axis=0)
        return _dot(x.astype(BF16), rhs)

    def load(t, hh):
        ri, cj, fwd_half = grids()
        lane = lax.broadcasted_iota(jnp.int32, (c, LANES), 1)
        cols = slice(hh * DN_DK, (hh + 1) * DN_DK)
        per_dir = []
        for d in range(2):
            q_ref, k_ref, v_ref, gcol_ref, grow_ref = ins[d]
            part = half if d == 0 else 1 - half
            rows = slice(part * c, (part + 1) * c)
            gates = gcol_ref[0, rows, :]
            g_lane = d * DN_HEADS + hh
            gcum = jnp.sum(jnp.where(lane == g_lane, gates, 0.0), axis=-1, keepdims=True)
            beta = jnp.sum(jnp.where(lane == 2 * DN_HEADS + g_lane, gates, 0.0), axis=-1, keepdims=True)
            grow = grow_ref[0, part, g_lane:g_lane + 1, :]
            g_last = grow[:, c - 1:c] if d == 0 else grow[:, 0:1]
            per_dir.append((q_ref[0, rows, cols], k_ref[0, rows, cols], v_ref[0, rows, cols],
                            gcum, beta, grow, g_last))
        (qf, kf, vf, gcf, bf, grf, glf), (qb, kb, vb, gcb, bb, grb, glb) = per_dir
        incl = jnp.logical_or(ri == cj, (ri > cj) == fwd_half)
        diff = jnp.where(fwd_half, gcf, gcb) - jnp.where(fwd_half[0:1], grf, grb)
        t["gamma"] = jnp.where(incl, jnp.exp(jnp.where(incl, diff, 0.0)), 0.0)
        kbf, kbb = kf * bf, kb * bb
        dec_f, dec_b = jnp.exp(gcf), jnp.exp(gcb)
        t["lhs16"] = _side_by_side(jnp.concatenate([kbf, qf], axis=0).astype(BF16),
                                   jnp.concatenate([kbb, qb], axis=0).astype(BF16))
        t["keys16"] = _block_diag16(kf, kb)
        t["rhs16"] = _block_diag16(_side_by_side(vf * bf, kbf * dec_f), _side_by_side(vb * bb, kbb * dec_b))
        wq_ref[sidx, hh, c:, :DN_DK] = (qf * dec_f).astype(BF16)
        wq_ref[sidx, hh, c:, DN_DK:] = (qb * dec_b).astype(BF16)
        k_dec = jnp.concatenate([kf * jnp.exp(glf - gcf), kb * jnp.exp(glb - gcb)], axis=0)
        m_ref[sidx, hh, c:, :] = k_dec.T.astype(BF16)
        e_ref[sidx, hh] = _side_by_side(jnp.broadcast_to(jnp.exp(glf), (SUBLANES, DN_DV)),
                                        jnp.broadcast_to(jnp.exp(glb), (SUBLANES, DN_DV)))

    def gram(t, hh):
        t["kq"] = _dot_nt(t.pop("lhs16"), t.pop("keys16"))

    def split(t, hh):
        ri, cj, fwd_half = grids()
        strict = jnp.logical_and(ri != cj, (ri > cj) == fwd_half)
        kq = t.pop("kq")
        gamma = t.pop("gamma")
        a = jnp.where(strict, kq[:c] * gamma, 0.0)
        m_ref[sidx, hh, :c, :] = (kq[c:] * gamma).astype(BF16)
        a_diag = jnp.where((ri // INV_BASE) == (cj // INV_BASE), a, 0.0)
        t["a"] = a
        t["inv"] = (ri == cj).astype(F32) - a_diag
        t["pw"] = pair_dot(a_diag, a_diag)

    def series(t, hh):
        both = pair_dot(jnp.concatenate([t["pw"], t["inv"]], axis=0), t["pw"])
        t["pw"] = both[:c]
        t["inv"] = t["inv"] + both[c:]

    def series_last(t, hh):
        t["inv"] = t["inv"] + pair_dot(t["inv"], t.pop("pw"))

    def merge_a(blk):
        def fn(t, hh):
            ri, cj, _ = grids()
            off = jnp.logical_and((ri // (2 * blk)) == (cj // (2 * blk)), (ri // blk) != (cj // blk))
            t["y"] = pair_dot(jnp.where(off, t["a"], 0.0), t["inv"])
        return fn

    def merge_b(t, hh):
        t["inv"] = t["inv"] - pair_dot(t["inv"], t.pop("y"))

    def solve(t, hh):
        t["uw"] = _dot(t.pop("inv").astype(BF16), t.pop("rhs16"))
        t.pop("a")

    def store(t, hh):
        uw = t.pop("uw")
        for d in range(2):
            base = d * (DN_DV + DN_DK)
            u_ref[sidx, hh, :, d * DN_DV:(d + 1) * DN_DV] = uw[:, base:base + DN_DV]
            wq_ref[sidx, hh, :c, d * DN_DK:(d + 1) * DN_DK] = uw[:, base + DN_DV:base + DN_DV + DN_DK].astype(BF16)

    stages = [load, gram, split] + [series] * (int(math.log2(INV_BASE)) - 2) + [series_last]
    blk = INV_BASE
    while blk < c:
        stages += [merge_a(blk), merge_b]
        blk *= 2
    stages += [solve, store]
    return [each(fn) for fn in stages]


def _dn_serial_stages(outs, s_ref, slot, sidx, half):
    u_ref, wq_ref, m_ref, e_ref = slot
    c = DN_CHUNK
    st = [dict() for _ in range(DN_HEADS)]

    def each(fn):
        def run():
            for hh, t in enumerate(st):
                fn(t, hh)
        return run

    def state_dot(t, hh):
        t["s"] = (s_ref[hh], s_ref[DN_HEADS + hh])
        t["ws"] = _dot(wq_ref[sidx, hh], _block_diag16(*t["s"]))

    def value_dot(t, hh):
        v_new = u_ref[sidx, hh] - t["ws"][:c]
        t["mv"] = _dot(m_ref[sidx, hh], _block_diag16(v_new[:, :DN_DV], v_new[:, DN_DV:]))

    def update(t, hh):
        mv, ws, s = t.pop("mv"), t.pop("ws"), t.pop("s")
        gain = e_ref[sidx, hh][0:1, :]
        for d in range(2):
            lanes = slice(d * DN_DV, (d + 1) * DN_DV)
            s_ref[d * DN_HEADS + hh] = s[d] * gain[:, lanes] + mv[c:, lanes]
            part = half if d == 0 else 1 - half
            outs[d][0, part * c:(part + 1) * c, hh * DN_DV:(hh + 1) * DN_DV] = ws[c:, lanes] + mv[:c, lanes]

    return [each(fn) for fn in (state_dot, value_dot, update)]


def _dn_scan_kernel(*refs):
    ins = (refs[0:5], refs[5:10])
    outs = refs[10:12]
    s_ref = refs[12]
    slot_sets = (refs[13:17], refs[17:21])
    g = pl.program_id(1)

    @pl.when(g == 0)
    def _():
        for ref in (s_ref,) + tuple(slot_sets[0]) + tuple(slot_sets[1]):
            ref[...] = jnp.zeros(ref.shape, ref.dtype)

    def step(written, read):
        loc = [_dn_local_stages(ins, written, half, half) for half in range(2)]
        ser = [stage for half in range(2) for stage in _dn_serial_stages(outs, s_ref, read, half, half)]
        gap = len(loc[0]) // len(ser)
        for n, stages in enumerate(zip(*loc)):
            if n % gap == 0 and n // gap < len(ser):
                ser[n // gap]()
            for stage in stages:
                stage()

    for parity in range(2):
        pl.when(g % 2 == parity)(functools.partial(step, slot_sets[parity], slot_sets[1 - parity]))


def _dn_scan(qkv, gcol, grow, n_lat):
    batch, n_tot, _ = qkv.shape
    hk = DN_HEADS * DN_DK
    pair = 2 * DN_CHUNK
    n_pairs = n_tot // pair
    lat_pairs = n_lat // pair
    ctx_pairs = n_pairs - lat_pairs
    assert n_lat % pair == 0 and n_tot % pair == 0

    def fwd(p):
        return jnp.where(p < ctx_pairs, p + lat_pairs, p - ctx_pairs)

    def bwd(p):
        return n_pairs - 1 - p

    def p_in(g):
        return jnp.minimum(g, n_pairs - 1)

    def p_out(g):
        return jnp.maximum(g - 1, 0)

    def in_specs(where):
        return [pl.BlockSpec((1, pair, hk), lambda b, g, col=col: (b, where(p_in(g)), col)) for col in range(3)] + [
            pl.BlockSpec((1, pair, LANES), lambda b, g: (b, where(p_in(g)), 0)),
            pl.BlockSpec((1, 2, LANES, LANES), lambda b, g: (b, where(p_in(g)), 0, 0)),
        ]

    out_spec = lambda where: pl.BlockSpec((1, pair, hk), lambda b, g: (b, where(p_out(g)), 0))
    out_shape = jax.ShapeDtypeStruct((batch, n_tot, hk), F32)
    slot = [pltpu.VMEM((2, DN_HEADS, DN_CHUNK, 2 * DN_DV), F32),
            pltpu.VMEM((2, DN_HEADS, 2 * DN_CHUNK, 2 * DN_DK), BF16),
            pltpu.VMEM((2, DN_HEADS, DN_CHUNK + DN_DK, 2 * DN_CHUNK), BF16),
            pltpu.VMEM((2, DN_HEADS, SUBLANES, 2 * DN_DV), F32)]
    per_dir = (qkv, qkv, qkv, gcol, grow)
    return pl.pallas_call(
        _dn_scan_kernel,
        grid=(batch, n_pairs + 1),
        in_specs=in_specs(fwd) + in_specs(bwd),
        out_specs=[out_spec(fwd), out_spec(bwd)],
        out_shape=[out_shape, out_shape],
        scratch_shapes=[pltpu.VMEM((2 * DN_HEADS, DN_DK, DN_DV), F32)] + slot + slot,
        compiler_params=_cparams(("arbitrary", "arbitrary"), 48),
        name="dn_scan",
    )(*per_dir, *per_dir)


def _out_ln_kernel(*refs, alpha, gated_norm, split_residual, n_lat_tiles):
    if gated_norm:
        y_ref, y2_ref, z_ref, ng_ref = refs[:4]
    else:
        y_ref = refs[0]
    gt_ref, w_ref, lg_ref, lb_ref, o_ref, yb_ref = refs[-6:]
    if split_residual:
        x_ref, c_ref = refs[-8:-6]
        residual = jnp.where(pl.program_id(1) < n_lat_tiles, x_ref[0], c_ref[0])
    else:
        residual = refs[-7][0]
    if gated_norm:
        for h0 in range(0, y_ref.shape[2], DN_DV):
            yh = y_ref[0, :, h0:h0 + DN_DV] + y2_ref[0, :, h0:h0 + DN_DV]
            ms = jnp.mean(yh * yh, axis=-1, keepdims=True)
            yh = yh * lax.rsqrt(ms + RMS_EPS) * ng_ref[...]
            yb_ref[:, h0:h0 + DN_DV] = (yh * _silu(z_ref[0, :, h0:h0 + DN_DV])).astype(BF16)
    else:
        yb_ref[...] = y_ref[0].astype(BF16)
    proj = _dot(yb_ref[...], w_ref[...])
    o_ref[0] = _layernorm(alpha * residual + gt_ref[0] * proj, lg_ref[...], lb_ref[...])


def _out_ln(ys, residual, mod, gt_idx, w16, ln_g, ln_b, alpha, n_lat, n_out, gated_norm):
    batch, _, d = residual[0].shape
    hv = ys[0].shape[2]
    n_lat_tiles = n_lat // ROW_TILE
    split = len(residual) == 2
    kern = functools.partial(_out_ln_kernel, alpha=alpha, gated_norm=gated_norm, split_residual=split,
                             n_lat_tiles=n_lat_tiles)
    tok = lambda width: pl.BlockSpec((1, ROW_TILE, width), lambda b, t: (b, t, 0))
    res_specs = [tok(d)] if not split else [
        pl.BlockSpec((1, ROW_TILE, d), lambda b, t: (b, jnp.minimum(t, n_lat_tiles - 1), 0)),
        pl.BlockSpec((1, ROW_TILE, d), lambda b, t: (b, jnp.maximum(t - n_lat_tiles, 0), 0))]
    return pl.pallas_call(
        kern,
        grid=(batch, n_out // ROW_TILE),
        in_specs=([tok(hv)] * 3 + [pl.BlockSpec((1, ys[3].shape[1]), lambda b, t: (0, 0))]
                  if gated_norm else [tok(hv)]) + res_specs + [
            pl.BlockSpec((1, 1, d), lambda b, t: (jnp.where(t < n_lat_tiles, b, batch), 0, gt_idx)),
            pl.BlockSpec(memory_space=pltpu.VMEM),
            pl.BlockSpec((1, d), lambda b, t: (0, 0)),
            pl.BlockSpec((1, d), lambda b, t: (0, 0)),
        ],
        out_specs=tok(d),
        out_shape=jax.ShapeDtypeStruct((batch, n_out, d), F32),
        scratch_shapes=[pltpu.VMEM((ROW_TILE, hv), BF16)],
        compiler_params=_cparams(("arbitrary", "arbitrary"), 40),
        name="out_ln",
    )(*ys, *residual, mod, w16, ln_g, ln_b)


def _ffn_kernel(x_ref, sc_ref, sh_ref, gt_ref, wg_ref, wu_ref, wd_ref, lg_ref, lb_ref, o_ref, *, alpha):
    x = x_ref[0]
    h = (x * (1.0 + sc_ref[0]) + sh_ref[0]).astype(BF16)
    gate = _dot(h, wg_ref[...])
    up = _dot(h, wu_ref[...])
    act = (_silu(gate) * up).astype(BF16)
    y = _dot(act, wd_ref[...])
    o_ref[0] = _layernorm(alpha * x + gt_ref[0] * y, lg_ref[...], lb_ref[...])


def _ffn(xc, mod, wg16, wu16, wd16, ln_g, ln_b, alpha, n_lat, n_out):
    batch, _, d = xc.shape
    tm = FFN_ROW_TILE if n_out % FFN_ROW_TILE == 0 and n_lat % FFN_ROW_TILE == 0 else ROW_TILE
    n_lat_tiles = n_lat // tm
    row = lambda b, t: jnp.where(t < n_lat_tiles, b, batch)
    kern = functools.partial(_ffn_kernel, alpha=alpha)
    tok = pl.BlockSpec((1, tm, d), lambda b, t: (b, t, 0))
    resident = pl.BlockSpec(memory_space=pltpu.VMEM)
    return pl.pallas_call(
        kern,
        grid=(batch, n_out // tm),
        in_specs=[
            tok,
            pl.BlockSpec((1, 1, d), lambda b, t: (row(b, t), 0, 4)),
            pl.BlockSpec((1, 1, d), lambda b, t: (row(b, t), 0, 3)),
            pl.BlockSpec((1, 1, d), lambda b, t: (row(b, t), 0, 5)),
            resident, resident, resident,
            pl.BlockSpec((1, d), lambda b, t: (0, 0)),
            pl.BlockSpec((1, d), lambda b, t: (0, 0)),
        ],
        out_specs=tok,
        out_shape=jax.ShapeDtypeStruct((batch, n_out, d), F32),
        compiler_params=_cparams(("arbitrary", "arbitrary"), 52),
        name="ffn",
    )(xc, mod, mod, mod, wg16, wu16, wd16, ln_g, ln_b)


def _da_proj_kernel(x_ref, sc_l, sh_l, sc_c, sh_c, w_ref, cos_ref, sin_a_ref, sin_b_ref,
                    o_ref, xb_ref, *, n_lat, n_qk_tiles):
    j = pl.program_id(1)

    @pl.when(j == 0)
    def _():
        _modulate_into(xb_ref, x_ref[0, :n_lat, :], x_ref[0, n_lat:, :], sc_l, sh_l, sc_c, sh_c)

    blocks = _row_blocks(n_lat, x_ref.shape[1])

    def projector():
        w16 = w_ref[...].astype(BF16)
        return lambda r0, rows: _dot(xb_ref[r0:r0 + rows, :], w16)

    @pl.when(j < n_qk_tiles)
    def _():
        project = projector()
        scale = jnp.where(j < n_qk_tiles // 2, DA_DIM ** -0.5 * LOG2E, 1.0)
        quarter = DA_DIM // 4

        def rope(r0, rows, acc):
            cos, sin_a, sin_b = (t[r0:r0 + rows, :] for t in (cos_ref, sin_a_ref, sin_b_ref))
            for h0 in range(0, COL_TILE, LANES):
                xh = acc[:, h0:h0 + LANES]
                rot = pltpu.roll(xh, quarter, 1) * sin_a + pltpu.roll(xh, LANES - quarter, 1) * sin_b
                o_ref[0, r0:r0 + rows, h0:h0 + LANES] = ((xh * cos + rot) * scale).astype(BF16)

        _pipelined(blocks, project, rope)

    @pl.when(j >= n_qk_tiles)
    def _():
        project = projector()
        for r0, rows in blocks:
            o_ref[0, r0:r0 + rows, :] = project(r0, rows).astype(BF16)


def _da_project(xc, mod, w_in, cos_t, sin_a, sin_b, n_lat):
    batch, n_tot, d = xc.shape
    n_cols = w_in.shape[1]
    n_tiles = n_cols // COL_TILE
    n_qk_tiles = 2 * n_tiles // 3
    kern = functools.partial(_da_proj_kernel, n_lat=n_lat, n_qk_tiles=n_qk_tiles)
    table = pl.BlockSpec((n_tot, LANES), lambda b, j: (0, 0))
    return pl.pallas_call(
        kern,
        grid=(batch, n_tiles),
        in_specs=[pl.BlockSpec((1, n_tot, d), lambda b, j: (b, 0, 0))]
        + _mod_specs(batch, d, 1, 0)
        + [pl.BlockSpec((d, COL_TILE), lambda b, j: (0, j)), table, table, table],
        out_specs=pl.BlockSpec((1, n_tot, COL_TILE), lambda b, j: (b, 0, j)),
        out_shape=jax.ShapeDtypeStruct((batch, n_tot, n_cols), BF16),
        scratch_shapes=[pltpu.VMEM((n_tot, d), BF16)],
        compiler_params=_cparams(("arbitrary", "arbitrary"), 56),
        name="da_project",
    )(xc, mod, mod, mod, mod, w_in, cos_t, sin_a, sin_b)


def _da_attn_kernel(q_ref, k_ref, v_ref, lam_ref, g_ref, o_ref, *, lambda_init):
    lam = lam_ref[...]
    lam_val = (jnp.exp(jnp.sum(lam[0:1] * lam[1:2], axis=-1, keepdims=True))
               - jnp.exp(jnp.sum(lam[2:3] * lam[3:4], axis=-1, keepdims=True)) + lambda_init)
    k = k_ref[0]
    n_sub = q_ref.shape[1] // ATT_SUB_ROWS
    lane = lax.broadcasted_iota(jnp.int32, (ATT_SUB_ROWS, 2 * DA_DIM), 1)

    def scores(r):
        q = q_ref[0, r * ATT_SUB_ROWS:(r + 1) * ATT_SUB_ROWS, :]
        zero = jnp.zeros_like(q)
        return [_dot_nt(jnp.where((lane < DA_DIM) if comp == 0 else (lane >= DA_DIM), q, zero), k)
                for comp in range(2)]

    def weights(s):
        e0, e1 = (jnp.exp2(sc - jnp.max(sc, axis=-1, keepdims=True)) for sc in s)
        l0 = jnp.sum(e0, axis=-1, keepdims=True)
        l1 = jnp.sum(e1, axis=-1, keepdims=True)
        return (e0 - e1 * (lam_val * l0 / l1)).astype(BF16), 1.0 / l0

    def values(r, a16, inv_l0):
        o = _dot(a16, v_ref[0]) * inv_l0
        ms = jnp.mean(o * o, axis=-1, keepdims=True)
        o_ref[0, r * ATT_SUB_ROWS:(r + 1) * ATT_SUB_ROWS, :] = (
            o * lax.rsqrt(ms + RMS_EPS) * g_ref[...] * (1.0 - lambda_init))

    pending = [scores(r) for r in range(min(ATT_LOOKAHEAD, n_sub))]
    for r in range(n_sub):
        if r + ATT_LOOKAHEAD < n_sub:
            pending.append(scores(r + ATT_LOOKAHEAD))
        values(r, *weights(pending.pop(0)))


def _da_attention(qkv16, lam, subln_g, lambda_init, n_lat):
    batch, n_tot, _ = qkv16.shape
    hw = 2 * DA_DIM
    kern = functools.partial(_da_attn_kernel, lambda_init=lambda_init)
    tq = ATT_Q_TILE if n_lat % ATT_Q_TILE == 0 else ROW_TILE
    return pl.pallas_call(
        kern,
        grid=(batch, DA_HEADS, n_lat // tq),
        in_specs=[
            pl.BlockSpec((1, tq, hw), lambda b, h, i: (b, i, h)),
            pl.BlockSpec((1, n_tot, hw), lambda b, h, i: (b, 0, DA_HEADS + h)),
            pl.BlockSpec((1, n_tot, hw), lambda b, h, i: (b, 0, 2 * DA_HEADS + h)),
            pl.BlockSpec(lam.shape, lambda b, h, i: (0, 0)),
            pl.BlockSpec((1, hw), lambda b, h, i: (0, 0)),
        ],
        out_specs=pl.BlockSpec((1, tq, hw), lambda b, h, i: (b, i, h)),
        out_shape=jax.ShapeDtypeStruct((batch, n_lat, DA_HEADS * hw), F32),
        compiler_params=_cparams(("arbitrary", "arbitrary", "arbitrary"), 48),
        name="da_attention",
    )(qkv16, qkv16, qkv16, lam, subln_g)


def _rope_tables(n_lat, n_tot):
    quarter = DA_DIM // 4
    inv_freq = ROPE_BASE ** (-jnp.arange(quarter, dtype=F32) / quarter)
    rows = n_lat // GRID_W
    row = jnp.repeat(jnp.arange(rows, dtype=F32), GRID_W)
    col = jnp.tile(jnp.arange(GRID_W, dtype=F32), rows)
    ang_r = row[:, None] * inv_freq
    ang_c = col[:, None] * inv_freq
    ang = jnp.concatenate([ang_r, ang_r, ang_c, ang_c], axis=-1)
    reps = LANES // DA_DIM
    cos = jnp.tile(jnp.cos(ang), (1, reps))
    sin = jnp.tile(jnp.sin(ang), (1, reps))
    upper = (jnp.arange(LANES) % (2 * quarter)) >= quarter
    sin_a = jnp.where(upper, sin, 0.0)
    sin_b = jnp.where(upper, 0.0, -sin)
    n_ctx = n_tot - n_lat
    ext = lambda t, fill: jnp.concatenate([t, jnp.full((n_ctx, LANES), fill, F32)], axis=0)
    return ext(cos, 1.0), ext(sin_a, 0.0), ext(sin_b, 0.0)


def kernel(x, c, ctx, c_ctx, ada_w, ada_b, ln1_g, ln1_b, ln2_g, ln2_b, ffn_w_gate, ffn_w_up, ffn_w_down,
           dn_w_in, dn_conv, dn_a_log, dn_dt_bias, dn_norm_g, dn_w_out, da_w_in, da_lambda, da_subln_g,
           da_w_out):
    batch, n_lat, d = x.shape
    n_ctx = ctx.shape[1]
    n_tot = n_lat + n_ctx
    depth = ada_w.shape[0]
    assert depth == 2, "layer 0 is gated DeltaNet, layer 1 (the last) differential attention"
    assert batch < MOD_ROWS and n_lat % ROW_TILE == 0 and n_ctx % ROW_TILE == 0
    alpha = (2.0 * depth) ** 0.25
    hk = DN_HEADS * DN_DK

    cs = jnp.concatenate([c, c_ctx[None, :], jnp.zeros((MOD_ROWS - batch - 1, d), F32)], axis=0)
    mods = _ada_modulation(cs, ada_w, ada_b).reshape(depth, MOD_ROWS, 1, 6 * d)

    mod = mods[0]
    gate_cols = dn_w_in[0][:, 4 * hk:]
    w_gates = jnp.concatenate([gate_cols, jnp.zeros((d, LANES - gate_cols.shape[1]), F32)], axis=1)
    lane_pad = lambda v: jnp.concatenate([v.reshape(1, -1), jnp.zeros((1, LANES - v.size), F32)], axis=1)
    qkv, z, gcol, grow = _dn_project(x, ctx, mod, dn_w_in[0], w_gates, dn_conv[0],
                                     lane_pad(dn_a_log[0]), lane_pad(dn_dt_bias[0]))
    o_fwd, o_bwd = _dn_scan(qkv, gcol, grow, n_lat)
    xc = _out_ln((o_fwd, o_bwd, z, dn_norm_g[0].reshape(1, -1)), (x, ctx), mod, 2, dn_w_out[0].astype(BF16),
                 ln1_g[0].reshape(1, -1), ln1_b[0].reshape(1, -1), alpha, n_lat, n_tot, True)
    xc = _ffn(xc, mod, ffn_w_gate[0].astype(BF16), ffn_w_up[0].astype(BF16), ffn_w_down[0].astype(BF16),
              ln2_g[0].reshape(1, -1), ln2_b[0].reshape(1, -1), alpha, n_lat, n_tot)

    mod = mods[1]
    lambda_init = 0.8 - 0.6 * math.exp(-0.3 * 1)
    cos_t, sin_a, sin_b = _rope_tables(n_lat, n_tot)
    qkv16 = _da_project(xc, mod, da_w_in[0], cos_t, sin_a, sin_b, n_lat)
    y = _da_attention(qkv16, da_lambda[0], da_subln_g[0].reshape(1, -1), lambda_init, n_lat)
    xl = _out_ln((y,), (xc,), mod, 2, da_w_out[0].astype(BF16),
                 ln1_g[1].reshape(1, -1), ln1_b[1].reshape(1, -1), alpha, n_lat, n_lat, False)
    return _ffn(xl, mod, ffn_w_gate[1].astype(BF16), ffn_w_up[1].astype(BF16), ffn_w_down[1].astype(BF16),
                ln2_g[1].reshape(1, -1), ln2_b[1].reshape(1, -1), alpha, n_lat, n_lat)
```

```python
import functools
import math

import jax
import jax.numpy as jnp
from jax import lax
from jax.experimental import pallas as pl
from jax.experimental.pallas import tpu as pltpu

F32 = jnp.float32
BF16 = jnp.bfloat16
HIGHEST = lax.Precision.HIGHEST

DN_HEADS = 8
DN_DK = 128
DN_DV = 128
DN_CONV = 5
DN_CHUNK = 64
DA_HEADS = 8
DA_DIM = 64
GRID_W = 64
ROPE_BASE = 10000.0
LN_EPS = 1e-5
RMS_EPS = 1e-6

LANES = 128
SUBLANES = 8
MOD_ROWS = 16
ROW_TILE = 256
PROJ_ROW_BLOCK = 512
COL_TILE = 256
ADA_COL_TILE = 1536
ATT_Q_TILE = 2048
ATT_SUB_ROWS = 256
ATT_LOOKAHEAD = 2
LOG2E = math.log2(math.e)


def _cparams(dims, vmem_mb):
    return pltpu.CompilerParams(dimension_semantics=dims, vmem_limit_bytes=vmem_mb * 1024 * 1024)


def _silu(x):
    return x * jax.nn.sigmoid(x)


def _dot(a, b, precision=None):
    return jnp.dot(a, b, preferred_element_type=F32, precision=precision)


def _dot_nt(a, b):
    return lax.dot_general(a, b, (((1,), (1,)), ((), ())), preferred_element_type=F32)


def _layernorm(x, g, b):
    mu = jnp.mean(x, axis=-1, keepdims=True)
    xc = x - mu
    var = jnp.mean(xc * xc, axis=-1, keepdims=True)
    return xc * lax.rsqrt(var + LN_EPS) * g + b


def _ada_kernel(cs_ref, w_ref, b_ref, o_ref):
    s = _silu(cs_ref[...]).astype(BF16)
    o_ref[0] = _dot(s, w_ref[0].astype(BF16)) + b_ref[0]


def _ada_modulation(cs, ada_w, ada_b):
    depth, d, n = ada_w.shape
    tn = ADA_COL_TILE if n % ADA_COL_TILE == 0 else n
    return pl.pallas_call(
        _ada_kernel,
        grid=(depth, n // tn),
        in_specs=[
            pl.BlockSpec((MOD_ROWS, d), lambda i, j: (0, 0)),
            pl.BlockSpec((1, d, tn), lambda i, j: (i, 0, j)),
            pl.BlockSpec((1, 1, tn), lambda i, j: (i, 0, j)),
        ],
        out_specs=pl.BlockSpec((1, MOD_ROWS, tn), lambda i, j: (i, 0, j)),
        out_shape=jax.ShapeDtypeStruct((depth, MOD_ROWS, n), F32),
        compiler_params=_cparams(("arbitrary", "arbitrary"), 40),
        name="ada_modulation",
    )(cs, ada_w, ada_b.reshape(depth, 1, n))


def _modulate_into(xb_ref, lat, ctx, sc_l, sh_l, sc_c, sh_c):
    n_lat = lat.shape[0]
    xb_ref[:n_lat, :] = (lat * (1.0 + sc_l[0]) + sh_l[0]).astype(BF16)
    xb_ref[n_lat:, :] = (ctx * (1.0 + sc_c[0]) + sh_c[0]).astype(BF16)


def _row_blocks(n_lat, n_tot):
    blocks = []
    for seg0, seg in ((0, n_lat), (n_lat, n_tot - n_lat)):
        size = PROJ_ROW_BLOCK if seg % PROJ_ROW_BLOCK == 0 else seg
        blocks += [(seg0 + r, size) for r in range(0, seg, size)]
    return blocks


def _pipelined(blocks, produce, consume):
    pending = None
    for blk in blocks:
        cur = produce(*blk)
        if pending is not None:
            consume(*pending)
        pending = blk + (cur,)
    consume(*pending)


def _mod_specs(batch, d, sc_idx, sh_idx):
    return [
        pl.BlockSpec((1, 1, d), lambda b, j: (b, 0, sc_idx)),
        pl.BlockSpec((1, 1, d), lambda b, j: (b, 0, sh_idx)),
        pl.BlockSpec((1, 1, d), lambda b, j: (batch, 0, sc_idx)),
        pl.BlockSpec((1, 1, d), lambda b, j: (batch, 0, sh_idx)),
    ]


def _dn_proj_kernel(x_ref, c_ref, sc_l, sh_l, sc_c, sh_c, w_ref, wg_ref, conv_ref, alog_ref, dt_ref,
                    qkv_ref, z_ref, gcol_ref, grow_ref, xb_ref, pad_ref,
                    *, n_lat, n_tot, n_qkv_tiles, n_z_tiles):
    j = pl.program_id(1)
    n_ctx = n_tot - n_lat
    halo = SUBLANES
    pad = (DN_CONV - 1) // 2
    lat0 = halo
    ctx0 = 2 * halo + n_lat

    @pl.when(j == 0)
    def _():
        _modulate_into(xb_ref, x_ref[0], c_ref[0], sc_l, sh_l, sc_c, sh_c)

    blocks = _row_blocks(n_lat, n_tot)

    def projector():
        w16 = w_ref[...].astype(BF16)
        return lambda r0, rows: _dot(xb_ref[r0:r0 + rows, :], w16)

    @pl.when(j < n_qkv_tiles)
    def _():
        project = projector()
        zeros = jnp.zeros((halo, COL_TILE), F32)
        pad_ref[0:halo, :] = zeros
        pad_ref[lat0 + n_lat:ctx0, :] = zeros
        pad_ref[ctx0 + n_ctx:ctx0 + n_ctx + halo, :] = zeros
        cw = conv_ref[...]
        n_qk_tiles = 2 * n_qkv_tiles // 3
        is_q = j < n_qk_tiles // 2
        is_qk = j < n_qk_tiles

        def padded(r0):
            return r0 + (lat0 if r0 < n_lat else ctx0 - n_lat)

        def stage(r0, rows):
            pad_ref[padded(r0):padded(r0) + rows, :] = project(r0, rows)

        def conv_norm(r0, rows, _):
            base = padded(r0)
            ext = pad_ref[base - halo:base + rows + halo, :]
            y = None
            for t in range(DN_CONV):
                shifted = ext if t == pad else pltpu.roll(ext, (pad - t) % ext.shape[0], 0)
                term = shifted[halo:halo + rows, :] * cw[t:t + 1, :]
                y = term if y is None else y + term
            y = _silu(y)
            for h0 in range(0, COL_TILE, DN_DK):
                yh = y[:, h0:h0 + DN_DK]
                ss = jnp.sum(yh * yh, axis=-1, keepdims=True)
                fac = lax.rsqrt(ss + RMS_EPS)
                fac = jnp.where(is_q, fac * (DN_DK ** -0.5), fac)
                fac = jnp.where(is_qk, fac, jnp.ones_like(fac))
                qkv_ref[0, r0:r0 + rows, h0:h0 + DN_DK] = yh * fac

        _pipelined(blocks, stage, conv_norm)

    @pl.when(jnp.logical_and(j >= n_qkv_tiles, j < n_qkv_tiles + n_z_tiles))
    def _():
        project = projector()
        for r0, rows in blocks:
            z_ref[0, r0:r0 + rows, :] = project(r0, rows)

    @pl.when(j == n_qkv_tiles + n_z_tiles)
    def _():
        t = _dot(xb_ref[...], wg_ref[...].astype(BF16))
        lane = lax.broadcasted_iota(jnp.int32, t.shape, 1)
        x = t + dt_ref[...]
        softplus = jnp.maximum(x, 0.0) + jnp.log1p(jnp.exp(-jnp.abs(x)))
        decay = -jnp.exp(alog_ref[...]) * softplus
        gates = jnp.where(lane < 2 * DN_HEADS, decay, jax.nn.sigmoid(t))
        ri = lax.broadcasted_iota(jnp.int32, (DN_CHUNK, DN_CHUNK), 0)
        ci = lax.broadcasted_iota(jnp.int32, (DN_CHUNK, DN_CHUNK), 1)
        tri = jnp.concatenate([(ri >= ci).astype(F32), (ri <= ci).astype(F32)], axis=0)
        lane_c = lax.broadcasted_iota(jnp.int32, (DN_CHUNK, LANES), 1)
        for c in range(n_tot // DN_CHUNK):
            gc = gates[c * DN_CHUNK:(c + 1) * DN_CHUNK]
            cum = _dot(tri, gc, HIGHEST)
            col = jnp.where(lane_c < DN_HEADS, cum[:DN_CHUNK],
                            jnp.where(lane_c < 2 * DN_HEADS, cum[DN_CHUNK:], gc))
            gcol_ref[0, c * DN_CHUNK:(c + 1) * DN_CHUNK, :] = col
            grow_ref[0, c] = jnp.concatenate([col, col], axis=0).T


def _dn_project(x, ctx, mod, w_in, w_gates, conv_w, alog_row, dt_row):
    batch, n_lat, d = x.shape
    n_tot = n_lat + ctx.shape[1]
    hk = DN_HEADS * DN_DK
    n_qkv_tiles = 3 * hk // COL_TILE
    n_z_tiles = hk // COL_TILE
    n_tiles = n_qkv_tiles + n_z_tiles + 1
    nc = n_tot // DN_CHUNK
    kern = functools.partial(_dn_proj_kernel, n_lat=n_lat, n_tot=n_tot,
                             n_qkv_tiles=n_qkv_tiles, n_z_tiles=n_z_tiles)
    return pl.pallas_call(
        kern,
        grid=(batch, n_tiles),
        in_specs=[pl.BlockSpec((1, n_lat, d), lambda b, j: (b, 0, 0)),
                  pl.BlockSpec((1, n_tot - n_lat, d), lambda b, j: (b, 0, 0))]
        + _mod_specs(batch, d, 1, 0)
        + [
            pl.BlockSpec((d, COL_TILE), lambda b, j: (0, jnp.minimum(j, n_tiles - 2))),
            pl.BlockSpec((d, LANES), lambda b, j: (0, 0)),
            pl.BlockSpec((DN_CONV, COL_TILE), lambda b, j: (0, jnp.minimum(j, n_qkv_tiles - 1))),
            pl.BlockSpec((1, LANES), lambda b, j: (0, 0)),
            pl.BlockSpec((1, LANES), lambda b, j: (0, 0)),
        ],
        out_specs=[
            pl.BlockSpec((1, n_tot, COL_TILE), lambda b, j: (b, 0, jnp.minimum(j, n_qkv_tiles - 1))),
            pl.BlockSpec((1, n_tot, COL_TILE),
                         lambda b, j: (b, 0, jnp.clip(j - n_qkv_tiles, 0, n_z_tiles - 1))),
            pl.BlockSpec((1, n_tot, LANES), lambda b, j: (b, 0, 0)),
            pl.BlockSpec((1, nc, LANES, LANES), lambda b, j: (b, 0, 0, 0)),
        ],
        out_shape=[
            jax.ShapeDtypeStruct((batch, n_tot, 3 * hk), F32),
            jax.ShapeDtypeStruct((batch, n_tot, hk), F32),
            jax.ShapeDtypeStruct((batch, n_tot, LANES), F32),
            jax.ShapeDtypeStruct((batch, nc, LANES, LANES), F32),
        ],
        scratch_shapes=[
            pltpu.VMEM((n_tot, d), BF16),
            pltpu.VMEM((n_tot + 3 * SUBLANES, COL_TILE), F32),
        ],
        compiler_params=_cparams(("arbitrary", "arbitrary"), 56),
        name="dn_project",
    )(x, ctx, mod, mod, mod, mod, w_in, w_gates, conv_w, alog_row, dt_row)


INV_BASE = 16


def _side_by_side(fwd, bwd):
    return jnp.concatenate([fwd, bwd], axis=1)


def _block_diag16(top, bottom):
    top, bottom = top.astype(BF16), bottom.astype(BF16)
    return jnp.concatenate([_side_by_side(top, jnp.zeros((top.shape[0], bottom.shape[1]), BF16)),
                            _side_by_side(jnp.zeros((bottom.shape[0], top.shape[1]), BF16), bottom)], axis=0)


def _dn_local_stages(ins, slot, sidx, half):
    u_ref, wq_ref, m_ref, e_ref = slot
    c = DN_CHUNK
    st = [dict() for _ in range(DN_HEADS)]

    def each(fn):
        def run():
            for hh, t in enumerate(st):
                fn(t, hh)
        return run

    def grids():
        ri = lax.broadcasted_iota(jnp.int32, (c, 2 * c), 0)
        lane = lax.broadcasted_iota(jnp.int32, (c, 2 * c), 1)
        return ri, lane & (c - 1), lane < c

    def pair_dot(x, y):
        _, _, fwd_half = grids()
        rhs = jnp.concatenate([jnp.where(fwd_half, y, 0.0).astype(BF16),
                               jnp.where(fwd_half, 0.0, y).astype(BF16)], axis=0)
        return _dot(x.astype(BF16), rhs)

    def load(t, hh):
        ri, cj, fwd_half = grids()
        lane = lax.broadcasted_iota(jnp.int32, (c, LANES), 1)
        cols = slice(hh * DN_DK, (hh + 1) * DN_DK)
        per_dir = []
        for d in range(2):
            q_ref, k_ref, v_ref, gcol_ref, grow_ref = ins[d]
            part = half if d == 0 else 1 - half
            rows = slice(part * c, (part + 1) * c)
            gates = gcol_ref[0, rows, :]
            g_lane = d * DN_HEADS + hh
            gcum = jnp.sum(jnp.where(lane == g_lane, gates, 0.0), axis=-1, keepdims=True)
            beta = jnp.sum(jnp.where(lane == 2 * DN_HEADS + g_lane, gates, 0.0), axis=-1, keepdims=True)
            grow = grow_ref[0, part, g_lane:g_lane + 1, :]
            g_last = grow[:, c - 1:c] if d == 0 else grow[:, 0:1]
            per_dir.append((q_ref[0, rows, cols], k_ref[0, rows, cols], v_ref[0, rows, cols],
                            gcum, beta, grow, g_last))
        (qf, kf, vf, gcf, bf, grf, glf), (qb, kb, vb, gcb, bb, grb, glb) = per_dir
        incl = jnp.logical_or(ri == cj, (ri > cj) == fwd_half)
        diff = jnp.where(fwd_half, gcf, gcb) - jnp.where(fwd_half[0:1], grf, grb)
        t["gamma"] = jnp.where(incl, jnp.exp(jnp.where(incl, diff, 0.0)), 0.0)
        kbf, kbb = kf * bf, kb * bb
        dec_f, dec_b = jnp.exp(gcf), jnp.exp(gcb)
        t["lhs16"] = _side_by_side(jnp.concatenate([kbf, qf], axis=0).astype(BF16),
                                   jnp.concatenate([kbb, qb], axis=0).astype(BF16))
        t["keys16"] = _block_diag16(kf, kb)
        t["rhs16"] = _block_diag16(_side_by_side(vf * bf, kbf * dec_f), _side_by_side(vb * bb, kbb * dec_b))
        wq_ref[sidx, hh, c:, :DN_DK] = (qf * dec_f).astype(BF16)
        wq_ref[sidx, hh, c:, DN_DK:] = (qb * dec_b).astype(BF16)
        k_dec = jnp.concatenate([kf * jnp.exp(glf - gcf), kb * jnp.exp(glb - gcb)], axis=0)
        m_ref[sidx, hh, c:, :] = k_dec.T.astype(BF16)
        e_ref[sidx, hh] = _side_by_side(jnp.broadcast_to(jnp.exp(glf), (SUBLANES, DN_DV)),
                                        jnp.broadcast_to(jnp.exp(glb), (SUBLANES, DN_DV)))

    def gram(t, hh):
        t["kq"] = _dot_nt(t.pop("lhs16"), t.pop("keys16"))

    def split(t, hh):
        ri, cj, fwd_half = grids()
        strict = jnp.logical_and(ri != cj, (ri > cj) == fwd_half)
        kq = t.pop("kq")
        gamma = t.pop("gamma")
        a = jnp.where(strict, kq[:c] * gamma, 0.0)
        m_ref[sidx, hh, :c, :] = (kq[c:] * gamma).astype(BF16)
        a_diag = jnp.where((ri // INV_BASE) == (cj // INV_BASE), a, 0.0)
        t["a"] = a
        t["inv"] = (ri == cj).astype(F32) - a_diag
        t["pw"] = pair_dot(a_diag, a_diag)

    def series(t, hh):
        both = pair_dot(jnp.concatenate([t["pw"], t["inv"]], axis=0), t["pw"])
        t["pw"] = both[:c]
        t["inv"] = t["inv"] + both[c:]

    def series_last(t, hh):
        t["inv"] = t["inv"] + pair_dot(t["inv"], t.pop("pw"))

    def merge_a(blk):
        def fn(t, hh):
            ri, cj, _ = grids()
            off = jnp.logical_and((ri // (2 * blk)) == (cj // (2 * blk)), (ri // blk) != (cj // blk))
            t["y"] = pair_dot(jnp.where(off, t["a"], 0.0), t["inv"])
        return fn

    def merge_b(t, hh):
        t["inv"] = t["inv"] - pair_dot(t["inv"], t.pop("y"))

    def solve(t, hh):
        t["uw"] = _dot(t.pop("inv").astype(BF16), t.pop("rhs16"))
        t.pop("a")

    def store(t, hh):
        uw = t.pop("uw")
        for d in range(2):
            base = d * (DN_DV + DN_DK)
            u_ref[sidx, hh, :, d * DN_DV:(d + 1) * DN_DV] = uw[:, base:base + DN_DV]
            wq_ref[sidx, hh, :c, d * DN_DK:(d + 1) * DN_DK] = uw[:, base + DN_DV:base + DN_DV + DN_DK].astype(BF16)

    stages = [load, gram, split] + [series] * (int(math.log2(INV_BASE)) - 2) + [series_last]
    blk = INV_BASE
    while blk < c:
        stages += [merge_a(blk), merge_b]
        blk *= 2
    stages += [solve, store]
    return [each(fn) for fn in stages]


def _dn_serial_stages(outs, s_ref, slot, sidx, half):
    u_ref, wq_ref, m_ref, e_ref = slot
    c = DN_CHUNK
    st = [dict() for _ in range(DN_HEADS)]

    def each(fn):
        def run():
            for hh, t in enumerate(st):
                fn(t, hh)
        return run

    def state_dot(t, hh):
        t["s"] = (s_ref[hh], s_ref[DN_HEADS + hh])
        t["ws"] = _dot(wq_ref[sidx, hh], _block_diag16(*t["s"]))

    def value_dot(t, hh):
        v_new = u_ref[sidx, hh] - t["ws"][:c]
        t["mv"] = _dot(m_ref[sidx, hh], _block_diag16(v_new[:, :DN_DV], v_new[:, DN_DV:]))

    def update(t, hh):
        mv, ws, s = t.pop("mv"), t.pop("ws"), t.pop("s")
        gain = e_ref[sidx, hh][0:1, :]
        for d in range(2):
            lanes = slice(d * DN_DV, (d + 1) * DN_DV)
            s_ref[d * DN_HEADS + hh] = s[d] * gain[:, lanes] + mv[c:, lanes]
            part = half if d == 0 else 1 - half
            outs[d][0, part * c:(part + 1) * c, hh * DN_DV:(hh + 1) * DN_DV] = ws[c:, lanes] + mv[:c, lanes]

    return [each(fn) for fn in (state_dot, value_dot, update)]


def _dn_scan_kernel(*refs):
    ins = (refs[0:5], refs[5:10])
    outs = refs[10:12]
    s_ref = refs[12]
    slot_sets = (refs[13:17], refs[17:21])
    g = pl.program_id(1)

    @pl.when(g == 0)
    def _():
        for ref in (s_ref,) + tuple(slot_sets[0]) + tuple(slot_sets[1]):
            ref[...] = jnp.zeros(ref.shape, ref.dtype)

    def step(written, read):
        loc = [_dn_local_stages(ins, written, half, half) for half in range(2)]
        ser = [stage for half in range(2) for stage in _dn_serial_stages(outs, s_ref, read, half, half)]
        gap = len(loc[0]) // len(ser)
        for n, stages in enumerate(zip(*loc)):
            if n % gap == 0 and n // gap < len(ser):
                ser[n // gap]()
            for stage in stages:
                stage()

    for parity in range(2):
        pl.when(g % 2 == parity)(functools.partial(step, slot_sets[parity], slot_sets[1 - parity]))


def _dn_scan(qkv, gcol, grow, n_lat):
    batch, n_tot, _ = qkv.shape
    hk = DN_HEADS * DN_DK
    pair = 2 * DN_CHUNK
    n_pairs = n_tot // pair
    lat_pairs = n_lat // pair
    ctx_pairs = n_pairs - lat_pairs
    assert n_lat % pair == 0 and n_tot % pair == 0

    def fwd(p):
        return jnp.where(p < ctx_pairs, p + lat_pairs, p - ctx_pairs)

    def bwd(p):
        return n_pairs - 1 - p

    def p_in(g):
        return jnp.minimum(g, n_pairs - 1)

    def p_out(g):
        return jnp.maximum(g - 1, 0)

    def in_specs(where):
        return [pl.BlockSpec((1, pair, hk), lambda b, g, col=col: (b, where(p_in(g)), col)) for col in range(3)] + [
            pl.BlockSpec((1, pair, LANES), lambda b, g: (b, where(p_in(g)), 0)),
            pl.BlockSpec((1, 2, LANES, LANES), lambda b, g: (b, where(p_in(g)), 0, 0)),
        ]

    out_spec = lambda where: pl.BlockSpec((1, pair, hk), lambda b, g: (b, where(p_out(g)), 0))
    out_shape = jax.ShapeDtypeStruct((batch, n_tot, hk), F32)
    slot = [pltpu.VMEM((2, DN_HEADS, DN_CHUNK, 2 * DN_DV), F32),
            pltpu.VMEM((2, DN_HEADS, 2 * DN_CHUNK, 2 * DN_DK), BF16),
            pltpu.VMEM((2, DN_HEADS, DN_CHUNK + DN_DK, 2 * DN_CHUNK), BF16),
            pltpu.VMEM((2, DN_HEADS, SUBLANES, 2 * DN_DV), F32)]
    per_dir = (qkv, qkv, qkv, gcol, grow)
    return pl.pallas_call(
        _dn_scan_kernel,
        grid=(batch, n_pairs + 1),
        in_specs=in_specs(fwd) + in_specs(bwd),
        out_specs=[out_spec(fwd), out_spec(bwd)],
        out_shape=[out_shape, out_shape],
        scratch_shapes=[pltpu.VMEM((2 * DN_HEADS, DN_DK, DN_DV), F32)] + slot + slot,
        compiler_params=_cparams(("arbitrary", "arbitrary"), 48),
        name="dn_scan",
    )(*per_dir, *per_dir)


def _sublayers_kernel(*refs, alpha, gated_norm, split_residual, n_lat_tiles):
    if gated_norm:
        y_ref, y2_ref, z_ref, ng_ref = refs[:4]
    else:
        y_ref = refs[0]
    (gt1_ref, sc2_ref, sh2_ref, gt2_ref, wo_ref, l1g_ref, l1b_ref,
     wg_ref, wu_ref, wd_ref, l2g_ref, l2b_ref, o_ref, yb_ref) = refs[-14:]
    if split_residual:
        x_ref, c_ref = refs[-16:-14]
        residual = jnp.where(pl.program_id(1) < n_lat_tiles, x_ref[0], c_ref[0])
    else:
        residual = refs[-15][0]
    if gated_norm:
        for h0 in range(0, y_ref.shape[2], DN_DV):
            yh = y_ref[0, :, h0:h0 + DN_DV] + y2_ref[0, :, h0:h0 + DN_DV]
            ms = jnp.mean(yh * yh, axis=-1, keepdims=True)
            yh = yh * lax.rsqrt(ms + RMS_EPS) * ng_ref[...]
            yb_ref[:, h0:h0 + DN_DV] = (yh * _silu(z_ref[0, :, h0:h0 + DN_DV])).astype(BF16)
        mixed = yb_ref[...]
    else:
        mixed = y_ref[0]
    x1 = _layernorm(alpha * residual + gt1_ref[0] * _dot(mixed, wo_ref[...]), l1g_ref[...], l1b_ref[...])
    h = (x1 * (1.0 + sc2_ref[0]) + sh2_ref[0]).astype(BF16)
    gate = _dot(h, wg_ref[...])
    up = _dot(h, wu_ref[...])
    act = (_silu(gate) * up).astype(BF16)
    o_ref[0] = _layernorm(alpha * x1 + gt2_ref[0] * _dot(act, wd_ref[...]), l2g_ref[...], l2b_ref[...])


def _sublayers(ys, residual, mod, w_out16, ln1, ffn16, ln2, alpha, n_lat, n_out, gated_norm):
    batch, _, d = residual[0].shape
    hv = ys[0].shape[2]
    n_lat_tiles = n_lat // ROW_TILE
    split = len(residual) == 2
    kern = functools.partial(_sublayers_kernel, alpha=alpha, gated_norm=gated_norm, split_residual=split,
                             n_lat_tiles=n_lat_tiles)
    tok = lambda width: pl.BlockSpec((1, ROW_TILE, width), lambda b, t: (b, t, 0))
    res_specs = [tok(d)] if not split else [
        pl.BlockSpec((1, ROW_TILE, d), lambda b, t: (b, jnp.minimum(t, n_lat_tiles - 1), 0)),
        pl.BlockSpec((1, ROW_TILE, d), lambda b, t: (b, jnp.maximum(t - n_lat_tiles, 0), 0))]
    mod_spec = lambda idx: pl.BlockSpec((1, 1, d), lambda b, t: (jnp.where(t < n_lat_tiles, b, batch), 0, idx))
    resident = pl.BlockSpec(memory_space=pltpu.VMEM)
    vec = pl.BlockSpec((1, d), lambda b, t: (0, 0))
    return pl.pallas_call(
        kern,
        grid=(batch, n_out // ROW_TILE),
        in_specs=([tok(hv)] * 3 + [pl.BlockSpec((1, ys[3].shape[1]), lambda b, t: (0, 0))]
                  if gated_norm else [tok(hv)]) + res_specs
        + [mod_spec(2), mod_spec(4), mod_spec(3), mod_spec(5), resident, vec, vec, resident, resident, resident,
           vec, vec],
        out_specs=tok(d),
        out_shape=jax.ShapeDtypeStruct((batch, n_out, d), F32),
        scratch_shapes=[pltpu.VMEM((ROW_TILE, hv), BF16)],
        compiler_params=_cparams(("arbitrary", "arbitrary"), 56),
        name="sublayers",
    )(*ys, *residual, mod, mod, mod, mod, w_out16, *ln1, *ffn16, *ln2)


def _da_proj_kernel(x_ref, sc_l, sh_l, sc_c, sh_c, w_ref, cos_ref, sin_a_ref, sin_b_ref,
                    o_ref, xb_ref, *, n_lat, n_qk_tiles):
    j = pl.program_id(1)

    @pl.when(j == 0)
    def _():
        _modulate_into(xb_ref, x_ref[0, :n_lat, :], x_ref[0, n_lat:, :], sc_l, sh_l, sc_c, sh_c)

    blocks = _row_blocks(n_lat, x_ref.shape[1])

    def projector():
        w16 = w_ref[...].astype(BF16)
        return lambda r0, rows: _dot(xb_ref[r0:r0 + rows, :], w16)

    @pl.when(j < n_qk_tiles)
    def _():
        project = projector()
        scale = jnp.where(j < n_qk_tiles // 2, DA_DIM ** -0.5 * LOG2E, 1.0)
        quarter = DA_DIM // 4

        def rope(r0, rows, acc):
            cos, sin_a, sin_b = (t[r0:r0 + rows, :] for t in (cos_ref, sin_a_ref, sin_b_ref))
            for h0 in range(0, COL_TILE, LANES):
                xh = acc[:, h0:h0 + LANES]
                rot = pltpu.roll(xh, quarter, 1) * sin_a + pltpu.roll(xh, LANES - quarter, 1) * sin_b
                o_ref[0, r0:r0 + rows, h0:h0 + LANES] = ((xh * cos + rot) * scale).astype(BF16)

        _pipelined(blocks, project, rope)

    @pl.when(j >= n_qk_tiles)
    def _():
        project = projector()
        for r0, rows in blocks:
            o_ref[0, r0:r0 + rows, :] = project(r0, rows).astype(BF16)


def _da_project(xc, mod, w_in, cos_t, sin_a, sin_b, n_lat):
    batch, n_tot, d = xc.shape
    n_cols = w_in.shape[1]
    n_tiles = n_cols // COL_TILE
    n_qk_tiles = 2 * n_tiles // 3
    kern = functools.partial(_da_proj_kernel, n_lat=n_lat, n_qk_tiles=n_qk_tiles)
    table = pl.BlockSpec((n_tot, LANES), lambda b, j: (0, 0))
    return pl.pallas_call(
        kern,
        grid=(batch, n_tiles),
        in_specs=[pl.BlockSpec((1, n_tot, d), lambda b, j: (b, 0, 0))]
        + _mod_specs(batch, d, 1, 0)
        + [pl.BlockSpec((d, COL_TILE), lambda b, j: (0, j)), table, table, table],
        out_specs=pl.BlockSpec((1, n_tot, COL_TILE), lambda b, j: (b, 0, j)),
        out_shape=jax.ShapeDtypeStruct((batch, n_tot, n_cols), BF16),
        scratch_shapes=[pltpu.VMEM((n_tot, d), BF16)],
        compiler_params=_cparams(("arbitrary", "arbitrary"), 56),
        name="da_project",
    )(xc, mod, mod, mod, mod, w_in, cos_t, sin_a, sin_b)


def _da_attn_kernel(q_ref, k_ref, v_ref, lam_ref, g_ref, o_ref, *, lambda_init):
    lam = lam_ref[...]
    lam_val = (jnp.exp(jnp.sum(lam[0:1] * lam[1:2], axis=-1, keepdims=True))
               - jnp.exp(jnp.sum(lam[2:3] * lam[3:4], axis=-1, keepdims=True)) + lambda_init)
    k = k_ref[0]
    n_sub = q_ref.shape[1] // ATT_SUB_ROWS
    lane = lax.broadcasted_iota(jnp.int32, (ATT_SUB_ROWS, 2 * DA_DIM), 1)

    def scores(r):
        q = q_ref[0, r * ATT_SUB_ROWS:(r + 1) * ATT_SUB_ROWS, :]
        zero = jnp.zeros_like(q)
        return [_dot_nt(jnp.where((lane < DA_DIM) if comp == 0 else (lane >= DA_DIM), q, zero), k)
                for comp in range(2)]

    def weights(s):
        e0, e1 = (jnp.exp2(sc - jnp.max(sc, axis=-1, keepdims=True)) for sc in s)
        l0 = jnp.sum(e0, axis=-1, keepdims=True)
        l1 = jnp.sum(e1, axis=-1, keepdims=True)
        return (e0 - e1 * (lam_val * l0 / l1)).astype(BF16), 1.0 / l0

    def values(r, a16, inv_l0):
        o = _dot(a16, v_ref[0]) * inv_l0
        ms = jnp.mean(o * o, axis=-1, keepdims=True)
        o_ref[0, r * ATT_SUB_ROWS:(r + 1) * ATT_SUB_ROWS, :] = (
            o * lax.rsqrt(ms + RMS_EPS) * g_ref[...] * (1.0 - lambda_init)).astype(BF16)

    pending = [scores(r) for r in range(min(ATT_LOOKAHEAD, n_sub))]
    for r in range(n_sub):
        if r + ATT_LOOKAHEAD < n_sub:
            pending.append(scores(r + ATT_LOOKAHEAD))
        values(r, *weights(pending.pop(0)))


def _da_attention(qkv16, lam, subln_g, lambda_init, n_lat):
    batch, n_tot, _ = qkv16.shape
    hw = 2 * DA_DIM
    kern = functools.partial(_da_attn_kernel, lambda_init=lambda_init)
    tq = ATT_Q_TILE if n_lat % ATT_Q_TILE == 0 else ROW_TILE
    return pl.pallas_call(
        kern,
        grid=(batch, DA_HEADS, n_lat // tq),
        in_specs=[
            pl.BlockSpec((1, tq, hw), lambda b, h, i: (b, i, h)),
            pl.BlockSpec((1, n_tot, hw), lambda b, h, i: (b, 0, DA_HEADS + h)),
            pl.BlockSpec((1, n_tot, hw), lambda b, h, i: (b, 0, 2 * DA_HEADS + h)),
            pl.BlockSpec(lam.shape, lambda b, h, i: (0, 0)),
            pl.BlockSpec((1, hw), lambda b, h, i: (0, 0)),
        ],
        out_specs=pl.BlockSpec((1, tq, hw), lambda b, h, i: (b, i, h)),
        out_shape=jax.ShapeDtypeStruct((batch, n_lat, DA_HEADS * hw), BF16),
        compiler_params=_cparams(("arbitrary", "arbitrary", "arbitrary"), 48),
        name="da_attention",
    )(qkv16, qkv16, qkv16, lam, subln_g)


def _rope_tables(n_lat, n_tot):
    quarter = DA_DIM // 4
    inv_freq = ROPE_BASE ** (-jnp.arange(quarter, dtype=F32) / quarter)
    rows = n_lat // GRID_W
    row = jnp.repeat(jnp.arange(rows, dtype=F32), GRID_W)
    col = jnp.tile(jnp.arange(GRID_W, dtype=F32), rows)
    ang_r = row[:, None] * inv_freq
    ang_c = col[:, None] * inv_freq
    ang = jnp.concatenate([ang_r, ang_r, ang_c, ang_c], axis=-1)
    reps = LANES // DA_DIM
    cos = jnp.tile(jnp.cos(ang), (1, reps))
    sin = jnp.tile(jnp.sin(ang), (1, reps))
    upper = (jnp.arange(LANES) % (2 * quarter)) >= quarter
    sin_a = jnp.where(upper, sin, 0.0)
    sin_b = jnp.where(upper, 0.0, -sin)
    n_ctx = n_tot - n_lat
    ext = lambda t, fill: jnp.concatenate([t, jnp.full((n_ctx, LANES), fill, F32)], axis=0)
    return ext(cos, 1.0), ext(sin_a, 0.0), ext(sin_b, 0.0)


def kernel(x, c, ctx, c_ctx, ada_w, ada_b, ln1_g, ln1_b, ln2_g, ln2_b, ffn_w_gate, ffn_w_up, ffn_w_down,
           dn_w_in, dn_conv, dn_a_log, dn_dt_bias, dn_norm_g, dn_w_out, da_w_in, da_lambda, da_subln_g,
           da_w_out):
    batch, n_lat, d = x.shape
    n_ctx = ctx.shape[1]
    n_tot = n_lat + n_ctx
    depth = ada_w.shape[0]
    assert depth == 2, "layer 0 is gated DeltaNet, layer 1 (the last) differential attention"
    assert batch < MOD_ROWS and n_lat % ROW_TILE == 0 and n_ctx % ROW_TILE == 0
    alpha = (2.0 * depth) ** 0.25
    hk = DN_HEADS * DN_DK

    cs = jnp.concatenate([c, c_ctx[None, :], jnp.zeros((MOD_ROWS - batch - 1, d), F32)], axis=0)
    mods = _ada_modulation(cs, ada_w, ada_b).reshape(depth, MOD_ROWS, 1, 6 * d)

    mod = mods[0]
    gate_cols = dn_w_in[0][:, 4 * hk:]
    w_gates = jnp.concatenate([gate_cols, jnp.zeros((d, LANES - gate_cols.shape[1]), F32)], axis=1)
    lane_pad = lambda v: jnp.concatenate([v.reshape(1, -1), jnp.zeros((1, LANES - v.size), F32)], axis=1)
    qkv, z, gcol, grow = _dn_project(x, ctx, mod, dn_w_in[0], w_gates, dn_conv[0],
                                     lane_pad(dn_a_log[0]), lane_pad(dn_dt_bias[0]))
    o_fwd, o_bwd = _dn_scan(qkv, gcol, grow, n_lat)
    vec = lambda v: v.reshape(1, -1)
    ffn16 = lambda i: tuple(w[i].astype(BF16) for w in (ffn_w_gate, ffn_w_up, ffn_w_down))
    xc = _sublayers((o_fwd, o_bwd, z, vec(dn_norm_g[0])), (x, ctx), mod, dn_w_out[0].astype(BF16),
                    (vec(ln1_g[0]), vec(ln1_b[0])), ffn16(0), (vec(ln2_g[0]), vec(ln2_b[0])),
                    alpha, n_lat, n_tot, True)

    mod = mods[1]
    lambda_init = 0.8 - 0.6 * math.exp(-0.3 * 1)
    cos_t, sin_a, sin_b = _rope_tables(n_lat, n_tot)
    qkv16 = _da_project(xc, mod, da_w_in[0], cos_t, sin_a, sin_b, n_lat)
    y = _da_attention(qkv16, da_lambda[0], da_subln_g[0].reshape(1, -1), lambda_init, n_lat)
    return _sublayers((y,), (xc,), mod, da_w_out[0].astype(BF16), (vec(ln1_g[1]), vec(ln1_b[1])), ffn16(1),
                      (vec(ln2_g[1]), vec(ln2_b[1])), alpha, n_lat, n_lat, False)
```

```python
import functools
import math

import jax
import jax.numpy as jnp
from jax import lax
from jax.experimental import pallas as pl
from jax.experimental.pallas import tpu as pltpu

F32 = jnp.float32
BF16 = jnp.bfloat16
HIGHEST = lax.Precision.HIGHEST

DN_HEADS = 8
DN_DK = 128
DN_DV = 128
DN_CONV = 5
DN_CHUNK = 64
DA_HEADS = 8
DA_DIM = 64
GRID_W = 64
ROPE_BASE = 10000.0
LN_EPS = 1e-5
RMS_EPS = 1e-6

LANES = 128
SUBLANES = 8
MOD_ROWS = 16
ROW_TILE = 256
PROJ_ROW_BLOCK = 512
COL_TILE = 256
ADA_COL_TILE = 1536
ATT_Q_TILE = 2048
ATT_SUB_ROWS = 256
ATT_LOOKAHEAD = 2
LOG2E = math.log2(math.e)


def _cparams(dims, vmem_mb):
    return pltpu.CompilerParams(dimension_semantics=dims, vmem_limit_bytes=vmem_mb * 1024 * 1024)


def _silu(x):
    return x * jax.nn.sigmoid(x)


def _dot(a, b, precision=None):
    return jnp.dot(a, b, preferred_element_type=F32, precision=precision)


def _dot_nt(a, b):
    return lax.dot_general(a, b, (((1,), (1,)), ((), ())), preferred_element_type=F32)


def _layernorm(x, g, b):
    mu = jnp.mean(x, axis=-1, keepdims=True)
    xc = x - mu
    var = jnp.mean(xc * xc, axis=-1, keepdims=True)
    return xc * lax.rsqrt(var + LN_EPS) * g + b


def _ada_kernel(cs_ref, w_ref, b_ref, o_ref):
    s = _silu(cs_ref[...]).astype(BF16)
    o_ref[0] = _dot(s, w_ref[0].astype(BF16)) + b_ref[0]


def _ada_modulation(cs, ada_w, ada_b):
    depth, d, n = ada_w.shape
    tn = ADA_COL_TILE if n % ADA_COL_TILE == 0 else n
    return pl.pallas_call(
        _ada_kernel,
        grid=(depth, n // tn),
        in_specs=[
            pl.BlockSpec((MOD_ROWS, d), lambda i, j: (0, 0)),
            pl.BlockSpec((1, d, tn), lambda i, j: (i, 0, j)),
            pl.BlockSpec((1, 1, tn), lambda i, j: (i, 0, j)),
        ],
        out_specs=pl.BlockSpec((1, MOD_ROWS, tn), lambda i, j: (i, 0, j)),
        out_shape=jax.ShapeDtypeStruct((depth, MOD_ROWS, n), F32),
        compiler_params=_cparams(("arbitrary", "arbitrary"), 40),
        name="ada_modulation",
    )(cs, ada_w, ada_b.reshape(depth, 1, n))


def _modulate_into(xb_ref, lat, ctx, sc_l, sh_l, sc_c, sh_c):
    n_lat = lat.shape[0]
    xb_ref[:n_lat, :] = (lat * (1.0 + sc_l[0]) + sh_l[0]).astype(BF16)
    xb_ref[n_lat:, :] = (ctx * (1.0 + sc_c[0]) + sh_c[0]).astype(BF16)


def _row_blocks(n_lat, n_tot):
    blocks = []
    for seg0, seg in ((0, n_lat), (n_lat, n_tot - n_lat)):
        size = PROJ_ROW_BLOCK if seg % PROJ_ROW_BLOCK == 0 else seg
        blocks += [(seg0 + r, size) for r in range(0, seg, size)]
    return blocks


def _pipelined(blocks, produce, consume):
    pending = None
    for blk in blocks:
        cur = produce(*blk)
        if pending is not None:
            consume(*pending)
        pending = blk + (cur,)
    consume(*pending)


def _mod_specs(batch, d, sc_idx, sh_idx):
    return [
        pl.BlockSpec((1, 1, d), lambda b, j: (b, 0, sc_idx)),
        pl.BlockSpec((1, 1, d), lambda b, j: (b, 0, sh_idx)),
        pl.BlockSpec((1, 1, d), lambda b, j: (batch, 0, sc_idx)),
        pl.BlockSpec((1, 1, d), lambda b, j: (batch, 0, sh_idx)),
    ]


def _dn_proj_kernel(x_ref, c_ref, sc_l, sh_l, sc_c, sh_c, w_ref, wg_ref, conv_ref, alog_ref, dt_ref,
                    qkv_ref, z_ref, gcol_ref, grow_ref, xb_ref, pad_ref,
                    *, n_lat, n_tot, n_qkv_tiles, n_z_tiles):
    j = pl.program_id(1)
    n_ctx = n_tot - n_lat
    halo = SUBLANES
    pad = (DN_CONV - 1) // 2
    lat0 = halo
    ctx0 = 2 * halo + n_lat

    @pl.when(j == 0)
    def _():
        _modulate_into(xb_ref, x_ref[0], c_ref[0], sc_l, sh_l, sc_c, sh_c)

    blocks = _row_blocks(n_lat, n_tot)

    def projector():
        w16 = w_ref[...].astype(BF16)
        return lambda r0, rows: _dot(xb_ref[r0:r0 + rows, :], w16)

    @pl.when(j < n_qkv_tiles)
    def _():
        project = projector()
        zeros = jnp.zeros((halo, COL_TILE), F32)
        pad_ref[0:halo, :] = zeros
        pad_ref[lat0 + n_lat:ctx0, :] = zeros
        pad_ref[ctx0 + n_ctx:ctx0 + n_ctx + halo, :] = zeros
        cw = conv_ref[...]
        n_qk_tiles = 2 * n_qkv_tiles // 3
        is_q = j < n_qk_tiles // 2
        is_qk = j < n_qk_tiles

        def padded(r0):
            return r0 + (lat0 if r0 < n_lat else ctx0 - n_lat)

        def stage(r0, rows):
            pad_ref[padded(r0):padded(r0) + rows, :] = project(r0, rows)

        def conv_norm(r0, rows, _):
            base = padded(r0)
            ext = pad_ref[base - halo:base + rows + halo, :]
            y = None
            for t in range(DN_CONV):
                shifted = ext if t == pad else pltpu.roll(ext, (pad - t) % ext.shape[0], 0)
                term = shifted[halo:halo + rows, :] * cw[t:t + 1, :]
                y = term if y is None else y + term
            y = _silu(y)
            for h0 in range(0, COL_TILE, DN_DK):
                yh = y[:, h0:h0 + DN_DK]
                ss = jnp.sum(yh * yh, axis=-1, keepdims=True)
                fac = lax.rsqrt(ss + RMS_EPS)
                fac = jnp.where(is_q, fac * (DN_DK ** -0.5), fac)
                fac = jnp.where(is_qk, fac, jnp.ones_like(fac))
                qkv_ref[0, r0:r0 + rows, h0:h0 + DN_DK] = yh * fac

        _pipelined(blocks, stage, conv_norm)

    @pl.when(jnp.logical_and(j >= n_qkv_tiles, j < n_qkv_tiles + n_z_tiles))
    def _():
        project = projector()
        for r0, rows in blocks:
            z_ref[0, r0:r0 + rows, :] = project(r0, rows)

    @pl.when(j == n_qkv_tiles + n_z_tiles)
    def _():
        t = _dot(xb_ref[...], wg_ref[...].astype(BF16))
        lane = lax.broadcasted_iota(jnp.int32, t.shape, 1)
        x = t + dt_ref[...]
        softplus = jnp.maximum(x, 0.0) + jnp.log1p(jnp.exp(-jnp.abs(x)))
        decay = -jnp.exp(alog_ref[...]) * softplus
        gates = jnp.where(lane < 2 * DN_HEADS, decay, jax.nn.sigmoid(t))
        ri = lax.broadcasted_iota(jnp.int32, (DN_CHUNK, DN_CHUNK), 0)
        ci = lax.broadcasted_iota(jnp.int32, (DN_CHUNK, DN_CHUNK), 1)
        tri = jnp.concatenate([(ri >= ci).astype(F32), (ri <= ci).astype(F32)], axis=0)
        lane_c = lax.broadcasted_iota(jnp.int32, (DN_CHUNK, LANES), 1)
        for c in range(n_tot // DN_CHUNK):
            gc = gates[c * DN_CHUNK:(c + 1) * DN_CHUNK]
            cum = _dot(tri, gc, HIGHEST)
            col = jnp.where(lane_c < DN_HEADS, cum[:DN_CHUNK],
                            jnp.where(lane_c < 2 * DN_HEADS, cum[DN_CHUNK:], gc))
            gcol_ref[0, c * DN_CHUNK:(c + 1) * DN_CHUNK, :] = col
            grow_ref[0, c] = jnp.concatenate([col, col], axis=0).T


def _dn_project(x, ctx, mod, w_in, w_gates, conv_w, alog_row, dt_row):
    batch, n_lat, d = x.shape
    n_tot = n_lat + ctx.shape[1]
    hk = DN_HEADS * DN_DK
    n_qkv_tiles = 3 * hk // COL_TILE
    n_z_tiles = hk // COL_TILE
    n_tiles = n_qkv_tiles + n_z_tiles + 1
    nc = n_tot // DN_CHUNK
    kern = functools.partial(_dn_proj_kernel, n_lat=n_lat, n_tot=n_tot,
                             n_qkv_tiles=n_qkv_tiles, n_z_tiles=n_z_tiles)
    return pl.pallas_call(
        kern,
        grid=(batch, n_tiles),
        in_specs=[pl.BlockSpec((1, n_lat, d), lambda b, j: (b, 0, 0)),
                  pl.BlockSpec((1, n_tot - n_lat, d), lambda b, j: (b, 0, 0))]
        + _mod_specs(batch, d, 1, 0)
        + [
            pl.BlockSpec((d, COL_TILE), lambda b, j: (0, jnp.minimum(j, n_tiles - 2))),
            pl.BlockSpec((d, LANES), lambda b, j: (0, 0)),
            pl.BlockSpec((DN_CONV, COL_TILE), lambda b, j: (0, jnp.minimum(j, n_qkv_tiles - 1))),
            pl.BlockSpec((1, LANES), lambda b, j: (0, 0)),
            pl.BlockSpec((1, LANES), lambda b, j: (0, 0)),
        ],
        out_specs=[
            pl.BlockSpec((1, n_tot, COL_TILE), lambda b, j: (b, 0, jnp.minimum(j, n_qkv_tiles - 1))),
            pl.BlockSpec((1, n_tot, COL_TILE),
                         lambda b, j: (b, 0, jnp.clip(j - n_qkv_tiles, 0, n_z_tiles - 1))),
            pl.BlockSpec((1, n_tot, LANES), lambda b, j: (b, 0, 0)),
            pl.BlockSpec((1, nc, LANES, LANES), lambda b, j: (b, 0, 0, 0)),
        ],
        out_shape=[
            jax.ShapeDtypeStruct((batch, n_tot, 3 * hk), F32),
            jax.ShapeDtypeStruct((batch, n_tot, hk), F32),
            jax.ShapeDtypeStruct((batch, n_tot, LANES), F32),
            jax.ShapeDtypeStruct((batch, nc, LANES, LANES), F32),
        ],
        scratch_shapes=[
            pltpu.VMEM((n_tot, d), BF16),
            pltpu.VMEM((n_tot + 3 * SUBLANES, COL_TILE), F32),
        ],
        compiler_params=_cparams(("arbitrary", "arbitrary"), 56),
        name="dn_project",
    )(x, ctx, mod, mod, mod, mod, w_in, w_gates, conv_w, alog_row, dt_row)


INV_BASE = 16
SCAN_BATCH = 2
SCAN_UNITS = SCAN_BATCH * DN_HEADS


def _side_by_side(fwd, bwd):
    return jnp.concatenate([fwd, bwd], axis=1)


def _block_diag16(top, bottom):
    top, bottom = top.astype(BF16), bottom.astype(BF16)
    return jnp.concatenate([_side_by_side(top, jnp.zeros((top.shape[0], bottom.shape[1]), BF16)),
                            _side_by_side(jnp.zeros((bottom.shape[0], top.shape[1]), BF16), bottom)], axis=0)


def _dn_local_stages(ins, slot, sidx, half):
    u_ref, wq_ref, m_ref, e_ref = slot
    c = DN_CHUNK
    st = [dict() for _ in range(SCAN_UNITS)]

    def each(fn):
        def run():
            for n, t in enumerate(st):
                fn(t, n)
        return run

    def grids():
        ri = lax.broadcasted_iota(jnp.int32, (c, 2 * c), 0)
        lane = lax.broadcasted_iota(jnp.int32, (c, 2 * c), 1)
        return ri, lane & (c - 1), lane < c

    def pair_dot(x, y):
        _, _, fwd_half = grids()
        rhs = jnp.concatenate([jnp.where(fwd_half, y, 0.0).astype(BF16),
                               jnp.where(fwd_half, 0.0, y).astype(BF16)], axis=0)
        return _dot(x.astype(BF16), rhs)

    def load(t, n):
        bb, hh = divmod(n, DN_HEADS)
        ri, cj, fwd_half = grids()
        lane = lax.broadcasted_iota(jnp.int32, (c, LANES), 1)
        cols = slice(hh * DN_DK, (hh + 1) * DN_DK)
        per_dir = []
        for d in range(2):
            q_ref, k_ref, v_ref, gcol_ref, grow_ref = ins[d]
            part = half if d == 0 else 1 - half
            rows = slice(part * c, (part + 1) * c)
            gates = gcol_ref[bb, rows, :]
            g_lane = d * DN_HEADS + hh
            gcum = jnp.sum(jnp.where(lane == g_lane, gates, 0.0), axis=-1, keepdims=True)
            beta = jnp.sum(jnp.where(lane == 2 * DN_HEADS + g_lane, gates, 0.0), axis=-1, keepdims=True)
            grow = grow_ref[bb, part, g_lane:g_lane + 1, :]
            g_last = grow[:, c - 1:c] if d == 0 else grow[:, 0:1]
            per_dir.append((q_ref[bb, rows, cols], k_ref[bb, rows, cols], v_ref[bb, rows, cols],
                            gcum, beta, grow, g_last))
        (qf, kf, vf, gcf, bf, grf, glf), (qb, kb, vb, gcb, bb, grb, glb) = per_dir
        incl = jnp.logical_or(ri == cj, (ri > cj) == fwd_half)
        diff = jnp.where(fwd_half, gcf, gcb) - jnp.where(fwd_half[0:1], grf, grb)
        t["gamma"] = jnp.where(incl, jnp.exp(jnp.where(incl, diff, 0.0)), 0.0)
        kbf, kbb = kf * bf, kb * bb
        dec_f, dec_b = jnp.exp(gcf), jnp.exp(gcb)
        t["lhs16"] = _side_by_side(jnp.concatenate([kbf, qf], axis=0).astype(BF16),
                                   jnp.concatenate([kbb, qb], axis=0).astype(BF16))
        t["keys16"] = _block_diag16(kf, kb)
        t["rhs16"] = _block_diag16(_side_by_side(vf * bf, kbf * dec_f), _side_by_side(vb * bb, kbb * dec_b))
        wq_ref[sidx, n, c:, :DN_DK] = (qf * dec_f).astype(BF16)
        wq_ref[sidx, n, c:, DN_DK:] = (qb * dec_b).astype(BF16)
        k_dec = jnp.concatenate([kf * jnp.exp(glf - gcf), kb * jnp.exp(glb - gcb)], axis=0)
        m_ref[sidx, n, c:, :] = k_dec.T.astype(BF16)
        e_ref[sidx, n] = _side_by_side(jnp.broadcast_to(jnp.exp(glf), (SUBLANES, DN_DV)),
                                        jnp.broadcast_to(jnp.exp(glb), (SUBLANES, DN_DV)))

    def gram(t, n):
        t["kq"] = _dot_nt(t.pop("lhs16"), t.pop("keys16"))

    def split(t, n):
        ri, cj, fwd_half = grids()
        strict = jnp.logical_and(ri != cj, (ri > cj) == fwd_half)
        kq = t.pop("kq")
        gamma = t.pop("gamma")
        a = jnp.where(strict, kq[:c] * gamma, 0.0)
        m_ref[sidx, n, :c, :] = (kq[c:] * gamma).astype(BF16)
        a_diag = jnp.where((ri // INV_BASE) == (cj // INV_BASE), a, 0.0)
        t["a"] = a
        t["inv"] = (ri == cj).astype(F32) - a_diag
        t["pw"] = pair_dot(a_diag, a_diag)

    def series(t, n):
        both = pair_dot(jnp.concatenate([t["pw"], t["inv"]], axis=0), t["pw"])
        t["pw"] = both[:c]
        t["inv"] = t["inv"] + both[c:]

    def series_last(t, n):
        t["inv"] = t["inv"] + pair_dot(t["inv"], t.pop("pw"))

    def merge_a(blk):
        def fn(t, n):
            ri, cj, _ = grids()
            off = jnp.logical_and((ri // (2 * blk)) == (cj // (2 * blk)), (ri // blk) != (cj // blk))
            t["y"] = pair_dot(jnp.where(off, t["a"], 0.0), t["inv"])
        return fn

    def merge_b(t, n):
        t["inv"] = t["inv"] - pair_dot(t["inv"], t.pop("y"))

    def solve(t, n):
        t["uw"] = _dot(t.pop("inv").astype(BF16), t.pop("rhs16"))
        t.pop("a")

    def store(t, n):
        uw = t.pop("uw")
        for d in range(2):
            base = d * (DN_DV + DN_DK)
            u_ref[sidx, n, :, d * DN_DV:(d + 1) * DN_DV] = uw[:, base:base + DN_DV]
            wq_ref[sidx, n, :c, d * DN_DK:(d + 1) * DN_DK] = uw[:, base + DN_DV:base + DN_DV + DN_DK].astype(BF16)

    stages = [load, gram, split] + [series] * (int(math.log2(INV_BASE)) - 2) + [series_last]
    blk = INV_BASE
    while blk < c:
        stages += [merge_a(blk), merge_b]
        blk *= 2
    stages += [solve, store]
    return [each(fn) for fn in stages]


def _dn_serial_stages(outs, s_ref, slot, sidx, half):
    u_ref, wq_ref, m_ref, e_ref = slot
    c = DN_CHUNK
    st = [dict() for _ in range(SCAN_UNITS)]

    def each(fn):
        def run():
            for n, t in enumerate(st):
                fn(t, n)
        return run

    def state_dot(t, n):
        t["s"] = (s_ref[n], s_ref[SCAN_UNITS + n])
        t["ws"] = _dot(wq_ref[sidx, n], _block_diag16(*t["s"]))

    def value_dot(t, n):
        v_new = u_ref[sidx, n] - t["ws"][:c]
        t["mv"] = _dot(m_ref[sidx, n], _block_diag16(v_new[:, :DN_DV], v_new[:, DN_DV:]))

    def update(t, n):
        bb, hh = divmod(n, DN_HEADS)
        mv, ws, s = t.pop("mv"), t.pop("ws"), t.pop("s")
        gain = e_ref[sidx, n][0:1, :]
        for d in range(2):
            lanes = slice(d * DN_DV, (d + 1) * DN_DV)
            s_ref[d * SCAN_UNITS + n] = s[d] * gain[:, lanes] + mv[c:, lanes]
            part = half if d == 0 else 1 - half
            outs[d][bb, part * c:(part + 1) * c, hh * DN_DV:(hh + 1) * DN_DV] = ws[c:, lanes] + mv[:c, lanes]

    return [each(fn) for fn in (state_dot, value_dot, update)]


def _dn_scan_kernel(*refs):
    ins = (refs[0:5], refs[5:10])
    outs = refs[10:12]
    s_ref = refs[12]
    slot_sets = (refs[13:17], refs[17:21])
    g = pl.program_id(1)

    @pl.when(g == 0)
    def _():
        for ref in (s_ref,) + tuple(slot_sets[0]) + tuple(slot_sets[1]):
            ref[...] = jnp.zeros(ref.shape, ref.dtype)

    def step(written, read):
        loc = [_dn_local_stages(ins, written, half, half) for half in range(2)]
        ser = [stage for half in range(2) for stage in _dn_serial_stages(outs, s_ref, read, half, half)]
        gap = len(loc[0]) // len(ser)
        for n, stages in enumerate(zip(*loc)):
            if n % gap == 0 and n // gap < len(ser):
                ser[n // gap]()
            for stage in stages:
                stage()

    for parity in range(2):
        pl.when(g % 2 == parity)(functools.partial(step, slot_sets[parity], slot_sets[1 - parity]))


def _dn_scan(qkv, gcol, grow, n_lat):
    batch, n_tot, _ = qkv.shape
    hk = DN_HEADS * DN_DK
    pair = 2 * DN_CHUNK
    n_pairs = n_tot // pair
    lat_pairs = n_lat // pair
    ctx_pairs = n_pairs - lat_pairs
    assert n_lat % pair == 0 and n_tot % pair == 0 and batch % SCAN_BATCH == 0

    def fwd(p):
        return jnp.where(p < ctx_pairs, p + lat_pairs, p - ctx_pairs)

    def bwd(p):
        return n_pairs - 1 - p

    def p_in(g):
        return jnp.minimum(g, n_pairs - 1)

    def p_out(g):
        return jnp.maximum(g - 1, 0)

    def in_specs(where):
        return [pl.BlockSpec((SCAN_BATCH, pair, hk), lambda b, g, col=col: (b, where(p_in(g)), col))
                for col in range(3)] + [
            pl.BlockSpec((SCAN_BATCH, pair, LANES), lambda b, g: (b, where(p_in(g)), 0)),
            pl.BlockSpec((SCAN_BATCH, 2, LANES, LANES), lambda b, g: (b, where(p_in(g)), 0, 0)),
        ]

    out_spec = lambda where: pl.BlockSpec((SCAN_BATCH, pair, hk), lambda b, g: (b, where(p_out(g)), 0))
    out_shape = jax.ShapeDtypeStruct((batch, n_tot, hk), F32)
    slot = [pltpu.VMEM((2, SCAN_UNITS, DN_CHUNK, 2 * DN_DV), F32),
            pltpu.VMEM((2, SCAN_UNITS, 2 * DN_CHUNK, 2 * DN_DK), BF16),
            pltpu.VMEM((2, SCAN_UNITS, DN_CHUNK + DN_DK, 2 * DN_CHUNK), BF16),
            pltpu.VMEM((2, SCAN_UNITS, SUBLANES, 2 * DN_DV), F32)]
    per_dir = (qkv, qkv, qkv, gcol, grow)
    return pl.pallas_call(
        _dn_scan_kernel,
        grid=(batch // SCAN_BATCH, n_pairs + 1),
        in_specs=in_specs(fwd) + in_specs(bwd),
        out_specs=[out_spec(fwd), out_spec(bwd)],
        out_shape=[out_shape, out_shape],
        scratch_shapes=[pltpu.VMEM((2 * SCAN_UNITS, DN_DK, DN_DV), F32)] + slot + slot,
        compiler_params=_cparams(("arbitrary", "arbitrary"), 48),
        name="dn_scan",
    )(*per_dir, *per_dir)


def _sublayers_kernel(*refs, alpha, gated_norm, split_residual, n_lat_tiles):
    if gated_norm:
        y_ref, y2_ref, z_ref, ng_ref = refs[:4]
    else:
        y_ref = refs[0]
    (gt1_ref, sc2_ref, sh2_ref, gt2_ref, wo_ref, l1g_ref, l1b_ref,
     wg_ref, wu_ref, wd_ref, l2g_ref, l2b_ref, o_ref, yb_ref) = refs[-14:]
    if split_residual:
        x_ref, c_ref = refs[-16:-14]
        residual = jnp.where(pl.program_id(1) < n_lat_tiles, x_ref[0], c_ref[0])
    else:
        residual = refs[-15][0]
    if gated_norm:
        for h0 in range(0, y_ref.shape[2], DN_DV):
            yh = y_ref[0, :, h0:h0 + DN_DV] + y2_ref[0, :, h0:h0 + DN_DV]
            ms = jnp.mean(yh * yh, axis=-1, keepdims=True)
            yh = yh * lax.rsqrt(ms + RMS_EPS) * ng_ref[...]
            yb_ref[:, h0:h0 + DN_DV] = (yh * _silu(z_ref[0, :, h0:h0 + DN_DV])).astype(BF16)
        mixed = yb_ref[...]
    else:
        mixed = y_ref[0]
    x1 = _layernorm(alpha * residual + gt1_ref[0] * _dot(mixed, wo_ref[...]), l1g_ref[...], l1b_ref[...])
    h = (x1 * (1.0 + sc2_ref[0]) + sh2_ref[0]).astype(BF16)
    gate = _dot(h, wg_ref[...])
    up = _dot(h, wu_ref[...])
    act = (_silu(gate) * up).astype(BF16)
    o_ref[0] = _layernorm(alpha * x1 + gt2_ref[0] * _dot(act, wd_ref[...]), l2g_ref[...], l2b_ref[...])


def _sublayers(ys, residual, mod, w_out16, ln1, ffn16, ln2, alpha, n_lat, n_out, gated_norm):
    batch, _, d = residual[0].shape
    hv = ys[0].shape[2]
    n_lat_tiles = n_lat // ROW_TILE
    split = len(residual) == 2
    kern = functools.partial(_sublayers_kernel, alpha=alpha, gated_norm=gated_norm, split_residual=split,
                             n_lat_tiles=n_lat_tiles)
    tok = lambda width: pl.BlockSpec((1, ROW_TILE, width), lambda b, t: (b, t, 0))
    res_specs = [tok(d)] if not split else [
        pl.BlockSpec((1, ROW_TILE, d), lambda b, t: (b, jnp.minimum(t, n_lat_tiles - 1), 0)),
        pl.BlockSpec((1, ROW_TILE, d), lambda b, t: (b, jnp.maximum(t - n_lat_tiles, 0), 0))]
    mod_spec = lambda idx: pl.BlockSpec((1, 1, d), lambda b, t: (jnp.where(t < n_lat_tiles, b, batch), 0, idx))
    resident = pl.BlockSpec(memory_space=pltpu.VMEM)
    vec = pl.BlockSpec((1, d), lambda b, t: (0, 0))
    return pl.pallas_call(
        kern,
        grid=(batch, n_out // ROW_TILE),
        in_specs=([tok(hv)] * 3 + [pl.BlockSpec((1, ys[3].shape[1]), lambda b, t: (0, 0))]
                  if gated_norm else [tok(hv)]) + res_specs
        + [mod_spec(2), mod_spec(4), mod_spec(3), mod_spec(5), resident, vec, vec, resident, resident, resident,
           vec, vec],
        out_specs=tok(d),
        out_shape=jax.ShapeDtypeStruct((batch, n_out, d), F32),
        scratch_shapes=[pltpu.VMEM((ROW_TILE, hv), BF16)],
        compiler_params=_cparams(("arbitrary", "arbitrary"), 56),
        name="sublayers",
    )(*ys, *residual, mod, mod, mod, mod, w_out16, *ln1, *ffn16, *ln2)


def _da_proj_kernel(x_ref, sc_l, sh_l, sc_c, sh_c, w_ref, cos_ref, sin_a_ref, sin_b_ref,
                    o_ref, xb_ref, *, n_lat, n_qk_tiles):
    j = pl.program_id(1)

    @pl.when(j == 0)
    def _():
        _modulate_into(xb_ref, x_ref[0, :n_lat, :], x_ref[0, n_lat:, :], sc_l, sh_l, sc_c, sh_c)

    blocks = _row_blocks(n_lat, x_ref.shape[1])

    def projector():
        w16 = w_ref[...].astype(BF16)
        return lambda r0, rows: _dot(xb_ref[r0:r0 + rows, :], w16)

    @pl.when(j < n_qk_tiles)
    def _():
        project = projector()
        scale = jnp.where(j < n_qk_tiles // 2, DA_DIM ** -0.5 * LOG2E, 1.0)
        quarter = DA_DIM // 4

        def rope(r0, rows, acc):
            cos, sin_a, sin_b = (t[r0:r0 + rows, :] for t in (cos_ref, sin_a_ref, sin_b_ref))
            for h0 in range(0, COL_TILE, LANES):
                xh = acc[:, h0:h0 + LANES]
                rot = pltpu.roll(xh, quarter, 1) * sin_a + pltpu.roll(xh, LANES - quarter, 1) * sin_b
                o_ref[0, r0:r0 + rows, h0:h0 + LANES] = ((xh * cos + rot) * scale).astype(BF16)

        _pipelined(blocks, project, rope)

    @pl.when(j >= n_qk_tiles)
    def _():
        project = projector()
        for r0, rows in blocks:
            o_ref[0, r0:r0 + rows, :] = project(r0, rows).astype(BF16)


def _da_project(xc, mod, w_in, cos_t, sin_a, sin_b, n_lat):
    batch, n_tot, d = xc.shape
    n_cols = w_in.shape[1]
    n_tiles = n_cols // COL_TILE
    n_qk_tiles = 2 * n_tiles // 3
    kern = functools.partial(_da_proj_kernel, n_lat=n_lat, n_qk_tiles=n_qk_tiles)
    table = pl.BlockSpec((n_tot, LANES), lambda b, j: (0, 0))
    return pl.pallas_call(
        kern,
        grid=(batch, n_tiles),
        in_specs=[pl.BlockSpec((1, n_tot, d), lambda b, j: (b, 0, 0))]
        + _mod_specs(batch, d, 1, 0)
        + [pl.BlockSpec((d, COL_TILE), lambda b, j: (0, j)), table, table, table],
        out_specs=pl.BlockSpec((1, n_tot, COL_TILE), lambda b, j: (b, 0, j)),
        out_shape=jax.ShapeDtypeStruct((batch, n_tot, n_cols), BF16),
        scratch_shapes=[pltpu.VMEM((n_tot, d), BF16)],
        compiler_params=_cparams(("arbitrary", "arbitrary"), 56),
        name="da_project",
    )(xc, mod, mod, mod, mod, w_in, cos_t, sin_a, sin_b)


def _da_attn_kernel(q_ref, k_ref, v_ref, lam_ref, g_ref, o_ref, *, lambda_init):
    lam = lam_ref[...]
    lam_val = (jnp.exp(jnp.sum(lam[0:1] * lam[1:2], axis=-1, keepdims=True))
               - jnp.exp(jnp.sum(lam[2:3] * lam[3:4], axis=-1, keepdims=True)) + lambda_init)
    k = k_ref[0]
    n_sub = q_ref.shape[1] // ATT_SUB_ROWS
    lane = lax.broadcasted_iota(jnp.int32, (ATT_SUB_ROWS, 2 * DA_DIM), 1)

    def scores(r):
        q = q_ref[0, r * ATT_SUB_ROWS:(r + 1) * ATT_SUB_ROWS, :]
        zero = jnp.zeros_like(q)
        return [_dot_nt(jnp.where((lane < DA_DIM) if comp == 0 else (lane >= DA_DIM), q, zero), k)
                for comp in range(2)]

    def weights(s):
        e0, e1 = (jnp.exp2(sc - jnp.max(sc, axis=-1, keepdims=True)) for sc in s)
        l0 = jnp.sum(e0, axis=-1, keepdims=True)
        l1 = jnp.sum(e1, axis=-1, keepdims=True)
        return (e0 - e1 * (lam_val * l0 / l1)).astype(BF16), 1.0 / l0

    def values(r, a16, inv_l0):
        o = _dot(a16, v_ref[0]) * inv_l0
        ms = jnp.mean(o * o, axis=-1, keepdims=True)
        o_ref[0, r * ATT_SUB_ROWS:(r + 1) * ATT_SUB_ROWS, :] = (
            o * lax.rsqrt(ms + RMS_EPS) * g_ref[...] * (1.0 - lambda_init)).astype(BF16)

    pending = [scores(r) for r in range(min(ATT_LOOKAHEAD, n_sub))]
    for r in range(n_sub):
        if r + ATT_LOOKAHEAD < n_sub:
            pending.append(scores(r + ATT_LOOKAHEAD))
        values(r, *weights(pending.pop(0)))


def _da_attention(qkv16, lam, subln_g, lambda_init, n_lat):
    batch, n_tot, _ = qkv16.shape
    hw = 2 * DA_DIM
    kern = functools.partial(_da_attn_kernel, lambda_init=lambda_init)
    tq = ATT_Q_TILE if n_lat % ATT_Q_TILE == 0 else ROW_TILE
    return pl.pallas_call(
        kern,
        grid=(batch, DA_HEADS, n_lat // tq),
        in_specs=[
            pl.BlockSpec((1, tq, hw), lambda b, h, i: (b, i, h)),
            pl.BlockSpec((1, n_tot, hw), lambda b, h, i: (b, 0, DA_HEADS + h)),
            pl.BlockSpec((1, n_tot, hw), lambda b, h, i: (b, 0, 2 * DA_HEADS + h)),
            pl.BlockSpec(lam.shape, lambda b, h, i: (0, 0)),
            pl.BlockSpec((1, hw), lambda b, h, i: (0, 0)),
        ],
        out_specs=pl.BlockSpec((1, tq, hw), lambda b, h, i: (b, i, h)),
        out_shape=jax.ShapeDtypeStruct((batch, n_lat, DA_HEADS * hw), BF16),
        compiler_params=_cparams(("arbitrary", "arbitrary", "arbitrary"), 48),
        name="da_attention",
    )(qkv16, qkv16, qkv16, lam, subln_g)


def _rope_tables(n_lat, n_tot):
    quarter = DA_DIM // 4
    inv_freq = ROPE_BASE ** (-jnp.arange(quarter, dtype=F32) / quarter)
    rows = n_lat // GRID_W
    row = jnp.repeat(jnp.arange(rows, dtype=F32), GRID_W)
    col = jnp.tile(jnp.arange(GRID_W, dtype=F32), rows)
    ang_r = row[:, None] * inv_freq
    ang_c = col[:, None] * inv_freq
    ang = jnp.concatenate([ang_r, ang_r, ang_c, ang_c], axis=-1)
    reps = LANES // DA_DIM
    cos = jnp.tile(jnp.cos(ang), (1, reps))
    sin = jnp.tile(jnp.sin(ang), (1, reps))
    upper = (jnp.arange(LANES) % (2 * quarter)) >= quarter
    sin_a = jnp.where(upper, sin, 0.0)
    sin_b = jnp.where(upper, 0.0, -sin)
    n_ctx = n_tot - n_lat
    ext = lambda t, fill: jnp.concatenate([t, jnp.full((n_ctx, LANES), fill, F32)], axis=0)
    return ext(cos, 1.0), ext(sin_a, 0.0), ext(sin_b, 0.0)


def kernel(x, c, ctx, c_ctx, ada_w, ada_b, ln1_g, ln1_b, ln2_g, ln2_b, ffn_w_gate, ffn_w_up, ffn_w_down,
           dn_w_in, dn_conv, dn_a_log, dn_dt_bias, dn_norm_g, dn_w_out, da_w_in, da_lambda, da_subln_g,
           da_w_out):
    batch, n_lat, d = x.shape
    n_ctx = ctx.shape[1]
    n_tot = n_lat + n_ctx
    depth = ada_w.shape[0]
    assert depth == 2, "layer 0 is gated DeltaNet, layer 1 (the last) differential attention"
    assert batch < MOD_ROWS and n_lat % ROW_TILE == 0 and n_ctx % ROW_TILE == 0
    alpha = (2.0 * depth) ** 0.25
    hk = DN_HEADS * DN_DK

    cs = jnp.concatenate([c, c_ctx[None, :], jnp.zeros((MOD_ROWS - batch - 1, d), F32)], axis=0)
    mods = _ada_modulation(cs, ada_w, ada_b).reshape(depth, MOD_ROWS, 1, 6 * d)

    mod = mods[0]
    gate_cols = dn_w_in[0][:, 4 * hk:]
    w_gates = jnp.concatenate([gate_cols, jnp.zeros((d, LANES - gate_cols.shape[1]), F32)], axis=1)
    lane_pad = lambda v: jnp.concatenate([v.reshape(1, -1), jnp.zeros((1, LANES - v.size), F32)], axis=1)
    qkv, z, gcol, grow = _dn_project(x, ctx, mod, dn_w_in[0], w_gates, dn_conv[0],
                                     lane_pad(dn_a_log[0]), lane_pad(dn_dt_bias[0]))
    o_fwd, o_bwd = _dn_scan(qkv, gcol, grow, n_lat)
    vec = lambda v: v.reshape(1, -1)
    ffn16 = lambda i: tuple(w[i].astype(BF16) for w in (ffn_w_gate, ffn_w_up, ffn_w_down))
    xc = _sublayers((o_fwd, o_bwd, z, vec(dn_norm_g[0])), (x, ctx), mod, dn_w_out[0].astype(BF16),
                    (vec(ln1_g[0]), vec(ln1_b[0])), ffn16(0), (vec(ln2_g[0]), vec(ln2_b[0])),
                    alpha, n_lat, n_tot, True)

    mod = mods[1]
    lambda_init = 0.8 - 0.6 * math.exp(-0.3 * 1)
    cos_t, sin_a, sin_b = _rope_tables(n_lat, n_tot)
    qkv16 = _da_project(xc, mod, da_w_in[0], cos_t, sin_a, sin_b, n_lat)
    y = _da_attention(qkv16, da_lambda[0], da_subln_g[0].reshape(1, -1), lambda_init, n_lat)
    return _sublayers((y,), (xc,), mod, da_w_out[0].astype(BF16), (vec(ln1_g[1]), vec(ln1_b[1])), ffn16(1),
                      (vec(ln2_g[1]), vec(ln2_b[1])), alpha, n_lat, n_lat, False)
```

```python
import functools
import math

import jax
import jax.numpy as jnp
from jax import lax
from jax.experimental import pallas as pl
from jax.experimental.pallas import tpu as pltpu

F32 = jnp.float32
BF16 = jnp.bfloat16

DN_HEADS = 8
DN_DK = 128
DN_DV = 128
DN_CONV = 5
DN_CHUNK = 64
DA_HEADS = 8
DA_DIM = 64
GRID_W = 64
ROPE_BASE = 10000.0
LN_EPS = 1e-5
RMS_EPS = 1e-6

LANES = 128
SUBLANES = 8
MOD_ROWS = 16
ROW_TILE = 256
PROJ_ROW_BLOCK = 512
COL_TILE = 256
DA_COL_TILE = 512
ADA_COL_TILE = 1536
ATT_Q_TILE = 2048
ATT_SUB_ROWS = 256
ATT_LOOKAHEAD = 2
LOG2E = math.log2(math.e)


def _cparams(dims, vmem_mb):
    return pltpu.CompilerParams(dimension_semantics=dims, vmem_limit_bytes=vmem_mb * 1024 * 1024)


def _silu(x):
    return x * jax.nn.sigmoid(x)


def _dot(a, b):
    return jnp.dot(a, b, preferred_element_type=F32)


def _dot_nt(a, b):
    return lax.dot_general(a, b, (((1,), (1,)), ((), ())), preferred_element_type=F32)


def _layernorm(x, g, b):
    mu = jnp.mean(x, axis=-1, keepdims=True)
    xc = x - mu
    var = jnp.mean(xc * xc, axis=-1, keepdims=True)
    return xc * lax.rsqrt(var + LN_EPS) * g + b


def _ada_kernel(cs_ref, w_ref, b_ref, o_ref):
    s = _silu(cs_ref[...]).astype(BF16)
    o_ref[0] = _dot(s, w_ref[0].astype(BF16)) + b_ref[0]


def _ada_modulation(cs, ada_w, ada_b):
    depth, d, n = ada_w.shape
    tn = ADA_COL_TILE if n % ADA_COL_TILE == 0 else n
    return pl.pallas_call(
        _ada_kernel,
        grid=(depth, n // tn),
        in_specs=[
            pl.BlockSpec((MOD_ROWS, d), lambda i, j: (0, 0)),
            pl.BlockSpec((1, d, tn), lambda i, j: (i, 0, j)),
            pl.BlockSpec((1, 1, tn), lambda i, j: (i, 0, j)),
        ],
        out_specs=pl.BlockSpec((1, MOD_ROWS, tn), lambda i, j: (i, 0, j)),
        out_shape=jax.ShapeDtypeStruct((depth, MOD_ROWS, n), F32),
        compiler_params=_cparams(("arbitrary", "arbitrary"), 40),
        name="ada_modulation",
    )(cs, ada_w, ada_b.reshape(depth, 1, n))


def _modulate_into(xb_ref, lat, ctx, sc_l, sh_l, sc_c, sh_c):
    n_lat = lat.shape[0]
    xb_ref[:n_lat, :] = (lat * (1.0 + sc_l[0]) + sh_l[0]).astype(BF16)
    xb_ref[n_lat:, :] = (ctx * (1.0 + sc_c[0]) + sh_c[0]).astype(BF16)


def _row_blocks(n_lat, n_tot):
    blocks = []
    for seg0, seg in ((0, n_lat), (n_lat, n_tot - n_lat)):
        size = PROJ_ROW_BLOCK if seg % PROJ_ROW_BLOCK == 0 else seg
        blocks += [(seg0 + r, size) for r in range(0, seg, size)]
    return blocks


def _pipelined(blocks, produce, consume):
    pending = None
    for blk in blocks:
        cur = produce(*blk)
        if pending is not None:
            consume(*pending)
        pending = blk + (cur,)
    consume(*pending)


def _mod_specs(batch, d, sc_idx, sh_idx):
    return [
        pl.BlockSpec((1, 1, d), lambda b, j: (b, 0, sc_idx)),
        pl.BlockSpec((1, 1, d), lambda b, j: (b, 0, sh_idx)),
        pl.BlockSpec((1, 1, d), lambda b, j: (batch, 0, sc_idx)),
        pl.BlockSpec((1, 1, d), lambda b, j: (batch, 0, sh_idx)),
    ]


def _dn_proj_kernel(x_ref, c_ref, sc_l, sh_l, sc_c, sh_c, w_ref, wg_ref, conv_ref, alog_ref, dt_ref,
                    qkv_ref, z_ref, gcol_ref, grow_ref, xb_ref, pad_ref,
                    *, n_lat, n_tot, n_qkv_tiles, n_z_tiles):
    j = pl.program_id(1)
    n_ctx = n_tot - n_lat
    halo = SUBLANES
    pad = (DN_CONV - 1) // 2
    lat0 = halo
    ctx0 = 2 * halo + n_lat

    @pl.when(j == 0)
    def _():
        _modulate_into(xb_ref, x_ref[0], c_ref[0], sc_l, sh_l, sc_c, sh_c)

    blocks = _row_blocks(n_lat, n_tot)

    def projector():
        w16 = w_ref[0].astype(BF16)
        return lambda r0, rows: _dot(xb_ref[r0:r0 + rows, :], w16)

    @pl.when(j < n_qkv_tiles)
    def _():
        project = projector()
        zeros = jnp.zeros((halo, COL_TILE), F32)
        pad_ref[0:halo, :] = zeros
        pad_ref[lat0 + n_lat:ctx0, :] = zeros
        pad_ref[ctx0 + n_ctx:ctx0 + n_ctx + halo, :] = zeros
        cw = conv_ref[...]
        n_qk_tiles = 2 * n_qkv_tiles // 3
        is_q = j < n_qk_tiles // 2
        is_qk = j < n_qk_tiles

        def padded(r0):
            return r0 + (lat0 if r0 < n_lat else ctx0 - n_lat)

        def stage(r0, rows):
            pad_ref[padded(r0):padded(r0) + rows, :] = project(r0, rows)

        def conv_norm(r0, rows, _):
            base = padded(r0)
            ext = pad_ref[base - halo:base + rows + halo, :]
            y = None
            for t in range(DN_CONV):
                shifted = ext if t == pad else pltpu.roll(ext, (pad - t) % ext.shape[0], 0)
                term = shifted[halo:halo + rows, :] * cw[t:t + 1, :]
                y = term if y is None else y + term
            y = _silu(y)
            for h0 in range(0, COL_TILE, DN_DK):
                yh = y[:, h0:h0 + DN_DK]
                ss = jnp.sum(yh * yh, axis=-1, keepdims=True)
                fac = lax.rsqrt(ss + RMS_EPS)
                fac = jnp.where(is_q, fac * (DN_DK ** -0.5), fac)
                fac = jnp.where(is_qk, fac, jnp.ones_like(fac))
                qkv_ref[0, r0:r0 + rows, h0:h0 + DN_DK] = yh * fac

        _pipelined(blocks, stage, conv_norm)

    @pl.when(jnp.logical_and(j >= n_qkv_tiles, j < n_qkv_tiles + n_z_tiles))
    def _():
        project = projector()
        for r0, rows in blocks:
            z_ref[0, r0:r0 + rows, :] = project(r0, rows)

    @pl.when(j == n_qkv_tiles + n_z_tiles)
    def _():
        t = _dot(xb_ref[...], wg_ref[...].astype(BF16))
        lane = lax.broadcasted_iota(jnp.int32, t.shape, 1)
        x = t + dt_ref[...]
        softplus = jnp.maximum(x, 0.0) + jnp.log1p(jnp.exp(-jnp.abs(x)))
        decay = -jnp.exp(alog_ref[...]) * softplus
        gates = jnp.where(lane < 2 * DN_HEADS, decay, jax.nn.sigmoid(t))
        ri = lax.broadcasted_iota(jnp.int32, (DN_CHUNK, DN_CHUNK), 0)
        ci = lax.broadcasted_iota(jnp.int32, (DN_CHUNK, DN_CHUNK), 1)
        tri = jnp.concatenate([ri >= ci, ri <= ci], axis=0).astype(BF16)
        hi = gates.astype(BF16)
        rest = gates - hi.astype(F32)
        mid = rest.astype(BF16)
        low = (rest - mid.astype(F32)).astype(BF16)
        parts = jnp.concatenate([hi, mid, low], axis=1)
        lane_c = lax.broadcasted_iota(jnp.int32, (DN_CHUNK, LANES), 1)
        for c in range(n_tot // DN_CHUNK):
            gc = gates[c * DN_CHUNK:(c + 1) * DN_CHUNK]
            cum3 = _dot(tri, parts[c * DN_CHUNK:(c + 1) * DN_CHUNK])
            cum = cum3[:, :LANES] + cum3[:, LANES:2 * LANES] + cum3[:, 2 * LANES:]
            col = jnp.where(lane_c < DN_HEADS, cum[:DN_CHUNK],
                            jnp.where(lane_c < 2 * DN_HEADS, cum[DN_CHUNK:], gc))
            gcol_ref[0, c * DN_CHUNK:(c + 1) * DN_CHUNK, :] = col
            grow_ref[0, c] = jnp.concatenate([col, col], axis=0).T


def _dn_project(x, ctx, mod, w_in, w_gates, conv_w, alog_row, dt_row):
    batch, n_lat, d = x.shape
    n_tot = n_lat + ctx.shape[1]
    hk = DN_HEADS * DN_DK
    n_qkv_tiles = 3 * hk // COL_TILE
    n_z_tiles = hk // COL_TILE
    n_tiles = n_qkv_tiles + n_z_tiles + 1
    nc = n_tot // DN_CHUNK
    kern = functools.partial(_dn_proj_kernel, n_lat=n_lat, n_tot=n_tot,
                             n_qkv_tiles=n_qkv_tiles, n_z_tiles=n_z_tiles)
    return pl.pallas_call(
        kern,
        grid=(batch, n_tiles),
        in_specs=[pl.BlockSpec((1, n_lat, d), lambda b, j: (b, 0, 0)),
                  pl.BlockSpec((1, n_tot - n_lat, d), lambda b, j: (b, 0, 0))]
        + _mod_specs(batch, d, 1, 0)
        + [
            pl.BlockSpec((1, d, COL_TILE), lambda b, j: (0, 0, jnp.minimum(j, n_tiles - 2))),
            pl.BlockSpec((d, LANES), lambda b, j: (0, 0)),
            pl.BlockSpec((DN_CONV, COL_TILE), lambda b, j: (0, jnp.minimum(j, n_qkv_tiles - 1))),
            pl.BlockSpec((1, LANES), lambda b, j: (0, 0)),
            pl.BlockSpec((1, LANES), lambda b, j: (0, 0)),
        ],
        out_specs=[
            pl.BlockSpec((1, n_tot, COL_TILE), lambda b, j: (b, 0, jnp.minimum(j, n_qkv_tiles - 1))),
            pl.BlockSpec((1, n_tot, COL_TILE),
                         lambda b, j: (b, 0, jnp.clip(j - n_qkv_tiles, 0, n_z_tiles - 1))),
            pl.BlockSpec((1, n_tot, LANES), lambda b, j: (b, 0, 0)),
            pl.BlockSpec((1, nc, LANES, LANES), lambda b, j: (b, 0, 0, 0)),
        ],
        out_shape=[
            jax.ShapeDtypeStruct((batch, n_tot, 3 * hk), F32),
            jax.ShapeDtypeStruct((batch, n_tot, hk), F32),
            jax.ShapeDtypeStruct((batch, n_tot, LANES), F32),
            jax.ShapeDtypeStruct((batch, nc, LANES, LANES), F32),
        ],
        scratch_shapes=[
            pltpu.VMEM((n_tot, d), BF16),
            pltpu.VMEM((n_tot + 3 * SUBLANES, COL_TILE), F32),
        ],
        compiler_params=_cparams(("arbitrary", "arbitrary"), 56),
        name="dn_project",
    )(x, ctx, mod, mod, mod, mod, w_in, w_gates, conv_w, alog_row, dt_row)


INV_BASE = 16
SCAN_BATCH = 2
SCAN_UNITS = SCAN_BATCH * DN_HEADS


def _side_by_side(fwd, bwd):
    return jnp.concatenate([fwd, bwd], axis=1)


def _block_diag16(top, bottom):
    top, bottom = top.astype(BF16), bottom.astype(BF16)
    return jnp.concatenate([_side_by_side(top, jnp.zeros((top.shape[0], bottom.shape[1]), BF16)),
                            _side_by_side(jnp.zeros((bottom.shape[0], top.shape[1]), BF16), bottom)], axis=0)


def _dn_local_stages(ins, slot, sidx, half):
    u_ref, wq_ref, m_ref, e_ref = slot
    c = DN_CHUNK
    st = [dict() for _ in range(SCAN_UNITS)]

    def each(fn):
        def run():
            for n, t in enumerate(st):
                fn(t, n)
        return run

    def grids():
        ri = lax.broadcasted_iota(jnp.int32, (c, 2 * c), 0)
        lane = lax.broadcasted_iota(jnp.int32, (c, 2 * c), 1)
        return ri, lane & (c - 1), lane < c

    def pair_dot(x, y):
        _, _, fwd_half = grids()
        rhs = jnp.concatenate([jnp.where(fwd_half, y, 0.0).astype(BF16),
                               jnp.where(fwd_half, 0.0, y).astype(BF16)], axis=0)
        return _dot(x.astype(BF16), rhs)

    def load(t, n):
        bb, hh = divmod(n, DN_HEADS)
        ri, cj, fwd_half = grids()
        lane = lax.broadcasted_iota(jnp.int32, (c, LANES), 1)
        cols = slice(hh * DN_DK, (hh + 1) * DN_DK)
        per_dir = []
        for d in range(2):
            q_ref, k_ref, v_ref, gcol_ref, grow_ref = ins[d]
            part = half if d == 0 else 1 - half
            rows = slice(part * c, (part + 1) * c)
            gates = gcol_ref[bb, rows, :]
            g_lane = d * DN_HEADS + hh
            gcum = jnp.sum(jnp.where(lane == g_lane, gates, 0.0), axis=-1, keepdims=True)
            beta = jnp.sum(jnp.where(lane == 2 * DN_HEADS + g_lane, gates, 0.0), axis=-1, keepdims=True)
            grow = grow_ref[bb, part, g_lane:g_lane + 1, :]
            g_last = grow[:, c - 1:c] if d == 0 else grow[:, 0:1]
            per_dir.append((q_ref[bb, rows, cols], k_ref[bb, rows, cols], v_ref[bb, rows, cols],
                            gcum, beta, grow, g_last))
        (qf, kf, vf, gcf, bf, grf, glf), (qb, kb, vb, gcb, bb, grb, glb) = per_dir
        incl = jnp.logical_or(ri == cj, (ri > cj) == fwd_half)
        diff = jnp.where(fwd_half, gcf, gcb) - jnp.where(fwd_half[0:1], grf, grb)
        t["gamma"] = jnp.where(incl, jnp.exp(jnp.where(incl, diff, 0.0)), 0.0)
        kbf, kbb = kf * bf, kb * bb
        dec_f, dec_b = jnp.exp(gcf), jnp.exp(gcb)
        t["lhs16"] = _side_by_side(jnp.concatenate([kbf, qf], axis=0).astype(BF16),
                                   jnp.concatenate([kbb, qb], axis=0).astype(BF16))
        t["keys16"] = _block_diag16(kf, kb)
        t["rhs16"] = _block_diag16(_side_by_side(vf * bf, kbf * dec_f), _side_by_side(vb * bb, kbb * dec_b))
        wq_ref[sidx, n, c:, :DN_DK] = (qf * dec_f).astype(BF16)
        wq_ref[sidx, n, c:, DN_DK:] = (qb * dec_b).astype(BF16)
        k_dec = jnp.concatenate([kf * jnp.exp(glf - gcf), kb * jnp.exp(glb - gcb)], axis=0)
        m_ref[sidx, n, c:, :] = k_dec.T.astype(BF16)
        e_ref[sidx, n] = _side_by_side(jnp.broadcast_to(jnp.exp(glf), (SUBLANES, DN_DV)),
                                        jnp.broadcast_to(jnp.exp(glb), (SUBLANES, DN_DV)))

    def gram(t, n):
        t["kq"] = _dot_nt(t.pop("lhs16"), t.pop("keys16"))

    def split(t, n):
        ri, cj, fwd_half = grids()
        strict = jnp.logical_and(ri != cj, (ri > cj) == fwd_half)
        kq = t.pop("kq")
        gamma = t.pop("gamma")
        a = jnp.where(strict, kq[:c] * gamma, 0.0)
        m_ref[sidx, n, :c, :] = (kq[c:] * gamma).astype(BF16)
        a_diag = jnp.where((ri // INV_BASE) == (cj // INV_BASE), a, 0.0)
        t["a"] = a
        t["inv"] = (ri == cj).astype(F32) - a_diag
        t["pw"] = pair_dot(a_diag, a_diag)

    def series(t, n):
        both = pair_dot(jnp.concatenate([t["pw"], t["inv"]], axis=0), t["pw"])
        t["pw"] = both[:c]
        t["inv"] = t["inv"] + both[c:]

    def series_last(t, n):
        t["inv"] = t["inv"] + pair_dot(t["inv"], t.pop("pw"))

    def merge_a(blk):
        def fn(t, n):
            ri, cj, _ = grids()
            off = jnp.logical_and((ri // (2 * blk)) == (cj // (2 * blk)), (ri // blk) != (cj // blk))
            t["y"] = pair_dot(jnp.where(off, t["a"], 0.0), t["inv"])
        return fn

    def merge_b(t, n):
        t["inv"] = t["inv"] - pair_dot(t["inv"], t.pop("y"))

    def solve(t, n):
        t["uw"] = _dot(t.pop("inv").astype(BF16), t.pop("rhs16"))
        t.pop("a")

    def store(t, n):
        uw = t.pop("uw")
        for d in range(2):
            base = d * (DN_DV + DN_DK)
            u_ref[sidx, n, :, d * DN_DV:(d + 1) * DN_DV] = uw[:, base:base + DN_DV]
            wq_ref[sidx, n, :c, d * DN_DK:(d + 1) * DN_DK] = uw[:, base + DN_DV:base + DN_DV + DN_DK].astype(BF16)

    stages = [load, gram, split] + [series] * (int(math.log2(INV_BASE)) - 2) + [series_last]
    blk = INV_BASE
    while blk < c:
        stages += [merge_a(blk), merge_b]
        blk *= 2
    stages += [solve, store]
    return [each(fn) for fn in stages]


def _dn_serial_stages(outs, s_ref, slot, sidx, half):
    u_ref, wq_ref, m_ref, e_ref = slot
    c = DN_CHUNK
    st = [dict() for _ in range(SCAN_UNITS)]

    def each(fn):
        def run():
            for n, t in enumerate(st):
                fn(t, n)
        return run

    def state_dot(t, n):
        t["s"] = (s_ref[n], s_ref[SCAN_UNITS + n])
        t["ws"] = _dot(wq_ref[sidx, n], _block_diag16(*t["s"]))

    def value_dot(t, n):
        v_new = u_ref[sidx, n] - t["ws"][:c]
        t["mv"] = _dot(m_ref[sidx, n], _block_diag16(v_new[:, :DN_DV], v_new[:, DN_DV:]))

    def update(t, n):
        bb, hh = divmod(n, DN_HEADS)
        mv, ws, s = t.pop("mv"), t.pop("ws"), t.pop("s")
        gain = e_ref[sidx, n][0:1, :]
        for d in range(2):
            lanes = slice(d * DN_DV, (d + 1) * DN_DV)
            s_ref[d * SCAN_UNITS + n] = s[d] * gain[:, lanes] + mv[c:, lanes]
            part = half if d == 0 else 1 - half
            outs[d][bb, part * c:(part + 1) * c, hh * DN_DV:(hh + 1) * DN_DV] = ws[c:, lanes] + mv[:c, lanes]

    return [each(fn) for fn in (state_dot, value_dot, update)]


def _dn_scan_kernel(*refs):
    ins = (refs[0:5], refs[5:10])
    outs = refs[10:12]
    s_ref = refs[12]
    slot_sets = (refs[13:17], refs[17:21])
    g = pl.program_id(1)

    @pl.when(g == 0)
    def _():
        for ref in (s_ref,) + tuple(slot_sets[0]) + tuple(slot_sets[1]):
            ref[...] = jnp.zeros(ref.shape, ref.dtype)

    def step(written, read):
        loc = [_dn_local_stages(ins, written, half, half) for half in range(2)]
        ser = [stage for half in range(2) for stage in _dn_serial_stages(outs, s_ref, read, half, half)]
        gap = len(loc[0]) // len(ser)
        for n, stages in enumerate(zip(*loc)):
            if n % gap == 0 and n // gap < len(ser):
                ser[n // gap]()
            for stage in stages:
                stage()

    for parity in range(2):
        pl.when(g % 2 == parity)(functools.partial(step, slot_sets[parity], slot_sets[1 - parity]))


def _dn_scan(qkv, gcol, grow, n_lat):
    batch, n_tot, _ = qkv.shape
    hk = DN_HEADS * DN_DK
    pair = 2 * DN_CHUNK
    n_pairs = n_tot // pair
    lat_pairs = n_lat // pair
    ctx_pairs = n_pairs - lat_pairs
    assert n_lat % pair == 0 and n_tot % pair == 0 and batch % SCAN_BATCH == 0

    def fwd(p):
        return jnp.where(p < ctx_pairs, p + lat_pairs, p - ctx_pairs)

    def bwd(p):
        return n_pairs - 1 - p

    def p_in(g):
        return jnp.minimum(g, n_pairs - 1)

    def p_out(g):
        return jnp.maximum(g - 1, 0)

    def in_specs(where):
        return [pl.BlockSpec((SCAN_BATCH, pair, hk), lambda b, g, col=col: (b, where(p_in(g)), col))
                for col in range(3)] + [
            pl.BlockSpec((SCAN_BATCH, pair, LANES), lambda b, g: (b, where(p_in(g)), 0)),
            pl.BlockSpec((SCAN_BATCH, 2, LANES, LANES), lambda b, g: (b, where(p_in(g)), 0, 0)),
        ]

    out_spec = lambda where: pl.BlockSpec((SCAN_BATCH, pair, hk), lambda b, g: (b, where(p_out(g)), 0))
    out_shape = jax.ShapeDtypeStruct((batch, n_tot, hk), F32)
    slot = [pltpu.VMEM((2, SCAN_UNITS, DN_CHUNK, 2 * DN_DV), F32),
            pltpu.VMEM((2, SCAN_UNITS, 2 * DN_CHUNK, 2 * DN_DK), BF16),
            pltpu.VMEM((2, SCAN_UNITS, DN_CHUNK + DN_DK, 2 * DN_CHUNK), BF16),
            pltpu.VMEM((2, SCAN_UNITS, SUBLANES, 2 * DN_DV), F32)]
    per_dir = (qkv, qkv, qkv, gcol, grow)
    return pl.pallas_call(
        _dn_scan_kernel,
        grid=(batch // SCAN_BATCH, n_pairs + 1),
        in_specs=in_specs(fwd) + in_specs(bwd),
        out_specs=[out_spec(fwd), out_spec(bwd)],
        out_shape=[out_shape, out_shape],
        scratch_shapes=[pltpu.VMEM((2 * SCAN_UNITS, DN_DK, DN_DV), F32)] + slot + slot,
        compiler_params=_cparams(("arbitrary", "arbitrary"), 48),
        name="dn_scan",
    )(*per_dir, *per_dir)


def _sublayers_kernel(*refs, alpha, gated_norm, split_residual, n_lat_tiles):
    if gated_norm:
        y_ref, y2_ref, z_ref, ng_ref = refs[:4]
    else:
        y_ref = refs[0]
    (gt1_ref, sc2_ref, sh2_ref, gt2_ref, wo_ref, l1g_ref, l1b_ref,
     wg_ref, wu_ref, wd_ref, l2g_ref, l2b_ref, o_ref, yb_ref) = refs[-14:]
    if split_residual:
        x_ref, c_ref = refs[-16:-14]
        residual = jnp.where(pl.program_id(1) < n_lat_tiles, x_ref[0], c_ref[0])
    else:
        residual = refs[-15][0]
    if gated_norm:
        for h0 in range(0, y_ref.shape[2], DN_DV):
            yh = y_ref[0, :, h0:h0 + DN_DV] + y2_ref[0, :, h0:h0 + DN_DV]
            ms = jnp.mean(yh * yh, axis=-1, keepdims=True)
            yh = yh * lax.rsqrt(ms + RMS_EPS) * ng_ref[...]
            yb_ref[:, h0:h0 + DN_DV] = (yh * _silu(z_ref[0, :, h0:h0 + DN_DV])).astype(BF16)
        mixed = yb_ref[...]
    else:
        mixed = y_ref[0]
    x1 = _layernorm(alpha * residual + gt1_ref[0] * _dot(mixed, wo_ref[...]), l1g_ref[...], l1b_ref[...])
    h = (x1 * (1.0 + sc2_ref[0]) + sh2_ref[0]).astype(BF16)
    gate = _dot(h, wg_ref[...])
    up = _dot(h, wu_ref[...])
    act = (_silu(gate) * up).astype(BF16)
    o_ref[0] = _layernorm(alpha * x1 + gt2_ref[0] * _dot(act, wd_ref[...]), l2g_ref[...], l2b_ref[...])


def _sublayers(ys, residual, mod, w_out16, ln1, ffn16, ln2, alpha, n_lat, n_out, gated_norm):
    batch, _, d = residual[0].shape
    hv = ys[0].shape[2]
    n_lat_tiles = n_lat // ROW_TILE
    split = len(residual) == 2
    kern = functools.partial(_sublayers_kernel, alpha=alpha, gated_norm=gated_norm, split_residual=split,
                             n_lat_tiles=n_lat_tiles)
    tok = lambda width: pl.BlockSpec((1, ROW_TILE, width), lambda b, t: (b, t, 0))
    res_specs = [tok(d)] if not split else [
        pl.BlockSpec((1, ROW_TILE, d), lambda b, t: (b, jnp.minimum(t, n_lat_tiles - 1), 0)),
        pl.BlockSpec((1, ROW_TILE, d), lambda b, t: (b, jnp.maximum(t - n_lat_tiles, 0), 0))]
    mod_spec = lambda idx: pl.BlockSpec((1, 1, d), lambda b, t: (jnp.where(t < n_lat_tiles, b, batch), 0, idx))
    resident = pl.BlockSpec(memory_space=pltpu.VMEM)
    vec = pl.BlockSpec((1, d), lambda b, t: (0, 0))
    return pl.pallas_call(
        kern,
        grid=(batch, n_out // ROW_TILE),
        in_specs=([tok(hv)] * 3 + [pl.BlockSpec((1, ys[3].shape[1]), lambda b, t: (0, 0))]
                  if gated_norm else [tok(hv)]) + res_specs
        + [mod_spec(2), mod_spec(4), mod_spec(3), mod_spec(5), resident, vec, vec, resident, resident, resident,
           vec, vec],
        out_specs=tok(d),
        out_shape=jax.ShapeDtypeStruct((batch, n_out, d), F32),
        scratch_shapes=[pltpu.VMEM((ROW_TILE, hv), BF16)],
        compiler_params=_cparams(("arbitrary", "arbitrary"), 56),
        name="sublayers",
    )(*ys, *residual, mod, mod, mod, mod, w_out16, *ln1, *ffn16, *ln2)


def _da_proj_kernel(x_ref, sc_l, sh_l, sc_c, sh_c, w_ref, cos_ref, sin_a_ref, sin_b_ref,
                    o_ref, xb_ref, *, n_lat, n_qk_tiles):
    j = pl.program_id(1)

    @pl.when(j == 0)
    def _():
        _modulate_into(xb_ref, x_ref[0, :n_lat, :], x_ref[0, n_lat:, :], sc_l, sh_l, sc_c, sh_c)

    blocks = _row_blocks(n_lat, x_ref.shape[1])

    def projector():
        w16 = w_ref[0].astype(BF16)
        return lambda r0, rows: _dot(xb_ref[r0:r0 + rows, :], w16)

    @pl.when(j < n_qk_tiles)
    def _():
        project = projector()
        scale = jnp.where(j < n_qk_tiles // 2, DA_DIM ** -0.5 * LOG2E, 1.0)
        quarter = DA_DIM // 4

        def rope(r0, rows, acc):
            cos, sin_a, sin_b = (t[r0:r0 + rows, :] for t in (cos_ref, sin_a_ref, sin_b_ref))
            for h0 in range(0, DA_COL_TILE, LANES):
                xh = acc[:, h0:h0 + LANES]
                rot = pltpu.roll(xh, quarter, 1) * sin_a + pltpu.roll(xh, LANES - quarter, 1) * sin_b
                o_ref[0, r0:r0 + rows, h0:h0 + LANES] = ((xh * cos + rot) * scale).astype(BF16)

        _pipelined(blocks, project, rope)

    @pl.when(j >= n_qk_tiles)
    def _():
        project = projector()
        for r0, rows in blocks:
            o_ref[0, r0:r0 + rows, :] = project(r0, rows).astype(BF16)


def _da_project(xc, mod, w_in, cos_t, sin_a, sin_b, n_lat):
    batch, n_tot, d = xc.shape
    n_cols = w_in.shape[2]
    n_tiles = n_cols // DA_COL_TILE
    n_qk_tiles = 2 * n_tiles // 3
    kern = functools.partial(_da_proj_kernel, n_lat=n_lat, n_qk_tiles=n_qk_tiles)
    table = pl.BlockSpec((n_tot, LANES), lambda b, j: (0, 0))
    return pl.pallas_call(
        kern,
        grid=(batch, n_tiles),
        in_specs=[pl.BlockSpec((1, n_tot, d), lambda b, j: (b, 0, 0))]
        + _mod_specs(batch, d, 1, 0)
        + [pl.BlockSpec((1, d, DA_COL_TILE), lambda b, j: (0, 0, j)), table, table, table],
        out_specs=pl.BlockSpec((1, n_tot, DA_COL_TILE), lambda b, j: (b, 0, j)),
        out_shape=jax.ShapeDtypeStruct((batch, n_tot, n_cols), BF16),
        scratch_shapes=[pltpu.VMEM((n_tot, d), BF16)],
        compiler_params=_cparams(("arbitrary", "arbitrary"), 56),
        name="da_project",
    )(xc, mod, mod, mod, mod, w_in, cos_t, sin_a, sin_b)


def _da_attn_kernel(q_ref, k_ref, v_ref, lam_ref, g_ref, o_ref, *, lambda_init):
    lam = lam_ref[...]
    lam_val = (jnp.exp(jnp.sum(lam[0:1] * lam[1:2], axis=-1, keepdims=True))
               - jnp.exp(jnp.sum(lam[2:3] * lam[3:4], axis=-1, keepdims=True)) + lambda_init)
    k = k_ref[0]
    n_sub = q_ref.shape[1] // ATT_SUB_ROWS
    lane = lax.broadcasted_iota(jnp.int32, (ATT_SUB_ROWS, 2 * DA_DIM), 1)

    def scores(r):
        q = q_ref[0, r * ATT_SUB_ROWS:(r + 1) * ATT_SUB_ROWS, :]
        zero = jnp.zeros_like(q)
        return [_dot_nt(jnp.where((lane < DA_DIM) if comp == 0 else (lane >= DA_DIM), q, zero), k)
                for comp in range(2)]

    def weights(s):
        e0, e1 = (jnp.exp2(sc - jnp.max(sc, axis=-1, keepdims=True)) for sc in s)
        l0 = jnp.sum(e0, axis=-1, keepdims=True)
        l1 = jnp.sum(e1, axis=-1, keepdims=True)
        return (e0 - e1 * (lam_val * l0 / l1)).astype(BF16), 1.0 / l0

    def values(r, a16, inv_l0):
        o = _dot(a16, v_ref[0]) * inv_l0
        ms = jnp.mean(o * o, axis=-1, keepdims=True)
        o_ref[0, r * ATT_SUB_ROWS:(r + 1) * ATT_SUB_ROWS, :] = (
            o * lax.rsqrt(ms + RMS_EPS) * g_ref[...] * (1.0 - lambda_init)).astype(BF16)

    pending = [scores(r) for r in range(min(ATT_LOOKAHEAD, n_sub))]
    for r in range(n_sub):
        if r + ATT_LOOKAHEAD < n_sub:
            pending.append(scores(r + ATT_LOOKAHEAD))
        values(r, *weights(pending.pop(0)))


def _da_attention(qkv16, lam, subln_g, lambda_init, n_lat):
    batch, n_tot, _ = qkv16.shape
    hw = 2 * DA_DIM
    kern = functools.partial(_da_attn_kernel, lambda_init=lambda_init)
    tq = ATT_Q_TILE if n_lat % ATT_Q_TILE == 0 else ROW_TILE
    return pl.pallas_call(
        kern,
        grid=(batch, DA_HEADS, n_lat // tq),
        in_specs=[
            pl.BlockSpec((1, tq, hw), lambda b, h, i: (b, i, h)),
            pl.BlockSpec((1, n_tot, hw), lambda b, h, i: (b, 0, DA_HEADS + h)),
            pl.BlockSpec((1, n_tot, hw), lambda b, h, i: (b, 0, 2 * DA_HEADS + h)),
            pl.BlockSpec(lam.shape, lambda b, h, i: (0, 0)),
            pl.BlockSpec((1, hw), lambda b, h, i: (0, 0)),
        ],
        out_specs=pl.BlockSpec((1, tq, hw), lambda b, h, i: (b, i, h)),
        out_shape=jax.ShapeDtypeStruct((batch, n_lat, DA_HEADS * hw), BF16),
        compiler_params=_cparams(("arbitrary", "arbitrary", "arbitrary"), 48),
        name="da_attention",
    )(qkv16, qkv16, qkv16, lam, subln_g)


def _rope_tables(n_lat, n_tot):
    quarter = DA_DIM // 4
    inv_freq = ROPE_BASE ** (-jnp.arange(quarter, dtype=F32) / quarter)
    rows = n_lat // GRID_W
    row = jnp.repeat(jnp.arange(rows, dtype=F32), GRID_W)
    col = jnp.tile(jnp.arange(GRID_W, dtype=F32), rows)
    ang_r = row[:, None] * inv_freq
    ang_c = col[:, None] * inv_freq
    ang = jnp.concatenate([ang_r, ang_r, ang_c, ang_c], axis=-1)
    reps = LANES // DA_DIM
    cos = jnp.tile(jnp.cos(ang), (1, reps))
    sin = jnp.tile(jnp.sin(ang), (1, reps))
    upper = (jnp.arange(LANES) % (2 * quarter)) >= quarter
    sin_a = jnp.where(upper, sin, 0.0)
    sin_b = jnp.where(upper, 0.0, -sin)
    n_ctx = n_tot - n_lat
    ext = lambda t, fill: jnp.concatenate([t, jnp.full((n_ctx, LANES), fill, F32)], axis=0)
    return ext(cos, 1.0), ext(sin_a, 0.0), ext(sin_b, 0.0)


def kernel(x, c, ctx, c_ctx, ada_w, ada_b, ln1_g, ln1_b, ln2_g, ln2_b, ffn_w_gate, ffn_w_up, ffn_w_down,
           dn_w_in, dn_conv, dn_a_log, dn_dt_bias, dn_norm_g, dn_w_out, da_w_in, da_lambda, da_subln_g,
           da_w_out):
    batch, n_lat, d = x.shape
    n_ctx = ctx.shape[1]
    n_tot = n_lat + n_ctx
    depth = ada_w.shape[0]
    assert depth == 2, "layer 0 is gated DeltaNet, layer 1 (the last) differential attention"
    assert batch < MOD_ROWS and n_lat % ROW_TILE == 0 and n_ctx % ROW_TILE == 0
    alpha = (2.0 * depth) ** 0.25
    hk = DN_HEADS * DN_DK

    cs = jnp.concatenate([c, c_ctx[None, :], jnp.zeros((MOD_ROWS - batch - 1, d), F32)], axis=0)
    mods = _ada_modulation(cs, ada_w, ada_b).reshape(depth, MOD_ROWS, 1, 6 * d)

    mod = mods[0]
    gate_cols = dn_w_in[0][:, 4 * hk:]
    w_gates = jnp.concatenate([gate_cols, jnp.zeros((d, LANES - gate_cols.shape[1]), F32)], axis=1)
    lane_pad = lambda v: jnp.concatenate([v.reshape(1, -1), jnp.zeros((1, LANES - v.size), F32)], axis=1)
    qkv, z, gcol, grow = _dn_project(x, ctx, mod, dn_w_in, w_gates, dn_conv[0],
                                     lane_pad(dn_a_log[0]), lane_pad(dn_dt_bias[0]))
    o_fwd, o_bwd = _dn_scan(qkv, gcol, grow, n_lat)
    vec = lambda v: v.reshape(1, -1)
    ffn16 = lambda i: tuple(w[i].astype(BF16) for w in (ffn_w_gate, ffn_w_up, ffn_w_down))
    xc = _sublayers((o_fwd, o_bwd, z, vec(dn_norm_g[0])), (x, ctx), mod, dn_w_out[0].astype(BF16),
                    (vec(ln1_g[0]), vec(ln1_b[0])), ffn16(0), (vec(ln2_g[0]), vec(ln2_b[0])),
                    alpha, n_lat, n_tot, True)

    mod = mods[1]
    lambda_init = 0.8 - 0.6 * math.exp(-0.3 * 1)
    cos_t, sin_a, sin_b = _rope_tables(n_lat, n_tot)
    qkv16 = _da_project(xc, mod, da_w_in, cos_t, sin_a, sin_b, n_lat)
    y = _da_attention(qkv16, da_lambda[0], da_subln_g[0].reshape(1, -1), lambda_init, n_lat)
    return _sublayers((y,), (xc,), mod, da_w_out[0].astype(BF16), (vec(ln1_g[1]), vec(ln1_b[1])), ffn16(1),
                      (vec(ln2_g[1]), vec(ln2_b[1])), alpha, n_lat, n_lat, False)
```

```python
import functools
import math

import jax
import jax.numpy as jnp
from jax import lax
from jax.experimental import pallas as pl
from jax.experimental.pallas import tpu as pltpu

F32 = jnp.float32
BF16 = jnp.bfloat16

DN_HEADS = 8
DN_DK = 128
DN_DV = 128
DN_CONV = 5
DN_CHUNK = 64
DA_HEADS = 8
DA_DIM = 64
GRID_W = 64
ROPE_BASE = 10000.0
LN_EPS = 1e-5
RMS_EPS = 1e-6

LANES = 128
SUBLANES = 8
MOD_ROWS = 16
ROW_TILE = 256
WIDE_ROW_TILE = 512
PROJ_ROW_BLOCK = 512
COL_TILE = 256
DA_COL_TILE = 512
ADA_COL_TILE = 1536
ATT_Q_TILE = 2048
ATT_SUB_ROWS = 256
ATT_LOOKAHEAD = 2
LOG2E = math.log2(math.e)


def _cparams(dims, vmem_mb):
    return pltpu.CompilerParams(dimension_semantics=dims, vmem_limit_bytes=vmem_mb * 1024 * 1024)


def _silu(x):
    return x * jax.nn.sigmoid(x)


def _dot(a, b):
    return jnp.dot(a, b, preferred_element_type=F32)


def _dot_nt(a, b):
    return lax.dot_general(a, b, (((1,), (1,)), ((), ())), preferred_element_type=F32)


def _layernorm(x, g, b):
    mu = jnp.mean(x, axis=-1, keepdims=True)
    xc = x - mu
    var = jnp.mean(xc * xc, axis=-1, keepdims=True)
    return xc * lax.rsqrt(var + LN_EPS) * g + b


def _cast_kernel(w_ref, o_ref):
    o_ref[...] = w_ref[...].astype(BF16)


def _to_bf16(w):
    layers, rows, cols = w.shape
    tr = rows // 2
    block = pl.BlockSpec((1, tr, cols), lambda i, r: (i, r, 0))
    return pl.pallas_call(
        _cast_kernel,
        grid=(layers, rows // tr),
        in_specs=[block],
        out_specs=block,
        out_shape=jax.ShapeDtypeStruct(w.shape, BF16),
        compiler_params=_cparams(("arbitrary", "arbitrary"), 40),
        name="to_bf16",
    )(w)


def _ada_kernel(cs_ref, w_ref, b_ref, o_ref):
    s = _silu(cs_ref[...]).astype(BF16)
    o_ref[0] = _dot(s, w_ref[0].astype(BF16)) + b_ref[0]


def _ada_modulation(cs, ada_w, ada_b):
    depth, d, n = ada_w.shape
    tn = ADA_COL_TILE if n % ADA_COL_TILE == 0 else n
    return pl.pallas_call(
        _ada_kernel,
        grid=(depth, n // tn),
        in_specs=[
            pl.BlockSpec((MOD_ROWS, d), lambda i, j: (0, 0)),
            pl.BlockSpec((1, d, tn), lambda i, j: (i, 0, j)),
            pl.BlockSpec((1, 1, tn), lambda i, j: (i, 0, j)),
        ],
        out_specs=pl.BlockSpec((1, MOD_ROWS, tn), lambda i, j: (i, 0, j)),
        out_shape=jax.ShapeDtypeStruct((depth, MOD_ROWS, n), F32),
        compiler_params=_cparams(("arbitrary", "arbitrary"), 40),
        name="ada_modulation",
    )(cs, ada_w, ada_b.reshape(depth, 1, n))


def _modulate_into(xb_ref, lat, ctx, sc_l, sh_l, sc_c, sh_c):
    n_lat = lat.shape[0]
    xb_ref[:n_lat, :] = (lat * (1.0 + sc_l[0]) + sh_l[0]).astype(BF16)
    xb_ref[n_lat:, :] = (ctx * (1.0 + sc_c[0]) + sh_c[0]).astype(BF16)


def _row_blocks(n_lat, n_tot):
    blocks = []
    for seg0, seg in ((0, n_lat), (n_lat, n_tot - n_lat)):
        size = PROJ_ROW_BLOCK if seg % PROJ_ROW_BLOCK == 0 else seg
        blocks += [(seg0 + r, size) for r in range(0, seg, size)]
    return blocks


def _pipelined(blocks, produce, consume):
    pending = None
    for blk in blocks:
        cur = produce(*blk)
        if pending is not None:
            consume(*pending)
        pending = blk + (cur,)
    consume(*pending)


def _mod_specs(batch, d, sc_idx, sh_idx):
    return [
        pl.BlockSpec((1, 1, d), lambda b, j: (b, 0, sc_idx)),
        pl.BlockSpec((1, 1, d), lambda b, j: (b, 0, sh_idx)),
        pl.BlockSpec((1, 1, d), lambda b, j: (batch, 0, sc_idx)),
        pl.BlockSpec((1, 1, d), lambda b, j: (batch, 0, sh_idx)),
    ]


def _dn_proj_kernel(x_ref, c_ref, sc_l, sh_l, sc_c, sh_c, w_ref, wg_ref, conv_ref, alog_ref, dt_ref,
                    qkv_ref, z_ref, gcol_ref, grow_ref, xb_ref, pad_ref,
                    *, n_lat, n_tot, n_qkv_tiles, n_z_tiles):
    j = pl.program_id(1)
    n_ctx = n_tot - n_lat
    halo = SUBLANES
    pad = (DN_CONV - 1) // 2
    lat0 = halo
    ctx0 = 2 * halo + n_lat

    @pl.when(j == 0)
    def _():
        _modulate_into(xb_ref, x_ref[0], c_ref[0], sc_l, sh_l, sc_c, sh_c)

    blocks = _row_blocks(n_lat, n_tot)

    def projector():
        w16 = w_ref[0].astype(BF16)
        return lambda r0, rows: _dot(xb_ref[r0:r0 + rows, :], w16)

    @pl.when(j < n_qkv_tiles)
    def _():
        project = projector()
        zeros = jnp.zeros((halo, COL_TILE), F32)
        pad_ref[0:halo, :] = zeros
        pad_ref[lat0 + n_lat:ctx0, :] = zeros
        pad_ref[ctx0 + n_ctx:ctx0 + n_ctx + halo, :] = zeros
        cw = conv_ref[...]
        n_qk_tiles = 2 * n_qkv_tiles // 3
        is_q = j < n_qk_tiles // 2
        is_qk = j < n_qk_tiles

        def padded(r0):
            return r0 + (lat0 if r0 < n_lat else ctx0 - n_lat)

        def stage(r0, rows):
            pad_ref[padded(r0):padded(r0) + rows, :] = project(r0, rows)

        def conv_norm(r0, rows, _):
            base = padded(r0)
            ext = pad_ref[base - halo:base + rows + halo, :]
            y = None
            for t in range(DN_CONV):
                shifted = ext if t == pad else pltpu.roll(ext, (pad - t) % ext.shape[0], 0)
                term = shifted[halo:halo + rows, :] * cw[t:t + 1, :]
                y = term if y is None else y + term
            y = _silu(y)
            for h0 in range(0, COL_TILE, DN_DK):
                yh = y[:, h0:h0 + DN_DK]
                ss = jnp.sum(yh * yh, axis=-1, keepdims=True)
                fac = lax.rsqrt(ss + RMS_EPS)
                fac = jnp.where(is_q, fac * (DN_DK ** -0.5), fac)
                fac = jnp.where(is_qk, fac, jnp.ones_like(fac))
                qkv_ref[0, r0:r0 + rows, h0:h0 + DN_DK] = yh * fac

        _pipelined(blocks, stage, conv_norm)

    @pl.when(jnp.logical_and(j >= n_qkv_tiles, j < n_qkv_tiles + n_z_tiles))
    def _():
        project = projector()
        for r0, rows in blocks:
            z_ref[0, r0:r0 + rows, :] = project(r0, rows)

    @pl.when(j == n_qkv_tiles + n_z_tiles)
    def _():
        t = _dot(xb_ref[...], wg_ref[...].astype(BF16))
        lane = lax.broadcasted_iota(jnp.int32, t.shape, 1)
        x = t + dt_ref[...]
        softplus = jnp.maximum(x, 0.0) + jnp.log1p(jnp.exp(-jnp.abs(x)))
        decay = -jnp.exp(alog_ref[...]) * softplus
        gates = jnp.where(lane < 2 * DN_HEADS, decay, jax.nn.sigmoid(t))
        ri = lax.broadcasted_iota(jnp.int32, (DN_CHUNK, DN_CHUNK), 0)
        ci = lax.broadcasted_iota(jnp.int32, (DN_CHUNK, DN_CHUNK), 1)
        tri = jnp.concatenate([ri >= ci, ri <= ci], axis=0).astype(BF16)
        hi = gates.astype(BF16)
        rest = gates - hi.astype(F32)
        mid = rest.astype(BF16)
        low = (rest - mid.astype(F32)).astype(BF16)
        parts = jnp.concatenate([hi, mid, low], axis=1)
        lane_c = lax.broadcasted_iota(jnp.int32, (DN_CHUNK, LANES), 1)
        for c in range(n_tot // DN_CHUNK):
            gc = gates[c * DN_CHUNK:(c + 1) * DN_CHUNK]
            cum3 = _dot(tri, parts[c * DN_CHUNK:(c + 1) * DN_CHUNK])
            cum = cum3[:, :LANES] + cum3[:, LANES:2 * LANES] + cum3[:, 2 * LANES:]
            col = jnp.where(lane_c < DN_HEADS, cum[:DN_CHUNK],
                            jnp.where(lane_c < 2 * DN_HEADS, cum[DN_CHUNK:], gc))
            gcol_ref[0, c * DN_CHUNK:(c + 1) * DN_CHUNK, :] = col
            grow_ref[0, c] = jnp.concatenate([col, col], axis=0).T


def _dn_project(x, ctx, mod, w_in, w_gates, conv_w, alog_row, dt_row):
    batch, n_lat, d = x.shape
    n_tot = n_lat + ctx.shape[1]
    hk = DN_HEADS * DN_DK
    n_qkv_tiles = 3 * hk // COL_TILE
    n_z_tiles = hk // COL_TILE
    n_tiles = n_qkv_tiles + n_z_tiles + 1
    nc = n_tot // DN_CHUNK
    kern = functools.partial(_dn_proj_kernel, n_lat=n_lat, n_tot=n_tot,
                             n_qkv_tiles=n_qkv_tiles, n_z_tiles=n_z_tiles)
    return pl.pallas_call(
        kern,
        grid=(batch, n_tiles),
        in_specs=[pl.BlockSpec((1, n_lat, d), lambda b, j: (b, 0, 0)),
                  pl.BlockSpec((1, n_tot - n_lat, d), lambda b, j: (b, 0, 0))]
        + _mod_specs(batch, d, 1, 0)
        + [
            pl.BlockSpec((1, d, COL_TILE), lambda b, j: (0, 0, jnp.minimum(j, n_tiles - 2))),
            pl.BlockSpec((d, LANES), lambda b, j: (0, 0)),
            pl.BlockSpec((DN_CONV, COL_TILE), lambda b, j: (0, jnp.minimum(j, n_qkv_tiles - 1))),
            pl.BlockSpec((1, LANES), lambda b, j: (0, 0)),
            pl.BlockSpec((1, LANES), lambda b, j: (0, 0)),
        ],
        out_specs=[
            pl.BlockSpec((1, n_tot, COL_TILE), lambda b, j: (b, 0, jnp.minimum(j, n_qkv_tiles - 1))),
            pl.BlockSpec((1, n_tot, COL_TILE),
                         lambda b, j: (b, 0, jnp.clip(j - n_qkv_tiles, 0, n_z_tiles - 1))),
            pl.BlockSpec((1, n_tot, LANES), lambda b, j: (b, 0, 0)),
            pl.BlockSpec((1, nc, LANES, LANES), lambda b, j: (b, 0, 0, 0)),
        ],
        out_shape=[
            jax.ShapeDtypeStruct((batch, n_tot, 3 * hk), F32),
            jax.ShapeDtypeStruct((batch, n_tot, hk), F32),
            jax.ShapeDtypeStruct((batch, n_tot, LANES), F32),
            jax.ShapeDtypeStruct((batch, nc, LANES, LANES), F32),
        ],
        scratch_shapes=[
            pltpu.VMEM((n_tot, d), BF16),
            pltpu.VMEM((n_tot + 3 * SUBLANES, COL_TILE), F32),
        ],
        compiler_params=_cparams(("arbitrary", "arbitrary"), 56),
        name="dn_project",
    )(x, ctx, mod, mod, mod, mod, w_in, w_gates, conv_w, alog_row, dt_row)


INV_BASE = 16
SCAN_BATCH = 2
SCAN_UNITS = SCAN_BATCH * DN_HEADS


def _side_by_side(fwd, bwd):
    return jnp.concatenate([fwd, bwd], axis=1)


def _block_diag16(top, bottom):
    top, bottom = top.astype(BF16), bottom.astype(BF16)
    return jnp.concatenate([_side_by_side(top, jnp.zeros((top.shape[0], bottom.shape[1]), BF16)),
                            _side_by_side(jnp.zeros((bottom.shape[0], top.shape[1]), BF16), bottom)], axis=0)


def _dn_local_stages(ins, slot, sidx, half):
    u_ref, wq_ref, m_ref, e_ref = slot
    c = DN_CHUNK
    st = [dict() for _ in range(SCAN_UNITS)]

    def each(fn):
        def run():
            for n, t in enumerate(st):
                fn(t, n)
        return run

    def grids():
        ri = lax.broadcasted_iota(jnp.int32, (c, 2 * c), 0)
        lane = lax.broadcasted_iota(jnp.int32, (c, 2 * c), 1)
        return ri, lane & (c - 1), lane < c

    def pair_dot(x, y):
        _, _, fwd_half = grids()
        rhs = jnp.concatenate([jnp.where(fwd_half, y, 0.0).astype(BF16),
                               jnp.where(fwd_half, 0.0, y).astype(BF16)], axis=0)
        return _dot(x.astype(BF16), rhs)

    def load(t, n):
        bb, hh = divmod(n, DN_HEADS)
        ri, cj, fwd_half = grids()
        lane = lax.broadcasted_iota(jnp.int32, (c, LANES), 1)
        cols = slice(hh * DN_DK, (hh + 1) * DN_DK)
        per_dir = []
        for d in range(2):
            q_ref, k_ref, v_ref, gcol_ref, grow_ref = ins[d]
            part = half if d == 0 else 1 - half
            rows = slice(part * c, (part + 1) * c)
            gates = gcol_ref[bb, rows, :]
            g_lane = d * DN_HEADS + hh
            gcum = jnp.sum(jnp.where(lane == g_lane, gates, 0.0), axis=-1, keepdims=True)
            beta = jnp.sum(jnp.where(lane == 2 * DN_HEADS + g_lane, gates, 0.0), axis=-1, keepdims=True)
            grow = grow_ref[bb, part, g_lane:g_lane + 1, :]
            g_last = grow[:, c - 1:c] if d == 0 else grow[:, 0:1]
            per_dir.append((q_ref[bb, rows, cols], k_ref[bb, rows, cols], v_ref[bb, rows, cols],
                            gcum, beta, grow, g_last))
        (qf, kf, vf, gcf, bf, grf, glf), (qb, kb, vb, gcb, bb, grb, glb) = per_dir
        incl = jnp.logical_or(ri == cj, (ri > cj) == fwd_half)
        diff = jnp.where(fwd_half, gcf, gcb) - jnp.where(fwd_half[0:1], grf, grb)
        t["gamma"] = jnp.where(incl, jnp.exp(jnp.where(incl, diff, 0.0)), 0.0)
        kbf, kbb = kf * bf, kb * bb
        dec_f, dec_b = jnp.exp(gcf), jnp.exp(gcb)
        t["lhs16"] = _side_by_side(jnp.concatenate([kbf, qf], axis=0).astype(BF16),
                                   jnp.concatenate([kbb, qb], axis=0).astype(BF16))
        t["keys16"] = _block_diag16(kf, kb)
        t["rhs16"] = _block_diag16(_side_by_side(vf * bf, kbf * dec_f), _side_by_side(vb * bb, kbb * dec_b))
        wq_ref[sidx, n, c:, :DN_DK] = (qf * dec_f).astype(BF16)
        wq_ref[sidx, n, c:, DN_DK:] = (qb * dec_b).astype(BF16)
        k_dec = jnp.concatenate([kf * jnp.exp(glf - gcf), kb * jnp.exp(glb - gcb)], axis=0)
        m_ref[sidx, n, c:, :] = k_dec.T.astype(BF16)
        e_ref[sidx, n] = _side_by_side(jnp.broadcast_to(jnp.exp(glf), (SUBLANES, DN_DV)),
                                        jnp.broadcast_to(jnp.exp(glb), (SUBLANES, DN_DV)))

    def gram(t, n):
        t["kq"] = _dot_nt(t.pop("lhs16"), t.pop("keys16"))

    def split(t, n):
        ri, cj, fwd_half = grids()
        strict = jnp.logical_and(ri != cj, (ri > cj) == fwd_half)
        kq = t.pop("kq")
        gamma = t.pop("gamma")
        a = jnp.where(strict, kq[:c] * gamma, 0.0)
        m_ref[sidx, n, :c, :] = (kq[c:] * gamma).astype(BF16)
        a_diag = jnp.where((ri // INV_BASE) == (cj // INV_BASE), a, 0.0)
        t["a"] = a
        t["inv"] = (ri == cj).astype(F32) - a_diag
        t["pw"] = pair_dot(a_diag, a_diag)

    def series(t, n):
        both = pair_dot(jnp.concatenate([t["pw"], t["inv"]], axis=0), t["pw"])
        t["pw"] = both[:c]
        t["inv"] = t["inv"] + both[c:]

    def series_last(t, n):
        t["inv"] = t["inv"] + pair_dot(t["inv"], t.pop("pw"))

    def merge_a(blk):
        def fn(t, n):
            ri, cj, _ = grids()
            off = jnp.logical_and((ri // (2 * blk)) == (cj // (2 * blk)), (ri // blk) != (cj // blk))
            t["y"] = pair_dot(jnp.where(off, t["a"], 0.0), t["inv"])
        return fn

    def merge_b(t, n):
        t["inv"] = t["inv"] - pair_dot(t["inv"], t.pop("y"))

    def solve(t, n):
        t["uw"] = _dot(t.pop("inv").astype(BF16), t.pop("rhs16"))
        t.pop("a")

    def store(t, n):
        uw = t.pop("uw")
        for d in range(2):
            base = d * (DN_DV + DN_DK)
            u_ref[sidx, n, :, d * DN_DV:(d + 1) * DN_DV] = uw[:, base:base + DN_DV]
            wq_ref[sidx, n, :c, d * DN_DK:(d + 1) * DN_DK] = uw[:, base + DN_DV:base + DN_DV + DN_DK].astype(BF16)

    stages = [load, gram, split] + [series] * (int(math.log2(INV_BASE)) - 2) + [series_last]
    blk = INV_BASE
    while blk < c:
        stages += [merge_a(blk), merge_b]
        blk *= 2
    stages += [solve, store]
    return [each(fn) for fn in stages]


def _dn_serial_stages(outs, s_ref, slot, sidx, half):
    u_ref, wq_ref, m_ref, e_ref = slot
    c = DN_CHUNK
    st = [dict() for _ in range(SCAN_UNITS)]

    def each(fn):
        def run():
            for n, t in enumerate(st):
                fn(t, n)
        return run

    def state_dot(t, n):
        t["s"] = (s_ref[n], s_ref[SCAN_UNITS + n])
        t["ws"] = _dot(wq_ref[sidx, n], _block_diag16(*t["s"]))

    def value_dot(t, n):
        v_new = u_ref[sidx, n] - t["ws"][:c]
        t["mv"] = _dot(m_ref[sidx, n], _block_diag16(v_new[:, :DN_DV], v_new[:, DN_DV:]))

    def update(t, n):
        bb, hh = divmod(n, DN_HEADS)
        mv, ws, s = t.pop("mv"), t.pop("ws"), t.pop("s")
        gain = e_ref[sidx, n][0:1, :]
        for d in range(2):
            lanes = slice(d * DN_DV, (d + 1) * DN_DV)
            s_ref[d * SCAN_UNITS + n] = s[d] * gain[:, lanes] + mv[c:, lanes]
            part = half if d == 0 else 1 - half
            outs[d][bb, part * c:(part + 1) * c, hh * DN_DV:(hh + 1) * DN_DV] = ws[c:, lanes] + mv[:c, lanes]

    return [each(fn) for fn in (state_dot, value_dot, update)]


def _dn_scan_kernel(*refs):
    ins = (refs[0:5], refs[5:10])
    outs = refs[10:12]
    s_ref = refs[12]
    slot_sets = (refs[13:17], refs[17:21])
    g = pl.program_id(1)

    @pl.when(g == 0)
    def _():
        for ref in (s_ref,) + tuple(slot_sets[0]) + tuple(slot_sets[1]):
            ref[...] = jnp.zeros(ref.shape, ref.dtype)

    def step(written, read):
        loc = [_dn_local_stages(ins, written, half, half) for half in range(2)]
        ser = [stage for half in range(2) for stage in _dn_serial_stages(outs, s_ref, read, half, half)]
        gap = len(loc[0]) // len(ser)
        for n, stages in enumerate(zip(*loc)):
            if n % gap == 0 and n // gap < len(ser):
                ser[n // gap]()
            for stage in stages:
                stage()

    for parity in range(2):
        pl.when(g % 2 == parity)(functools.partial(step, slot_sets[parity], slot_sets[1 - parity]))


def _dn_scan(qkv, gcol, grow, n_lat):
    batch, n_tot, _ = qkv.shape
    hk = DN_HEADS * DN_DK
    pair = 2 * DN_CHUNK
    n_pairs = n_tot // pair
    lat_pairs = n_lat // pair
    ctx_pairs = n_pairs - lat_pairs
    assert n_lat % pair == 0 and n_tot % pair == 0 and batch % SCAN_BATCH == 0

    def fwd(p):
        return jnp.where(p < ctx_pairs, p + lat_pairs, p - ctx_pairs)

    def bwd(p):
        return n_pairs - 1 - p

    def p_in(g):
        return jnp.minimum(g, n_pairs - 1)

    def p_out(g):
        return jnp.maximum(g - 1, 0)

    def in_specs(where):
        return [pl.BlockSpec((SCAN_BATCH, pair, hk), lambda b, g, col=col: (b, where(p_in(g)), col))
                for col in range(3)] + [
            pl.BlockSpec((SCAN_BATCH, pair, LANES), lambda b, g: (b, where(p_in(g)), 0)),
            pl.BlockSpec((SCAN_BATCH, 2, LANES, LANES), lambda b, g: (b, where(p_in(g)), 0, 0)),
        ]

    out_spec = lambda where: pl.BlockSpec((SCAN_BATCH, pair, hk), lambda b, g: (b, where(p_out(g)), 0))
    out_shape = jax.ShapeDtypeStruct((batch, n_tot, hk), F32)
    slot = [pltpu.VMEM((2, SCAN_UNITS, DN_CHUNK, 2 * DN_DV), F32),
            pltpu.VMEM((2, SCAN_UNITS, 2 * DN_CHUNK, 2 * DN_DK), BF16),
            pltpu.VMEM((2, SCAN_UNITS, DN_CHUNK + DN_DK, 2 * DN_CHUNK), BF16),
            pltpu.VMEM((2, SCAN_UNITS, SUBLANES, 2 * DN_DV), F32)]
    per_dir = (qkv, qkv, qkv, gcol, grow)
    return pl.pallas_call(
        _dn_scan_kernel,
        grid=(batch // SCAN_BATCH, n_pairs + 1),
        in_specs=in_specs(fwd) + in_specs(bwd),
        out_specs=[out_spec(fwd), out_spec(bwd)],
        out_shape=[out_shape, out_shape],
        scratch_shapes=[pltpu.VMEM((2 * SCAN_UNITS, DN_DK, DN_DV), F32)] + slot + slot,
        compiler_params=_cparams(("arbitrary", "arbitrary"), 48),
        name="dn_scan",
    )(*per_dir, *per_dir)


def _sublayers_kernel(*refs, alpha, gated_norm, split_residual, n_lat_tiles):
    if gated_norm:
        y_ref, y2_ref, z_ref, ng_ref = refs[:4]
    else:
        y_ref = refs[0]
    (gt1_ref, sc2_ref, sh2_ref, gt2_ref, wo_ref, l1g_ref, l1b_ref,
     wg_ref, wu_ref, wd_ref, l2g_ref, l2b_ref, o_ref, yb_ref) = refs[-14:]
    if split_residual:
        x_ref, c_ref = refs[-16:-14]
        residual = jnp.where(pl.program_id(1) < n_lat_tiles, x_ref[0], c_ref[0])
    else:
        residual = refs[-15][0]
    if gated_norm:
        for h0 in range(0, y_ref.shape[2], DN_DV):
            yh = y_ref[0, :, h0:h0 + DN_DV] + y2_ref[0, :, h0:h0 + DN_DV]
            ms = jnp.mean(yh * yh, axis=-1, keepdims=True)
            yh = yh * lax.rsqrt(ms + RMS_EPS) * ng_ref[...]
            yb_ref[:, h0:h0 + DN_DV] = (yh * _silu(z_ref[0, :, h0:h0 + DN_DV])).astype(BF16)
        mixed = yb_ref[...]
    else:
        mixed = y_ref[0]
    x1 = _layernorm(alpha * residual + gt1_ref[0] * _dot(mixed, wo_ref[...]), l1g_ref[...], l1b_ref[...])
    h = (x1 * (1.0 + sc2_ref[0]) + sh2_ref[0]).astype(BF16)
    gate = _dot(h, wg_ref[...])
    up = _dot(h, wu_ref[...])
    act = (_silu(gate) * up).astype(BF16)
    o_ref[0] = _layernorm(alpha * x1 + gt2_ref[0] * _dot(act, wd_ref[...]), l2g_ref[...], l2b_ref[...])


def _sublayers(ys, residual, mod, w_out16, ln1, ffn16, ln2, alpha, n_lat, n_out, gated_norm):
    batch, _, d = residual[0].shape
    hv = ys[0].shape[2]
    split = len(residual) == 2
    tm = WIDE_ROW_TILE if not split and n_out % WIDE_ROW_TILE == 0 and n_lat % WIDE_ROW_TILE == 0 else ROW_TILE
    n_lat_tiles = n_lat // tm
    kern = functools.partial(_sublayers_kernel, alpha=alpha, gated_norm=gated_norm, split_residual=split,
                             n_lat_tiles=n_lat_tiles)
    tok = lambda width: pl.BlockSpec((1, tm, width), lambda b, t: (b, t, 0))
    res_specs = [tok(d)] if not split else [
        pl.BlockSpec((1, tm, d), lambda b, t: (b, jnp.minimum(t, n_lat_tiles - 1), 0)),
        pl.BlockSpec((1, tm, d), lambda b, t: (b, jnp.maximum(t - n_lat_tiles, 0), 0))]
    mod_spec = lambda idx: pl.BlockSpec((1, 1, d), lambda b, t: (jnp.where(t < n_lat_tiles, b, batch), 0, idx))
    resident = pl.BlockSpec(memory_space=pltpu.VMEM)
    vec = pl.BlockSpec((1, d), lambda b, t: (0, 0))
    return pl.pallas_call(
        kern,
        grid=(batch, n_out // tm),
        in_specs=([tok(hv)] * 3 + [pl.BlockSpec((1, ys[3].shape[1]), lambda b, t: (0, 0))]
                  if gated_norm else [tok(hv)]) + res_specs
        + [mod_spec(2), mod_spec(4), mod_spec(3), mod_spec(5), resident, vec, vec, resident, resident, resident,
           vec, vec],
        out_specs=tok(d),
        out_shape=jax.ShapeDtypeStruct((batch, n_out, d), F32),
        scratch_shapes=[pltpu.VMEM((tm, hv), BF16)],
        compiler_params=_cparams(("arbitrary", "arbitrary"), 56),
        name="sublayers",
    )(*ys, *residual, mod, mod, mod, mod, w_out16, *ln1, *ffn16, *ln2)


def _da_proj_kernel(x_ref, sc_l, sh_l, sc_c, sh_c, w_ref, cos_ref, sin_a_ref, sin_b_ref,
                    o_ref, xb_ref, *, n_lat, n_qk_tiles):
    j = pl.program_id(1)

    @pl.when(j == 0)
    def _():
        _modulate_into(xb_ref, x_ref[0, :n_lat, :], x_ref[0, n_lat:, :], sc_l, sh_l, sc_c, sh_c)

    blocks = _row_blocks(n_lat, x_ref.shape[1])

    def projector():
        w16 = w_ref[0].astype(BF16)
        return lambda r0, rows: _dot(xb_ref[r0:r0 + rows, :], w16)

    @pl.when(j < n_qk_tiles)
    def _():
        project = projector()
        scale = jnp.where(j < n_qk_tiles // 2, DA_DIM ** -0.5 * LOG2E, 1.0)
        quarter = DA_DIM // 4

        def rope(r0, rows, acc):
            cos, sin_a, sin_b = (t[r0:r0 + rows, :] for t in (cos_ref, sin_a_ref, sin_b_ref))
            for h0 in range(0, DA_COL_TILE, LANES):
                xh = acc[:, h0:h0 + LANES]
                rot = pltpu.roll(xh, quarter, 1) * sin_a + pltpu.roll(xh, LANES - quarter, 1) * sin_b
                o_ref[0, r0:r0 + rows, h0:h0 + LANES] = ((xh * cos + rot) * scale).astype(BF16)

        _pipelined(blocks, project, rope)

    @pl.when(j >= n_qk_tiles)
    def _():
        project = projector()
        for r0, rows in blocks:
            o_ref[0, r0:r0 + rows, :] = project(r0, rows).astype(BF16)


def _da_project(xc, mod, w_in, cos_t, sin_a, sin_b, n_lat):
    batch, n_tot, d = xc.shape
    n_cols = w_in.shape[2]
    n_tiles = n_cols // DA_COL_TILE
    n_qk_tiles = 2 * n_tiles // 3
    kern = functools.partial(_da_proj_kernel, n_lat=n_lat, n_qk_tiles=n_qk_tiles)
    table = pl.BlockSpec((n_tot, LANES), lambda b, j: (0, 0))
    return pl.pallas_call(
        kern,
        grid=(batch, n_tiles),
        in_specs=[pl.BlockSpec((1, n_tot, d), lambda b, j: (b, 0, 0))]
        + _mod_specs(batch, d, 1, 0)
        + [pl.BlockSpec((1, d, DA_COL_TILE), lambda b, j: (0, 0, j)), table, table, table],
        out_specs=pl.BlockSpec((1, n_tot, DA_COL_TILE), lambda b, j: (b, 0, j)),
        out_shape=jax.ShapeDtypeStruct((batch, n_tot, n_cols), BF16),
        scratch_shapes=[pltpu.VMEM((n_tot, d), BF16)],
        compiler_params=_cparams(("arbitrary", "arbitrary"), 56),
        name="da_project",
    )(xc, mod, mod, mod, mod, w_in, cos_t, sin_a, sin_b)


def _da_attn_kernel(q_ref, k_ref, v_ref, lam_ref, g_ref, o_ref, *, lambda_init):
    lam = lam_ref[...]
    lam_val = (jnp.exp(jnp.sum(lam[0:1] * lam[1:2], axis=-1, keepdims=True))
               - jnp.exp(jnp.sum(lam[2:3] * lam[3:4], axis=-1, keepdims=True)) + lambda_init)
    k = k_ref[0]
    n_sub = q_ref.shape[1] // ATT_SUB_ROWS
    lane = lax.broadcasted_iota(jnp.int32, (ATT_SUB_ROWS, 2 * DA_DIM), 1)

    def scores(r):
        q = q_ref[0, r * ATT_SUB_ROWS:(r + 1) * ATT_SUB_ROWS, :]
        zero = jnp.zeros_like(q)
        return [_dot_nt(jnp.where((lane < DA_DIM) if comp == 0 else (lane >= DA_DIM), q, zero), k)
                for comp in range(2)]

    def weights(s):
        e0, e1 = (jnp.exp2(sc - jnp.max(sc, axis=-1, keepdims=True)) for sc in s)
        l0 = jnp.sum(e0, axis=-1, keepdims=True)
        l1 = jnp.sum(e1, axis=-1, keepdims=True)
        return (e0 - e1 * (lam_val * l0 / l1)).astype(BF16), 1.0 / l0

    def values(r, a16, inv_l0):
        o = _dot(a16, v_ref[0]) * inv_l0
        ms = jnp.mean(o * o, axis=-1, keepdims=True)
        o_ref[0, r * ATT_SUB_ROWS:(r + 1) * ATT_SUB_ROWS, :] = (
            o * lax.rsqrt(ms + RMS_EPS) * g_ref[...] * (1.0 - lambda_init)).astype(BF16)

    pending = [scores(r) for r in range(min(ATT_LOOKAHEAD, n_sub))]
    for r in range(n_sub):
        if r + ATT_LOOKAHEAD < n_sub:
            pending.append(scores(r + ATT_LOOKAHEAD))
        values(r, *weights(pending.pop(0)))


def _da_attention(qkv16, lam, subln_g, lambda_init, n_lat):
    batch, n_tot, _ = qkv16.shape
    hw = 2 * DA_DIM
    kern = functools.partial(_da_attn_kernel, lambda_init=lambda_init)
    tq = ATT_Q_TILE if n_lat % ATT_Q_TILE == 0 else ROW_TILE
    return pl.pallas_call(
        kern,
        grid=(batch, DA_HEADS, n_lat // tq),
        in_specs=[
            pl.BlockSpec((1, tq, hw), lambda b, h, i: (b, i, h)),
            pl.BlockSpec((1, n_tot, hw), lambda b, h, i: (b, 0, DA_HEADS + h)),
            pl.BlockSpec((1, n_tot, hw), lambda b, h, i: (b, 0, 2 * DA_HEADS + h)),
            pl.BlockSpec(lam.shape, lambda b, h, i: (0, 0)),
            pl.BlockSpec((1, hw), lambda b, h, i: (0, 0)),
        ],
        out_specs=pl.BlockSpec((1, tq, hw), lambda b, h, i: (b, i, h)),
        out_shape=jax.ShapeDtypeStruct((batch, n_lat, DA_HEADS * hw), BF16),
        compiler_params=_cparams(("arbitrary", "arbitrary", "arbitrary"), 48),
        name="da_attention",
    )(qkv16, qkv16, qkv16, lam, subln_g)


def _rope_tables(n_lat, n_tot):
    quarter = DA_DIM // 4
    inv_freq = ROPE_BASE ** (-jnp.arange(quarter, dtype=F32) / quarter)
    rows = n_lat // GRID_W
    row = jnp.repeat(jnp.arange(rows, dtype=F32), GRID_W)
    col = jnp.tile(jnp.arange(GRID_W, dtype=F32), rows)
    ang_r = row[:, None] * inv_freq
    ang_c = col[:, None] * inv_freq
    ang = jnp.concatenate([ang_r, ang_r, ang_c, ang_c], axis=-1)
    reps = LANES // DA_DIM
    cos = jnp.tile(jnp.cos(ang), (1, reps))
    sin = jnp.tile(jnp.sin(ang), (1, reps))
    upper = (jnp.arange(LANES) % (2 * quarter)) >= quarter
    sin_a = jnp.where(upper, sin, 0.0)
    sin_b = jnp.where(upper, 0.0, -sin)
    n_ctx = n_tot - n_lat
    ext = lambda t, fill: jnp.concatenate([t, jnp.full((n_ctx, LANES), fill, F32)], axis=0)
    return ext(cos, 1.0), ext(sin_a, 0.0), ext(sin_b, 0.0)


def kernel(x, c, ctx, c_ctx, ada_w, ada_b, ln1_g, ln1_b, ln2_g, ln2_b, ffn_w_gate, ffn_w_up, ffn_w_down,
           dn_w_in, dn_conv, dn_a_log, dn_dt_bias, dn_norm_g, dn_w_out, da_w_in, da_lambda, da_subln_g,
           da_w_out):
    batch, n_lat, d = x.shape
    n_ctx = ctx.shape[1]
    n_tot = n_lat + n_ctx
    depth = ada_w.shape[0]
    assert depth == 2, "layer 0 is gated DeltaNet, layer 1 (the last) differential attention"
    assert batch < MOD_ROWS and n_lat % ROW_TILE == 0 and n_ctx % ROW_TILE == 0
    alpha = (2.0 * depth) ** 0.25
    hk = DN_HEADS * DN_DK

    cs = jnp.concatenate([c, c_ctx[None, :], jnp.zeros((MOD_ROWS - batch - 1, d), F32)], axis=0)
    mods = _ada_modulation(cs, ada_w, ada_b).reshape(depth, MOD_ROWS, 1, 6 * d)

    mod = mods[0]
    gate_cols = dn_w_in[0][:, 4 * hk:]
    w_gates = jnp.concatenate([gate_cols, jnp.zeros((d, LANES - gate_cols.shape[1]), F32)], axis=1)
    lane_pad = lambda v: jnp.concatenate([v.reshape(1, -1), jnp.zeros((1, LANES - v.size), F32)], axis=1)
    qkv, z, gcol, grow = _dn_project(x, ctx, mod, dn_w_in, w_gates, dn_conv[0],
                                     lane_pad(dn_a_log[0]), lane_pad(dn_dt_bias[0]))
    o_fwd, o_bwd = _dn_scan(qkv, gcol, grow, n_lat)
    vec = lambda v: v.reshape(1, -1)
    ffn_all16 = tuple(_to_bf16(w) for w in (ffn_w_gate, ffn_w_up, ffn_w_down))
    ffn16 = lambda i: tuple(w[i] for w in ffn_all16)
    xc = _sublayers((o_fwd, o_bwd, z, vec(dn_norm_g[0])), (x, ctx), mod, dn_w_out[0].astype(BF16),
                    (vec(ln1_g[0]), vec(ln1_b[0])), ffn16(0), (vec(ln2_g[0]), vec(ln2_b[0])),
                    alpha, n_lat, n_tot, True)

    mod = mods[1]
    lambda_init = 0.8 - 0.6 * math.exp(-0.3 * 1)
    cos_t, sin_a, sin_b = _rope_tables(n_lat, n_tot)
    qkv16 = _da_project(xc, mod, da_w_in, cos_t, sin_a, sin_b, n_lat)
    y = _da_attention(qkv16, da_lambda[0], da_subln_g[0].reshape(1, -1), lambda_init, n_lat)
    return _sublayers((y,), (xc,), mod, da_w_out[0].astype(BF16), (vec(ln1_g[1]), vec(ln1_b[1])), ffn16(1),
                      (vec(ln2_g[1]), vec(ln2_b[1])), alpha, n_lat, n_lat, False)
```

```python
import functools
import math

import jax
import jax.numpy as jnp
from jax import lax
from jax.experimental import pallas as pl
from jax.experimental.pallas import tpu as pltpu

F32 = jnp.float32
BF16 = jnp.bfloat16

DN_HEADS = 8
DN_DK = 128
DN_DV = 128
DN_CONV = 5
DN_CHUNK = 64
DA_HEADS = 8
DA_DIM = 64
GRID_W = 64
ROPE_BASE = 10000.0
LN_EPS = 1e-5
RMS_EPS = 1e-6

LANES = 128
SUBLANES = 8
MOD_ROWS = 16
ROW_TILE = 256
WIDE_ROW_TILE = 512
PROJ_ROW_BLOCK = 256
COL_TILE = 256
DA_COL_TILE = 512
ADA_COL_TILE = 1536
ATT_Q_TILE = 2048
ATT_SUB_ROWS = 256
ATT_LOOKAHEAD = 2
LOG2E = math.log2(math.e)


def _cparams(dims, vmem_mb):
    return pltpu.CompilerParams(dimension_semantics=dims, vmem_limit_bytes=vmem_mb * 1024 * 1024)


def _silu(x):
    return x * jax.nn.sigmoid(x)


def _dot(a, b):
    return jnp.dot(a, b, preferred_element_type=F32)


def _dot_nt(a, b):
    return lax.dot_general(a, b, (((1,), (1,)), ((), ())), preferred_element_type=F32)


def _layernorm(x, g, b):
    mu = jnp.mean(x, axis=-1, keepdims=True)
    xc = x - mu
    var = jnp.mean(xc * xc, axis=-1, keepdims=True)
    return xc * lax.rsqrt(var + LN_EPS) * g + b


def _cast_kernel(w_ref, o_ref):
    o_ref[...] = w_ref[...].astype(BF16)


def _to_bf16(w, layer):
    _, rows, cols = w.shape
    tr = rows // 2
    return pl.pallas_call(
        _cast_kernel,
        grid=(rows // tr,),
        in_specs=[pl.BlockSpec((1, tr, cols), lambda r: (layer, r, 0))],
        out_specs=pl.BlockSpec((1, tr, cols), lambda r: (0, r, 0)),
        out_shape=jax.ShapeDtypeStruct((1, rows, cols), BF16),
        compiler_params=_cparams(("arbitrary",), 40),
        name="to_bf16",
    )(w).reshape(rows, cols)


def _ada_kernel(cs_ref, w_ref, b_ref, o_ref):
    s = _silu(cs_ref[...]).astype(BF16)
    o_ref[0] = _dot(s, w_ref[0].astype(BF16)) + b_ref[0]


def _ada_modulation(cs, ada_w, ada_b):
    depth, d, n = ada_w.shape
    tn = ADA_COL_TILE if n % ADA_COL_TILE == 0 else n
    return pl.pallas_call(
        _ada_kernel,
        grid=(depth, n // tn),
        in_specs=[
            pl.BlockSpec((MOD_ROWS, d), lambda i, j: (0, 0)),
            pl.BlockSpec((1, d, tn), lambda i, j: (i, 0, j)),
            pl.BlockSpec((1, 1, tn), lambda i, j: (i, 0, j)),
        ],
        out_specs=pl.BlockSpec((1, MOD_ROWS, tn), lambda i, j: (i, 0, j)),
        out_shape=jax.ShapeDtypeStruct((depth, MOD_ROWS, n), F32),
        compiler_params=_cparams(("arbitrary", "arbitrary"), 40),
        name="ada_modulation",
    )(cs, ada_w, ada_b.reshape(depth, 1, n))


def _modulate_into(xb_ref, lat, ctx, sc_l, sh_l, sc_c, sh_c):
    n_lat = lat.shape[0]
    xb_ref[:n_lat, :] = (lat * (1.0 + sc_l[0]) + sh_l[0]).astype(BF16)
    xb_ref[n_lat:, :] = (ctx * (1.0 + sc_c[0]) + sh_c[0]).astype(BF16)


def _row_blocks(n_lat, n_tot):
    blocks = []
    for seg0, seg in ((0, n_lat), (n_lat, n_tot - n_lat)):
        size = PROJ_ROW_BLOCK if seg % PROJ_ROW_BLOCK == 0 else seg
        blocks += [(seg0 + r, size) for r in range(0, seg, size)]
    return blocks


def _pipelined(blocks, produce, consume):
    pending = None
    for blk in blocks:
        cur = produce(*blk)
        if pending is not None:
            consume(*pending)
        pending = blk + (cur,)
    consume(*pending)


def _mod_specs(batch, d, sc_idx, sh_idx):
    return [
        pl.BlockSpec((1, 1, d), lambda b, j: (b, 0, sc_idx)),
        pl.BlockSpec((1, 1, d), lambda b, j: (b, 0, sh_idx)),
        pl.BlockSpec((1, 1, d), lambda b, j: (batch, 0, sc_idx)),
        pl.BlockSpec((1, 1, d), lambda b, j: (batch, 0, sh_idx)),
    ]


def _dn_proj_kernel(x_ref, c_ref, sc_l, sh_l, sc_c, sh_c, w_ref, wg_ref, conv_ref, alog_ref, dt_ref,
                    qkv_ref, z_ref, gcol_ref, grow_ref, xb_ref, pad_ref,
                    *, n_lat, n_tot, n_qkv_tiles, n_z_tiles):
    j = pl.program_id(1)
    n_ctx = n_tot - n_lat
    halo = SUBLANES
    pad = (DN_CONV - 1) // 2
    lat0 = halo
    ctx0 = 2 * halo + n_lat

    @pl.when(j == 0)
    def _():
        _modulate_into(xb_ref, x_ref[0], c_ref[0], sc_l, sh_l, sc_c, sh_c)

    blocks = _row_blocks(n_lat, n_tot)

    def projector():
        w16 = w_ref[0].astype(BF16)
        return lambda r0, rows: _dot(xb_ref[r0:r0 + rows, :], w16)

    @pl.when(j < n_qkv_tiles)
    def _():
        project = projector()
        zeros = jnp.zeros((halo, COL_TILE), F32)
        pad_ref[0:halo, :] = zeros
        pad_ref[lat0 + n_lat:ctx0, :] = zeros
        pad_ref[ctx0 + n_ctx:ctx0 + n_ctx + halo, :] = zeros
        cw = conv_ref[...]
        n_qk_tiles = 2 * n_qkv_tiles // 3
        is_q = j < n_qk_tiles // 2
        is_qk = j < n_qk_tiles

        def padded(r0):
            return r0 + (lat0 if r0 < n_lat else ctx0 - n_lat)

        def stage(r0, rows):
            pad_ref[padded(r0):padded(r0) + rows, :] = project(r0, rows)

        def conv_norm(r0, rows, _):
            base = padded(r0)
            ext = pad_ref[base - halo:base + rows + halo, :]
            y = None
            for t in range(DN_CONV):
                shifted = ext if t == pad else pltpu.roll(ext, (pad - t) % ext.shape[0], 0)
                term = shifted[halo:halo + rows, :] * cw[t:t + 1, :]
                y = term if y is None else y + term
            y = _silu(y)
            for h0 in range(0, COL_TILE, DN_DK):
                yh = y[:, h0:h0 + DN_DK]
                ss = jnp.sum(yh * yh, axis=-1, keepdims=True)
                fac = lax.rsqrt(ss + RMS_EPS)
                fac = jnp.where(is_q, fac * (DN_DK ** -0.5), fac)
                fac = jnp.where(is_qk, fac, jnp.ones_like(fac))
                qkv_ref[0, r0:r0 + rows, h0:h0 + DN_DK] = yh * fac

        _pipelined(blocks, stage, conv_norm)

    @pl.when(jnp.logical_and(j >= n_qkv_tiles, j < n_qkv_tiles + n_z_tiles))
    def _():
        project = projector()
        for r0, rows in blocks:
            z_ref[0, r0:r0 + rows, :] = project(r0, rows)

    @pl.when(j == n_qkv_tiles + n_z_tiles)
    def _():
        t = _dot(xb_ref[...], wg_ref[...].astype(BF16))
        lane = lax.broadcasted_iota(jnp.int32, t.shape, 1)
        x = t + dt_ref[...]
        softplus = jnp.maximum(x, 0.0) + jnp.log1p(jnp.exp(-jnp.abs(x)))
        decay = -jnp.exp(alog_ref[...]) * softplus
        gates = jnp.where(lane < 2 * DN_HEADS, decay, jax.nn.sigmoid(t))
        ri = lax.broadcasted_iota(jnp.int32, (DN_CHUNK, DN_CHUNK), 0)
        ci = lax.broadcasted_iota(jnp.int32, (DN_CHUNK, DN_CHUNK), 1)
        tri = jnp.concatenate([ri >= ci, ri <= ci], axis=0).astype(BF16)
        hi = gates.astype(BF16)
        rest = gates - hi.astype(F32)
        mid = rest.astype(BF16)
        low = (rest - mid.astype(F32)).astype(BF16)
        parts = jnp.concatenate([hi, mid, low], axis=1)
        lane_c = lax.broadcasted_iota(jnp.int32, (DN_CHUNK, LANES), 1)
        for c in range(n_tot // DN_CHUNK):
            gc = gates[c * DN_CHUNK:(c + 1) * DN_CHUNK]
            cum3 = _dot(tri, parts[c * DN_CHUNK:(c + 1) * DN_CHUNK])
            cum = cum3[:, :LANES] + cum3[:, LANES:2 * LANES] + cum3[:, 2 * LANES:]
            col = jnp.where(lane_c < DN_HEADS, cum[:DN_CHUNK],
                            jnp.where(lane_c < 2 * DN_HEADS, cum[DN_CHUNK:], gc))
            gcol_ref[0, c * DN_CHUNK:(c + 1) * DN_CHUNK, :] = col
            grow_ref[0, c] = jnp.concatenate([col, col], axis=0).T


def _dn_project(x, ctx, mod, w_in, w_gates, conv_w, alog_row, dt_row):
    batch, n_lat, d = x.shape
    n_tot = n_lat + ctx.shape[1]
    hk = DN_HEADS * DN_DK
    n_qkv_tiles = 3 * hk // COL_TILE
    n_z_tiles = hk // COL_TILE
    n_tiles = n_qkv_tiles + n_z_tiles + 1
    nc = n_tot // DN_CHUNK
    kern = functools.partial(_dn_proj_kernel, n_lat=n_lat, n_tot=n_tot,
                             n_qkv_tiles=n_qkv_tiles, n_z_tiles=n_z_tiles)
    return pl.pallas_call(
        kern,
        grid=(batch, n_tiles),
        in_specs=[pl.BlockSpec((1, n_lat, d), lambda b, j: (b, 0, 0)),
                  pl.BlockSpec((1, n_tot - n_lat, d), lambda b, j: (b, 0, 0))]
        + _mod_specs(batch, d, 1, 0)
        + [
            pl.BlockSpec((1, d, COL_TILE), lambda b, j: (0, 0, jnp.minimum(j, n_tiles - 2))),
            pl.BlockSpec((d, LANES), lambda b, j: (0, 0)),
            pl.BlockSpec((DN_CONV, COL_TILE), lambda b, j: (0, jnp.minimum(j, n_qkv_tiles - 1))),
            pl.BlockSpec((1, LANES), lambda b, j: (0, 0)),
            pl.BlockSpec((1, LANES), lambda b, j: (0, 0)),
        ],
        out_specs=[
            pl.BlockSpec((1, n_tot, COL_TILE), lambda b, j: (b, 0, jnp.minimum(j, n_qkv_tiles - 1))),
            pl.BlockSpec((1, n_tot, COL_TILE),
                         lambda b, j: (b, 0, jnp.clip(j - n_qkv_tiles, 0, n_z_tiles - 1))),
            pl.BlockSpec((1, n_tot, LANES), lambda b, j: (b, 0, 0)),
            pl.BlockSpec((1, nc, LANES, LANES), lambda b, j: (b, 0, 0, 0)),
        ],
        out_shape=[
            jax.ShapeDtypeStruct((batch, n_tot, 3 * hk), F32),
            jax.ShapeDtypeStruct((batch, n_tot, hk), F32),
            jax.ShapeDtypeStruct((batch, n_tot, LANES), F32),
            jax.ShapeDtypeStruct((batch, nc, LANES, LANES), F32),
        ],
        scratch_shapes=[
            pltpu.VMEM((n_tot, d), BF16),
            pltpu.VMEM((n_tot + 3 * SUBLANES, COL_TILE), F32),
        ],
        compiler_params=_cparams(("arbitrary", "arbitrary"), 56),
        name="dn_project",
    )(x, ctx, mod, mod, mod, mod, w_in, w_gates, conv_w, alog_row, dt_row)


INV_BASE = 16
SCAN_BATCH = 2
SCAN_UNITS = SCAN_BATCH * DN_HEADS


def _side_by_side(fwd, bwd):
    return jnp.concatenate([fwd, bwd], axis=1)


def _block_diag16(top, bottom):
    top, bottom = top.astype(BF16), bottom.astype(BF16)
    return jnp.concatenate([_side_by_side(top, jnp.zeros((top.shape[0], bottom.shape[1]), BF16)),
                            _side_by_side(jnp.zeros((bottom.shape[0], top.shape[1]), BF16), bottom)], axis=0)


def _dn_local_stages(ins, slot, sidx, half):
    u_ref, wq_ref, m_ref, e_ref = slot
    c = DN_CHUNK
    st = [dict() for _ in range(SCAN_UNITS)]

    def each(fn):
        def run():
            for n, t in enumerate(st):
                fn(t, n)
        return run

    def grids():
        ri = lax.broadcasted_iota(jnp.int32, (c, 2 * c), 0)
        lane = lax.broadcasted_iota(jnp.int32, (c, 2 * c), 1)
        return ri, lane & (c - 1), lane < c

    def pair_dot(x, y):
        _, _, fwd_half = grids()
        rhs = jnp.concatenate([jnp.where(fwd_half, y, 0.0).astype(BF16),
                               jnp.where(fwd_half, 0.0, y).astype(BF16)], axis=0)
        return _dot(x.astype(BF16), rhs)

    def load(t, n):
        bb, hh = divmod(n, DN_HEADS)
        ri, cj, fwd_half = grids()
        lane = lax.broadcasted_iota(jnp.int32, (c, LANES), 1)
        cols = slice(hh * DN_DK, (hh + 1) * DN_DK)
        per_dir = []
        for d in range(2):
            q_ref, k_ref, v_ref, gcol_ref, grow_ref = ins[d]
            part = half if d == 0 else 1 - half
            rows = slice(part * c, (part + 1) * c)
            gates = gcol_ref[bb, rows, :]
            g_lane = d * DN_HEADS + hh
            gcum = jnp.sum(jnp.where(lane == g_lane, gates, 0.0), axis=-1, keepdims=True)
            beta = jnp.sum(jnp.where(lane == 2 * DN_HEADS + g_lane, gates, 0.0), axis=-1, keepdims=True)
            grow = grow_ref[bb, part, g_lane:g_lane + 1, :]
            g_last = grow[:, c - 1:c] if d == 0 else grow[:, 0:1]
            per_dir.append((q_ref[bb, rows, cols], k_ref[bb, rows, cols], v_ref[bb, rows, cols],
                            gcum, beta, grow, g_last))
        (qf, kf, vf, gcf, bf, grf, glf), (qb, kb, vb, gcb, bb, grb, glb) = per_dir
        incl = jnp.logical_or(ri == cj, (ri > cj) == fwd_half)
        diff = jnp.where(fwd_half, gcf, gcb) - jnp.where(fwd_half[0:1], grf, grb)
        t["gamma"] = jnp.where(incl, jnp.exp(jnp.where(incl, diff, 0.0)), 0.0)
        kbf, kbb = kf * bf, kb * bb
        dec_f, dec_b = jnp.exp(gcf), jnp.exp(gcb)
        t["lhs16"] = _side_by_side(jnp.concatenate([kbf, qf], axis=0).astype(BF16),
                                   jnp.concatenate([kbb, qb], axis=0).astype(BF16))
        t["keys16"] = _block_diag16(kf, kb)
        t["rhs16"] = _block_diag16(_side_by_side(vf * bf, kbf * dec_f), _side_by_side(vb * bb, kbb * dec_b))
        wq_ref[sidx, n, c:, :DN_DK] = (qf * dec_f).astype(BF16)
        wq_ref[sidx, n, c:, DN_DK:] = (qb * dec_b).astype(BF16)
        k_dec = jnp.concatenate([kf * jnp.exp(glf - gcf), kb * jnp.exp(glb - gcb)], axis=0)
        m_ref[sidx, n, c:, :] = k_dec.T.astype(BF16)
        e_ref[sidx, n] = _side_by_side(jnp.broadcast_to(jnp.exp(glf), (SUBLANES, DN_DV)),
                                        jnp.broadcast_to(jnp.exp(glb), (SUBLANES, DN_DV)))

    def gram(t, n):
        t["kq"] = _dot_nt(t.pop("lhs16"), t.pop("keys16"))

    def split(t, n):
        ri, cj, fwd_half = grids()
        strict = jnp.logical_and(ri != cj, (ri > cj) == fwd_half)
        kq = t.pop("kq")
        gamma = t.pop("gamma")
        a = jnp.where(strict, kq[:c] * gamma, 0.0)
        m_ref[sidx, n, :c, :] = (kq[c:] * gamma).astype(BF16)
        a_diag = jnp.where((ri // INV_BASE) == (cj // INV_BASE), a, 0.0)
        t["a"] = a
        t["inv"] = (ri == cj).astype(F32) - a_diag
        t["pw"] = pair_dot(a_diag, a_diag)

    def series(t, n):
        both = pair_dot(jnp.concatenate([t["pw"], t["inv"]], axis=0), t["pw"])
        t["pw"] = both[:c]
        t["inv"] = t["inv"] + both[c:]

    def series_last(t, n):
        t["inv"] = t["inv"] + pair_dot(t["inv"], t.pop("pw"))

    def merge_a(blk):
        def fn(t, n):
            ri, cj, _ = grids()
            off = jnp.logical_and((ri // (2 * blk)) == (cj // (2 * blk)), (ri // blk) != (cj // blk))
            t["y"] = pair_dot(jnp.where(off, t["a"], 0.0), t["inv"])
        return fn

    def merge_b(t, n):
        t["inv"] = t["inv"] - pair_dot(t["inv"], t.pop("y"))

    def solve(t, n):
        t["uw"] = _dot(t.pop("inv").astype(BF16), t.pop("rhs16"))
        t.pop("a")

    def store(t, n):
        uw = t.pop("uw")
        for d in range(2):
            base = d * (DN_DV + DN_DK)
            u_ref[sidx, n, :, d * DN_DV:(d + 1) * DN_DV] = uw[:, base:base + DN_DV]
            wq_ref[sidx, n, :c, d * DN_DK:(d + 1) * DN_DK] = uw[:, base + DN_DV:base + DN_DV + DN_DK].astype(BF16)

    stages = [load, gram, split] + [series] * (int(math.log2(INV_BASE)) - 2) + [series_last]
    blk = INV_BASE
    while blk < c:
        stages += [merge_a(blk), merge_b]
        blk *= 2
    stages += [solve, store]
    return [each(fn) for fn in stages]


def _dn_serial_stages(outs, s_ref, slot, sidx, half):
    u_ref, wq_ref, m_ref, e_ref = slot
    c = DN_CHUNK
    st = [dict() for _ in range(SCAN_UNITS)]

    def each(fn):
        def run():
            for n, t in enumerate(st):
                fn(t, n)
        return run

    def state_dot(t, n):
        t["s"] = (s_ref[n], s_ref[SCAN_UNITS + n])
        t["ws"] = _dot(wq_ref[sidx, n], _block_diag16(*t["s"]))

    def value_dot(t, n):
        v_new = u_ref[sidx, n] - t["ws"][:c]
        t["mv"] = _dot(m_ref[sidx, n], _block_diag16(v_new[:, :DN_DV], v_new[:, DN_DV:]))

    def update(t, n):
        bb, hh = divmod(n, DN_HEADS)
        mv, ws, s = t.pop("mv"), t.pop("ws"), t.pop("s")
        gain = e_ref[sidx, n][0:1, :]
        for d in range(2):
            lanes = slice(d * DN_DV, (d + 1) * DN_DV)
            s_ref[d * SCAN_UNITS + n] = s[d] * gain[:, lanes] + mv[c:, lanes]
            part = half if d == 0 else 1 - half
            outs[d][bb, part * c:(part + 1) * c, hh * DN_DV:(hh + 1) * DN_DV] = ws[c:, lanes] + mv[:c, lanes]

    return [each(fn) for fn in (state_dot, value_dot, update)]


def _dn_scan_kernel(*refs):
    ins = (refs[0:5], refs[5:10])
    outs = refs[10:12]
    s_ref = refs[12]
    slot_sets = (refs[13:17], refs[17:21])
    g = pl.program_id(1)

    @pl.when(g == 0)
    def _():
        for ref in (s_ref,) + tuple(slot_sets[0]) + tuple(slot_sets[1]):
            ref[...] = jnp.zeros(ref.shape, ref.dtype)

    def step(written, read):
        loc = [_dn_local_stages(ins, written, half, half) for half in range(2)]
        ser = [stage for half in range(2) for stage in _dn_serial_stages(outs, s_ref, read, half, half)]
        gap = len(loc[0]) // len(ser)
        for n, stages in enumerate(zip(*loc)):
            if n % gap == 0 and n // gap < len(ser):
                ser[n // gap]()
            for stage in stages:
                stage()

    for parity in range(2):
        pl.when(g % 2 == parity)(functools.partial(step, slot_sets[parity], slot_sets[1 - parity]))


def _dn_scan(qkv, gcol, grow, n_lat):
    batch, n_tot, _ = qkv.shape
    hk = DN_HEADS * DN_DK
    pair = 2 * DN_CHUNK
    n_pairs = n_tot // pair
    lat_pairs = n_lat // pair
    ctx_pairs = n_pairs - lat_pairs
    assert n_lat % pair == 0 and n_tot % pair == 0 and batch % SCAN_BATCH == 0

    def fwd(p):
        return jnp.where(p < ctx_pairs, p + lat_pairs, p - ctx_pairs)

    def bwd(p):
        return n_pairs - 1 - p

    def p_in(g):
        return jnp.minimum(g, n_pairs - 1)

    def p_out(g):
        return jnp.maximum(g - 1, 0)

    def in_specs(where):
        return [pl.BlockSpec((SCAN_BATCH, pair, hk), lambda b, g, col=col: (b, where(p_in(g)), col))
                for col in range(3)] + [
            pl.BlockSpec((SCAN_BATCH, pair, LANES), lambda b, g: (b, where(p_in(g)), 0)),
            pl.BlockSpec((SCAN_BATCH, 2, LANES, LANES), lambda b, g: (b, where(p_in(g)), 0, 0)),
        ]

    out_spec = lambda where: pl.BlockSpec((SCAN_BATCH, pair, hk), lambda b, g: (b, where(p_out(g)), 0))
    out_shape = jax.ShapeDtypeStruct((batch, n_tot, hk), F32)
    slot = [pltpu.VMEM((2, SCAN_UNITS, DN_CHUNK, 2 * DN_DV), F32),
            pltpu.VMEM((2, SCAN_UNITS, 2 * DN_CHUNK, 2 * DN_DK), BF16),
            pltpu.VMEM((2, SCAN_UNITS, DN_CHUNK + DN_DK, 2 * DN_CHUNK), BF16),
            pltpu.VMEM((2, SCAN_UNITS, SUBLANES, 2 * DN_DV), F32)]
    per_dir = (qkv, qkv, qkv, gcol, grow)
    return pl.pallas_call(
        _dn_scan_kernel,
        grid=(batch // SCAN_BATCH, n_pairs + 1),
        in_specs=in_specs(fwd) + in_specs(bwd),
        out_specs=[out_spec(fwd), out_spec(bwd)],
        out_shape=[out_shape, out_shape],
        scratch_shapes=[pltpu.VMEM((2 * SCAN_UNITS, DN_DK, DN_DV), F32)] + slot + slot,
        compiler_params=_cparams(("arbitrary", "arbitrary"), 48),
        name="dn_scan",
    )(*per_dir, *per_dir)


def _sublayers_kernel(*refs, alpha, gated_norm, split_residual, n_lat_tiles):
    if gated_norm:
        y_ref, y2_ref, z_ref, ng_ref = refs[:4]
    else:
        y_ref = refs[0]
    (gt1_ref, sc2_ref, sh2_ref, gt2_ref, wo_ref, l1g_ref, l1b_ref,
     wg_ref, wu_ref, wd_ref, l2g_ref, l2b_ref, o_ref, yb_ref) = refs[-14:]
    if split_residual:
        x_ref, c_ref = refs[-16:-14]
        residual = jnp.where(pl.program_id(1) < n_lat_tiles, x_ref[0], c_ref[0])
    else:
        residual = refs[-15][0]
    if gated_norm:
        for h0 in range(0, y_ref.shape[2], DN_DV):
            yh = y_ref[0, :, h0:h0 + DN_DV] + y2_ref[0, :, h0:h0 + DN_DV]
            ms = jnp.mean(yh * yh, axis=-1, keepdims=True)
            yh = yh * lax.rsqrt(ms + RMS_EPS) * ng_ref[...]
            yb_ref[:, h0:h0 + DN_DV] = (yh * _silu(z_ref[0, :, h0:h0 + DN_DV])).astype(BF16)
        mixed = yb_ref[...]
    else:
        mixed = y_ref[0]
    x1 = _layernorm(alpha * residual + gt1_ref[0] * _dot(mixed, wo_ref[...]), l1g_ref[...], l1b_ref[...])
    h = (x1 * (1.0 + sc2_ref[0]) + sh2_ref[0]).astype(BF16)
    gate = _dot(h, wg_ref[...])
    up = _dot(h, wu_ref[...])
    act = (_silu(gate) * up).astype(BF16)
    o_ref[0] = _layernorm(alpha * x1 + gt2_ref[0] * _dot(act, wd_ref[...]), l2g_ref[...], l2b_ref[...])


def _sublayers(ys, residual, mod, w_out16, ln1, ffn16, ln2, alpha, n_lat, n_out, gated_norm):
    batch, _, d = residual[0].shape
    hv = ys[0].shape[2]
    split = len(residual) == 2
    tm = WIDE_ROW_TILE if not split and n_out % WIDE_ROW_TILE == 0 and n_lat % WIDE_ROW_TILE == 0 else ROW_TILE
    n_lat_tiles = n_lat // tm
    kern = functools.partial(_sublayers_kernel, alpha=alpha, gated_norm=gated_norm, split_residual=split,
                             n_lat_tiles=n_lat_tiles)
    tok = lambda width: pl.BlockSpec((1, tm, width), lambda b, t: (b, t, 0))
    res_specs = [tok(d)] if not split else [
        pl.BlockSpec((1, tm, d), lambda b, t: (b, jnp.minimum(t, n_lat_tiles - 1), 0)),
        pl.BlockSpec((1, tm, d), lambda b, t: (b, jnp.maximum(t - n_lat_tiles, 0), 0))]
    mod_spec = lambda idx: pl.BlockSpec((1, 1, d), lambda b, t: (jnp.where(t < n_lat_tiles, b, batch), 0, idx))
    resident = pl.BlockSpec(memory_space=pltpu.VMEM)
    vec = pl.BlockSpec((1, d), lambda b, t: (0, 0))
    return pl.pallas_call(
        kern,
        grid=(batch, n_out // tm),
        in_specs=([tok(hv)] * 3 + [pl.BlockSpec((1, ys[3].shape[1]), lambda b, t: (0, 0))]
                  if gated_norm else [tok(hv)]) + res_specs
        + [mod_spec(2), mod_spec(4), mod_spec(3), mod_spec(5), resident, vec, vec, resident, resident, resident,
           vec, vec],
        out_specs=tok(d),
        out_shape=jax.ShapeDtypeStruct((batch, n_out, d), F32),
        scratch_shapes=[pltpu.VMEM((tm, hv), BF16)],
        compiler_params=_cparams(("arbitrary", "arbitrary"), 56),
        name="sublayers",
    )(*ys, *residual, mod, mod, mod, mod, w_out16, *ln1, *ffn16, *ln2)


def _da_proj_kernel(x_ref, sc_l, sh_l, sc_c, sh_c, w_ref, cos_ref, sin_a_ref, sin_b_ref,
                    o_ref, xb_ref, *, n_lat, n_qk_tiles):
    j = pl.program_id(1)

    @pl.when(j == 0)
    def _():
        _modulate_into(xb_ref, x_ref[0, :n_lat, :], x_ref[0, n_lat:, :], sc_l, sh_l, sc_c, sh_c)

    blocks = _row_blocks(n_lat, x_ref.shape[1])

    def projector():
        w16 = w_ref[0].astype(BF16)
        return lambda r0, rows: _dot(xb_ref[r0:r0 + rows, :], w16)

    @pl.when(j < n_qk_tiles)
    def _():
        project = projector()
        scale = jnp.where(j < n_qk_tiles // 2, DA_DIM ** -0.5 * LOG2E, 1.0)
        quarter = DA_DIM // 4

        def rope(r0, rows, acc):
            cos, sin_a, sin_b = (t[r0:r0 + rows, :] for t in (cos_ref, sin_a_ref, sin_b_ref))
            for h0 in range(0, DA_COL_TILE, LANES):
                xh = acc[:, h0:h0 + LANES]
                rot = pltpu.roll(xh, quarter, 1) * sin_a + pltpu.roll(xh, LANES - quarter, 1) * sin_b
                o_ref[0, r0:r0 + rows, h0:h0 + LANES] = ((xh * cos + rot) * scale).astype(BF16)

        _pipelined(blocks, project, rope)

    @pl.when(j >= n_qk_tiles)
    def _():
        project = projector()
        for r0, rows in blocks:
            o_ref[0, r0:r0 + rows, :] = project(r0, rows).astype(BF16)


def _da_project(xc, mod, w_in, cos_t, sin_a, sin_b, n_lat):
    batch, n_tot, d = xc.shape
    n_cols = w_in.shape[2]
    n_tiles = n_cols // DA_COL_TILE
    n_qk_tiles = 2 * n_tiles // 3
    kern = functools.partial(_da_proj_kernel, n_lat=n_lat, n_qk_tiles=n_qk_tiles)
    table = pl.BlockSpec((n_tot, LANES), lambda b, j: (0, 0))
    return pl.pallas_call(
        kern,
        grid=(batch, n_tiles),
        in_specs=[pl.BlockSpec((1, n_tot, d), lambda b, j: (b, 0, 0))]
        + _mod_specs(batch, d, 1, 0)
        + [pl.BlockSpec((1, d, DA_COL_TILE), lambda b, j: (0, 0, j)), table, table, table],
        out_specs=pl.BlockSpec((1, n_tot, DA_COL_TILE), lambda b, j: (b, 0, j)),
        out_shape=jax.ShapeDtypeStruct((batch, n_tot, n_cols), BF16),
        scratch_shapes=[pltpu.VMEM((n_tot, d), BF16)],
        compiler_params=_cparams(("arbitrary", "arbitrary"), 56),
        name="da_project",
    )(xc, mod, mod, mod, mod, w_in, cos_t, sin_a, sin_b)


def _da_attn_kernel(q_ref, k_ref, v_ref, lam_ref, g_ref, o_ref, *, lambda_init):
    lam = lam_ref[...]
    lam_val = (jnp.exp(jnp.sum(lam[0:1] * lam[1:2], axis=-1, keepdims=True))
               - jnp.exp(jnp.sum(lam[2:3] * lam[3:4], axis=-1, keepdims=True)) + lambda_init)
    k = k_ref[0]
    n_sub = q_ref.shape[1] // ATT_SUB_ROWS
    lane = lax.broadcasted_iota(jnp.int32, (ATT_SUB_ROWS, 2 * DA_DIM), 1)

    def scores(r):
        q = q_ref[0, r * ATT_SUB_ROWS:(r + 1) * ATT_SUB_ROWS, :]
        zero = jnp.zeros_like(q)
        return [_dot_nt(jnp.where((lane < DA_DIM) if comp == 0 else (lane >= DA_DIM), q, zero), k)
                for comp in range(2)]

    def weights(s):
        e0, e1 = (jnp.exp2(sc - jnp.max(sc, axis=-1, keepdims=True)) for sc in s)
        l0 = jnp.sum(e0, axis=-1, keepdims=True)
        l1 = jnp.sum(e1, axis=-1, keepdims=True)
        return (e0 - e1 * (lam_val * l0 / l1)).astype(BF16), 1.0 / l0

    def values(r, a16, inv_l0):
        o = _dot(a16, v_ref[0]) * inv_l0
        ms = jnp.mean(o * o, axis=-1, keepdims=True)
        o_ref[0, r * ATT_SUB_ROWS:(r + 1) * ATT_SUB_ROWS, :] = (
            o * lax.rsqrt(ms + RMS_EPS) * g_ref[...] * (1.0 - lambda_init)).astype(BF16)

    pending = [scores(r) for r in range(min(ATT_LOOKAHEAD, n_sub))]
    for r in range(n_sub):
        if r + ATT_LOOKAHEAD < n_sub:
            pending.append(scores(r + ATT_LOOKAHEAD))
        values(r, *weights(pending.pop(0)))


def _da_attention(qkv16, lam, subln_g, lambda_init, n_lat):
    batch, n_tot, _ = qkv16.shape
    hw = 2 * DA_DIM
    kern = functools.partial(_da_attn_kernel, lambda_init=lambda_init)
    tq = ATT_Q_TILE if n_lat % ATT_Q_TILE == 0 else ROW_TILE
    return pl.pallas_call(
        kern,
        grid=(batch, DA_HEADS, n_lat // tq),
        in_specs=[
            pl.BlockSpec((1, tq, hw), lambda b, h, i: (b, i, h)),
            pl.BlockSpec((1, n_tot, hw), lambda b, h, i: (b, 0, DA_HEADS + h)),
            pl.BlockSpec((1, n_tot, hw), lambda b, h, i: (b, 0, 2 * DA_HEADS + h)),
            pl.BlockSpec(lam.shape, lambda b, h, i: (0, 0)),
            pl.BlockSpec((1, hw), lambda b, h, i: (0, 0)),
        ],
        out_specs=pl.BlockSpec((1, tq, hw), lambda b, h, i: (b, i, h)),
        out_shape=jax.ShapeDtypeStruct((batch, n_lat, DA_HEADS * hw), BF16),
        compiler_params=_cparams(("arbitrary", "arbitrary", "arbitrary"), 48),
        name="da_attention",
    )(qkv16, qkv16, qkv16, lam, subln_g)


def _rope_tables(n_lat, n_tot):
    quarter = DA_DIM // 4
    inv_freq = ROPE_BASE ** (-jnp.arange(quarter, dtype=F32) / quarter)
    rows = n_lat // GRID_W
    row = jnp.repeat(jnp.arange(rows, dtype=F32), GRID_W)
    col = jnp.tile(jnp.arange(GRID_W, dtype=F32), rows)
    ang_r = row[:, None] * inv_freq
    ang_c = col[:, None] * inv_freq
    ang = jnp.concatenate([ang_r, ang_r, ang_c, ang_c], axis=-1)
    reps = LANES // DA_DIM
    cos = jnp.tile(jnp.cos(ang), (1, reps))
    sin = jnp.tile(jnp.sin(ang), (1, reps))
    upper = (jnp.arange(LANES) % (2 * quarter)) >= quarter
    sin_a = jnp.where(upper, sin, 0.0)
    sin_b = jnp.where(upper, 0.0, -sin)
    n_ctx = n_tot - n_lat
    ext = lambda t, fill: jnp.concatenate([t, jnp.full((n_ctx, LANES), fill, F32)], axis=0)
    return ext(cos, 1.0), ext(sin_a, 0.0), ext(sin_b, 0.0)


def kernel(x, c, ctx, c_ctx, ada_w, ada_b, ln1_g, ln1_b, ln2_g, ln2_b, ffn_w_gate, ffn_w_up, ffn_w_down,
           dn_w_in, dn_conv, dn_a_log, dn_dt_bias, dn_norm_g, dn_w_out, da_w_in, da_lambda, da_subln_g,
           da_w_out):
    batch, n_lat, d = x.shape
    n_ctx = ctx.shape[1]
    n_tot = n_lat + n_ctx
    depth = ada_w.shape[0]
    assert depth == 2, "layer 0 is gated DeltaNet, layer 1 (the last) differential attention"
    assert batch < MOD_ROWS and n_lat % ROW_TILE == 0 and n_ctx % ROW_TILE == 0
    alpha = (2.0 * depth) ** 0.25
    hk = DN_HEADS * DN_DK

    cs = jnp.concatenate([c, c_ctx[None, :], jnp.zeros((MOD_ROWS - batch - 1, d), F32)], axis=0)
    mods = _ada_modulation(cs, ada_w, ada_b).reshape(depth, MOD_ROWS, 1, 6 * d)

    mod = mods[0]
    gate_cols = dn_w_in[0][:, 4 * hk:]
    w_gates = jnp.concatenate([gate_cols, jnp.zeros((d, LANES - gate_cols.shape[1]), F32)], axis=1)
    lane_pad = lambda v: jnp.concatenate([v.reshape(1, -1), jnp.zeros((1, LANES - v.size), F32)], axis=1)
    qkv, z, gcol, grow = _dn_project(x, ctx, mod, dn_w_in, w_gates, dn_conv[0],
                                     lane_pad(dn_a_log[0]), lane_pad(dn_dt_bias[0]))
    o_fwd, o_bwd = _dn_scan(qkv, gcol, grow, n_lat)
    vec = lambda v: v.reshape(1, -1)
    ffn16 = lambda i: tuple(_to_bf16(w, i) for w in (ffn_w_gate, ffn_w_up, ffn_w_down))
    xc = _sublayers((o_fwd, o_bwd, z, vec(dn_norm_g[0])), (x, ctx), mod, _to_bf16(dn_w_out, 0),
                    (vec(ln1_g[0]), vec(ln1_b[0])), ffn16(0), (vec(ln2_g[0]), vec(ln2_b[0])),
                    alpha, n_lat, n_tot, True)

    mod = mods[1]
    lambda_init = 0.8 - 0.6 * math.exp(-0.3 * 1)
    cos_t, sin_a, sin_b = _rope_tables(n_lat, n_tot)
    qkv16 = _da_project(xc, mod, da_w_in, cos_t, sin_a, sin_b, n_lat)
    y = _da_attention(qkv16, da_lambda[0], da_subln_g[0].reshape(1, -1), lambda_init, n_lat)
    return _sublayers((y,), (xc,), mod, _to_bf16(da_w_out, 0), (vec(ln1_g[1]), vec(ln1_b[1])), ffn16(1),
                      (vec(ln2_g[1]), vec(ln2_b[1])), alpha, n_lat, n_lat, False)
```

```python
import functools
import math

import jax
import jax.numpy as jnp
from jax import lax
from jax.experimental import pallas as pl
from jax.experimental.pallas import tpu as pltpu

F32 = jnp.float32
BF16 = jnp.bfloat16

DN_HEADS = 8
DN_DK = 128
DN_DV = 128
DN_CONV = 5
DN_CHUNK = 64
DA_HEADS = 8
DA_DIM = 64
GRID_W = 64
ROPE_BASE = 10000.0
LN_EPS = 1e-5
RMS_EPS = 1e-6

LANES = 128
SUBLANES = 8
MOD_ROWS = 16
ROW_TILE = 256
WIDE_ROW_TILE = 512
PROJ_ROW_BLOCK = 256
COL_TILE = 256
DA_COL_TILE = 512
ADA_COL_TILE = 1536
ATT_Q_TILE = 2048
ATT_SUB_ROWS = 256
ATT_LOOKAHEAD = 2
LOG2E = math.log2(math.e)


VMEM_LIMIT_MIB = {
    "to_bf16": 40,
    "ada_modulation": 40,
    "dn_project": 56,
    "dn_scan": 48,
    "sublayers": 56,
    "da_project": 56,
    "da_attention": 48,
}


def _cparams(name, n_grid_axes):
    return pltpu.CompilerParams(dimension_semantics=("arbitrary",) * n_grid_axes,
                                vmem_limit_bytes=VMEM_LIMIT_MIB[name] * 1024 * 1024)


def _silu(x):
    return x * jax.nn.sigmoid(x)


def _dot(a, b):
    return jnp.dot(a, b, preferred_element_type=F32)


def _dot_nt(a, b):
    return lax.dot_general(a, b, (((1,), (1,)), ((), ())), preferred_element_type=F32)


def _layernorm(x, g, b):
    mu = jnp.mean(x, axis=-1, keepdims=True)
    xc = x - mu
    var = jnp.mean(xc * xc, axis=-1, keepdims=True)
    return xc * lax.rsqrt(var + LN_EPS) * g + b


def _cast_kernel(w_ref, o_ref):
    o_ref[...] = w_ref[...].astype(BF16)


def _to_bf16(w, layer):
    _, rows, cols = w.shape
    tr = rows // 2
    return pl.pallas_call(
        _cast_kernel,
        grid=(rows // tr,),
        in_specs=[pl.BlockSpec((1, tr, cols), lambda r: (layer, r, 0))],
        out_specs=pl.BlockSpec((1, tr, cols), lambda r: (0, r, 0)),
        out_shape=jax.ShapeDtypeStruct((1, rows, cols), BF16),
        compiler_params=_cparams("to_bf16", 1),
        name="to_bf16",
    )(w).reshape(rows, cols)


def _ada_kernel(cs_ref, w_ref, b_ref, o_ref):
    s = _silu(cs_ref[...]).astype(BF16)
    o_ref[0] = _dot(s, w_ref[0].astype(BF16)) + b_ref[0]


def _ada_modulation(cs, ada_w, ada_b):
    depth, d, n = ada_w.shape
    tn = ADA_COL_TILE if n % ADA_COL_TILE == 0 else n
    return pl.pallas_call(
        _ada_kernel,
        grid=(depth, n // tn),
        in_specs=[
            pl.BlockSpec((MOD_ROWS, d), lambda i, j: (0, 0)),
            pl.BlockSpec((1, d, tn), lambda i, j: (i, 0, j)),
            pl.BlockSpec((1, 1, tn), lambda i, j: (i, 0, j)),
        ],
        out_specs=pl.BlockSpec((1, MOD_ROWS, tn), lambda i, j: (i, 0, j)),
        out_shape=jax.ShapeDtypeStruct((depth, MOD_ROWS, n), F32),
        compiler_params=_cparams("ada_modulation", 2),
        name="ada_modulation",
    )(cs, ada_w, ada_b.reshape(depth, 1, n))


def _modulate_into(xb_ref, lat, ctx, sc_l, sh_l, sc_c, sh_c):
    n_lat = lat.shape[0]
    xb_ref[:n_lat, :] = (lat * (1.0 + sc_l[0]) + sh_l[0]).astype(BF16)
    xb_ref[n_lat:, :] = (ctx * (1.0 + sc_c[0]) + sh_c[0]).astype(BF16)


def _row_blocks(n_lat, n_tot):
    blocks = []
    for seg0, seg in ((0, n_lat), (n_lat, n_tot - n_lat)):
        size = PROJ_ROW_BLOCK if seg % PROJ_ROW_BLOCK == 0 else seg
        blocks += [(seg0 + r, size) for r in range(0, seg, size)]
    return blocks


def _pipelined(blocks, produce, consume):
    pending = None
    for blk in blocks:
        cur = produce(*blk)
        if pending is not None:
            consume(*pending)
        pending = blk + (cur,)
    consume(*pending)


def _mod_specs(batch, d, sc_idx, sh_idx):
    return [
        pl.BlockSpec((1, 1, d), lambda b, j: (b, 0, sc_idx)),
        pl.BlockSpec((1, 1, d), lambda b, j: (b, 0, sh_idx)),
        pl.BlockSpec((1, 1, d), lambda b, j: (batch, 0, sc_idx)),
        pl.BlockSpec((1, 1, d), lambda b, j: (batch, 0, sh_idx)),
    ]


def _dn_proj_kernel(x_ref, c_ref, sc_l, sh_l, sc_c, sh_c, w_ref, wg_ref, conv_ref, alog_ref, dt_ref,
                    qkv_ref, z_ref, gcol_ref, grow_ref, xb_ref, pad_ref,
                    *, n_lat, n_tot, n_qkv_tiles, n_z_tiles):
    j = pl.program_id(1)
    n_ctx = n_tot - n_lat
    halo = SUBLANES
    pad = (DN_CONV - 1) // 2
    lat0 = halo
    ctx0 = 2 * halo + n_lat

    @pl.when(j == 0)
    def _():
        _modulate_into(xb_ref, x_ref[0], c_ref[0], sc_l, sh_l, sc_c, sh_c)

    blocks = _row_blocks(n_lat, n_tot)

    def project(r0, rows):
        return _dot(xb_ref[r0:r0 + rows, :], w_ref[...])

    @pl.when(j < n_qkv_tiles)
    def _():
        zeros = jnp.zeros((halo, COL_TILE), F32)
        pad_ref[0:halo, :] = zeros
        pad_ref[lat0 + n_lat:ctx0, :] = zeros
        pad_ref[ctx0 + n_ctx:ctx0 + n_ctx + halo, :] = zeros
        cw = conv_ref[...]
        n_qk_tiles = 2 * n_qkv_tiles // 3
        is_q = j < n_qk_tiles // 2
        is_qk = j < n_qk_tiles

        def padded(r0):
            return r0 + (lat0 if r0 < n_lat else ctx0 - n_lat)

        def stage(r0, rows):
            pad_ref[padded(r0):padded(r0) + rows, :] = project(r0, rows)

        def conv_norm(r0, rows, _):
            base = padded(r0)
            ext = pad_ref[base - halo:base + rows + halo, :]
            y = None
            for t in range(DN_CONV):
                shifted = ext if t == pad else pltpu.roll(ext, (pad - t) % ext.shape[0], 0)
                term = shifted[halo:halo + rows, :] * cw[t:t + 1, :]
                y = term if y is None else y + term
            y = _silu(y)
            for h0 in range(0, COL_TILE, DN_DK):
                yh = y[:, h0:h0 + DN_DK]
                ss = jnp.sum(yh * yh, axis=-1, keepdims=True)
                fac = lax.rsqrt(ss + RMS_EPS)
                fac = jnp.where(is_q, fac * (DN_DK ** -0.5), fac)
                fac = jnp.where(is_qk, fac, jnp.ones_like(fac))
                qkv_ref[0, r0:r0 + rows, h0:h0 + DN_DK] = yh * fac

        _pipelined(blocks, stage, conv_norm)

    @pl.when(jnp.logical_and(j >= n_qkv_tiles, j < n_qkv_tiles + n_z_tiles))
    def _():
        for r0, rows in blocks:
            z_ref[0, r0:r0 + rows, :] = project(r0, rows)

    @pl.when(j == n_qkv_tiles + n_z_tiles)
    def _():
        t = _dot(xb_ref[...], wg_ref[...].astype(BF16))
        lane = lax.broadcasted_iota(jnp.int32, t.shape, 1)
        x = t + dt_ref[...]
        softplus = jnp.maximum(x, 0.0) + jnp.log1p(jnp.exp(-jnp.abs(x)))
        decay = -jnp.exp(alog_ref[...]) * softplus
        gates = jnp.where(lane < 2 * DN_HEADS, decay, jax.nn.sigmoid(t))
        ri = lax.broadcasted_iota(jnp.int32, (DN_CHUNK, DN_CHUNK), 0)
        ci = lax.broadcasted_iota(jnp.int32, (DN_CHUNK, DN_CHUNK), 1)
        tri = jnp.concatenate([ri >= ci, ri <= ci], axis=0).astype(BF16)
        hi = gates.astype(BF16)
        rest = gates - hi.astype(F32)
        mid = rest.astype(BF16)
        low = (rest - mid.astype(F32)).astype(BF16)
        parts = jnp.concatenate([hi, mid, low], axis=1)
        lane_c = lax.broadcasted_iota(jnp.int32, (DN_CHUNK, LANES), 1)
        for c in range(n_tot // DN_CHUNK):
            gc = gates[c * DN_CHUNK:(c + 1) * DN_CHUNK]
            cum3 = _dot(tri, parts[c * DN_CHUNK:(c + 1) * DN_CHUNK])
            cum = cum3[:, :LANES] + cum3[:, LANES:2 * LANES] + cum3[:, 2 * LANES:]
            col = jnp.where(lane_c < DN_HEADS, cum[:DN_CHUNK],
                            jnp.where(lane_c < 2 * DN_HEADS, cum[DN_CHUNK:], gc))
            gcol_ref[0, c * DN_CHUNK:(c + 1) * DN_CHUNK, :] = col
            grow_ref[0, c] = jnp.concatenate([col, col], axis=0).T


def _dn_project(x, ctx, mod, w16, w_gates, conv_w, alog_row, dt_row):
    batch, n_lat, d = x.shape
    n_tot = n_lat + ctx.shape[1]
    hk = DN_HEADS * DN_DK
    n_qkv_tiles = 3 * hk // COL_TILE
    n_z_tiles = hk // COL_TILE
    n_tiles = n_qkv_tiles + n_z_tiles + 1
    nc = n_tot // DN_CHUNK
    kern = functools.partial(_dn_proj_kernel, n_lat=n_lat, n_tot=n_tot,
                             n_qkv_tiles=n_qkv_tiles, n_z_tiles=n_z_tiles)
    return pl.pallas_call(
        kern,
        grid=(batch, n_tiles),
        in_specs=[pl.BlockSpec((1, n_lat, d), lambda b, j: (b, 0, 0)),
                  pl.BlockSpec((1, n_tot - n_lat, d), lambda b, j: (b, 0, 0))]
        + _mod_specs(batch, d, 1, 0)
        + [
            pl.BlockSpec((d, COL_TILE), lambda b, j: (0, jnp.minimum(j, n_tiles - 2))),
            pl.BlockSpec((d, LANES), lambda b, j: (0, 0)),
            pl.BlockSpec((DN_CONV, COL_TILE), lambda b, j: (0, jnp.minimum(j, n_qkv_tiles - 1))),
            pl.BlockSpec((1, LANES), lambda b, j: (0, 0)),
            pl.BlockSpec((1, LANES), lambda b, j: (0, 0)),
        ],
        out_specs=[
            pl.BlockSpec((1, n_tot, COL_TILE), lambda b, j: (b, 0, jnp.minimum(j, n_qkv_tiles - 1))),
            pl.BlockSpec((1, n_tot, COL_TILE),
                         lambda b, j: (b, 0, jnp.clip(j - n_qkv_tiles, 0, n_z_tiles - 1))),
            pl.BlockSpec((1, n_tot, LANES), lambda b, j: (b, 0, 0)),
            pl.BlockSpec((1, nc, LANES, LANES), lambda b, j: (b, 0, 0, 0)),
        ],
        out_shape=[
            jax.ShapeDtypeStruct((batch, n_tot, 3 * hk), F32),
            jax.ShapeDtypeStruct((batch, n_tot, hk), F32),
            jax.ShapeDtypeStruct((batch, n_tot, LANES), F32),
            jax.ShapeDtypeStruct((batch, nc, LANES, LANES), F32),
        ],
        scratch_shapes=[
            pltpu.VMEM((n_tot, d), BF16),
            pltpu.VMEM((n_tot + 3 * SUBLANES, COL_TILE), F32),
        ],
        compiler_params=_cparams("dn_project", 2),
        name="dn_project",
    )(x, ctx, mod, mod, mod, mod, w16, w_gates, conv_w, alog_row, dt_row)


INV_BASE = 16
SCAN_BATCH = 2
SCAN_UNITS = SCAN_BATCH * DN_HEADS


def _side_by_side(fwd, bwd):
    return jnp.concatenate([fwd, bwd], axis=1)


def _block_diag16(top, bottom):
    top, bottom = top.astype(BF16), bottom.astype(BF16)
    return jnp.concatenate([_side_by_side(top, jnp.zeros((top.shape[0], bottom.shape[1]), BF16)),
                            _side_by_side(jnp.zeros((bottom.shape[0], top.shape[1]), BF16), bottom)], axis=0)


def _dn_local_stages(ins, slot, sidx, half):
    u_ref, wq_ref, m_ref, e_ref = slot
    c = DN_CHUNK
    st = [dict() for _ in range(SCAN_UNITS)]

    def each(fn):
        def run():
            for n, t in enumerate(st):
                fn(t, n)
        return run

    def grids():
        ri = lax.broadcasted_iota(jnp.int32, (c, 2 * c), 0)
        lane = lax.broadcasted_iota(jnp.int32, (c, 2 * c), 1)
        return ri, lane & (c - 1), lane < c

    def pair_dot(x, y):
        _, _, fwd_half = grids()
        rhs = jnp.concatenate([jnp.where(fwd_half, y, 0.0).astype(BF16),
                               jnp.where(fwd_half, 0.0, y).astype(BF16)], axis=0)
        return _dot(x.astype(BF16), rhs)

    def load(t, n):
        bb, hh = divmod(n, DN_HEADS)
        ri, cj, fwd_half = grids()
        lane = lax.broadcasted_iota(jnp.int32, (c, LANES), 1)
        cols = slice(hh * DN_DK, (hh + 1) * DN_DK)
        per_dir = []
        for d in range(2):
            q_ref, k_ref, v_ref, gcol_ref, grow_ref = ins[d]
            part = half if d == 0 else 1 - half
            rows = slice(part * c, (part + 1) * c)
            gates = gcol_ref[bb, rows, :]
            g_lane = d * DN_HEADS + hh
            gcum = jnp.sum(jnp.where(lane == g_lane, gates, 0.0), axis=-1, keepdims=True)
            beta = jnp.sum(jnp.where(lane == 2 * DN_HEADS + g_lane, gates, 0.0), axis=-1, keepdims=True)
            grow = grow_ref[bb, part, g_lane:g_lane + 1, :]
            g_last = grow[:, c - 1:c] if d == 0 else grow[:, 0:1]
            per_dir.append((q_ref[bb, rows, cols], k_ref[bb, rows, cols], v_ref[bb, rows, cols],
                            gcum, beta, grow, g_last))
        (qf, kf, vf, gcf, bf, grf, glf), (qb, kb, vb, gcb, bb, grb, glb) = per_dir
        incl = jnp.logical_or(ri == cj, (ri > cj) == fwd_half)
        diff = jnp.where(fwd_half, gcf, gcb) - jnp.where(fwd_half[0:1], grf, grb)
        t["gamma"] = jnp.where(incl, jnp.exp(jnp.where(incl, diff, 0.0)), 0.0)
        kbf, kbb = kf * bf, kb * bb
        dec_f, dec_b = jnp.exp(gcf), jnp.exp(gcb)
        t["lhs16"] = _side_by_side(jnp.concatenate([kbf, qf], axis=0).astype(BF16),
                                   jnp.concatenate([kbb, qb], axis=0).astype(BF16))
        t["keys16"] = _block_diag16(kf, kb)
        t["rhs16"] = _block_diag16(_side_by_side(vf * bf, kbf * dec_f), _side_by_side(vb * bb, kbb * dec_b))
        wq_ref[sidx, n, c:, :DN_DK] = (qf * dec_f).astype(BF16)
        wq_ref[sidx, n, c:, DN_DK:] = (qb * dec_b).astype(BF16)
        k_dec = jnp.concatenate([kf * jnp.exp(glf - gcf), kb * jnp.exp(glb - gcb)], axis=0)
        m_ref[sidx, n, c:, :] = k_dec.T.astype(BF16)
        e_ref[sidx, n] = _side_by_side(jnp.broadcast_to(jnp.exp(glf), (SUBLANES, DN_DV)),
                                        jnp.broadcast_to(jnp.exp(glb), (SUBLANES, DN_DV)))

    def gram(t, n):
        t["kq"] = _dot_nt(t.pop("lhs16"), t.pop("keys16"))

    def split(t, n):
        ri, cj, fwd_half = grids()
        strict = jnp.logical_and(ri != cj, (ri > cj) == fwd_half)
        kq = t.pop("kq")
        gamma = t.pop("gamma")
        a = jnp.where(strict, kq[:c] * gamma, 0.0)
        m_ref[sidx, n, :c, :] = (kq[c:] * gamma).astype(BF16)
        a_diag = jnp.where((ri // INV_BASE) == (cj // INV_BASE), a, 0.0)
        t["a"] = a
        t["inv"] = (ri == cj).astype(F32) - a_diag
        t["pw"] = pair_dot(a_diag, a_diag)

    def series(t, n):
        both = pair_dot(jnp.concatenate([t["pw"], t["inv"]], axis=0), t["pw"])
        t["pw"] = both[:c]
        t["inv"] = t["inv"] + both[c:]

    def series_last(t, n):
        t["inv"] = t["inv"] + pair_dot(t["inv"], t.pop("pw"))

    def merge_a(blk):
        def fn(t, n):
            ri, cj, _ = grids()
            off = jnp.logical_and((ri // (2 * blk)) == (cj // (2 * blk)), (ri // blk) != (cj // blk))
            t["y"] = pair_dot(jnp.where(off, t["a"], 0.0), t["inv"])
        return fn

    def merge_b(t, n):
        t["inv"] = t["inv"] - pair_dot(t["inv"], t.pop("y"))

    def solve(t, n):
        t["uw"] = _dot(t.pop("inv").astype(BF16), t.pop("rhs16"))
        t.pop("a")

    def store(t, n):
        uw = t.pop("uw")
        for d in range(2):
            base = d * (DN_DV + DN_DK)
            u_ref[sidx, n, :, d * DN_DV:(d + 1) * DN_DV] = uw[:, base:base + DN_DV]
            wq_ref[sidx, n, :c, d * DN_DK:(d + 1) * DN_DK] = uw[:, base + DN_DV:base + DN_DV + DN_DK].astype(BF16)

    stages = [load, gram, split] + [series] * (int(math.log2(INV_BASE)) - 2) + [series_last]
    blk = INV_BASE
    while blk < c:
        stages += [merge_a(blk), merge_b]
        blk *= 2
    stages += [solve, store]
    return [each(fn) for fn in stages]


def _dn_serial_stages(outs, s_ref, slot, sidx, half):
    u_ref, wq_ref, m_ref, e_ref = slot
    c = DN_CHUNK
    st = [dict() for _ in range(SCAN_UNITS)]

    def each(fn):
        def run():
            for n, t in enumerate(st):
                fn(t, n)
        return run

    def state_dot(t, n):
        t["s"] = (s_ref[n], s_ref[SCAN_UNITS + n])
        t["ws"] = _dot(wq_ref[sidx, n], _block_diag16(*t["s"]))

    def value_dot(t, n):
        v_new = u_ref[sidx, n] - t["ws"][:c]
        t["mv"] = _dot(m_ref[sidx, n], _block_diag16(v_new[:, :DN_DV], v_new[:, DN_DV:]))

    def update(t, n):
        bb, hh = divmod(n, DN_HEADS)
        mv, ws, s = t.pop("mv"), t.pop("ws"), t.pop("s")
        gain = e_ref[sidx, n][0:1, :]
        for d in range(2):
            lanes = slice(d * DN_DV, (d + 1) * DN_DV)
            s_ref[d * SCAN_UNITS + n] = s[d] * gain[:, lanes] + mv[c:, lanes]
            part = half if d == 0 else 1 - half
            outs[d][bb, part * c:(part + 1) * c, hh * DN_DV:(hh + 1) * DN_DV] = ws[c:, lanes] + mv[:c, lanes]

    return [each(fn) for fn in (state_dot, value_dot, update)]


def _dn_scan_kernel(*refs):
    ins = (refs[0:5], refs[5:10])
    outs = refs[10:12]
    s_ref = refs[12]
    slot_sets = (refs[13:17], refs[17:21])
    g = pl.program_id(1)

    @pl.when(g == 0)
    def _():
        for ref in (s_ref,) + tuple(slot_sets[0]) + tuple(slot_sets[1]):
            ref[...] = jnp.zeros(ref.shape, ref.dtype)

    def step(written, read):
        loc = [_dn_local_stages(ins, written, half, half) for half in range(2)]
        ser = [stage for half in range(2) for stage in _dn_serial_stages(outs, s_ref, read, half, half)]
        gap = len(loc[0]) // len(ser)
        for n, stages in enumerate(zip(*loc)):
            if n % gap == 0 and n // gap < len(ser):
                ser[n // gap]()
            for stage in stages:
                stage()

    for parity in range(2):
        pl.when(g % 2 == parity)(functools.partial(step, slot_sets[parity], slot_sets[1 - parity]))


def _dn_scan(qkv, gcol, grow, n_lat):
    batch, n_tot, _ = qkv.shape
    hk = DN_HEADS * DN_DK
    pair = 2 * DN_CHUNK
    n_pairs = n_tot // pair
    lat_pairs = n_lat // pair
    ctx_pairs = n_pairs - lat_pairs
    assert n_lat % pair == 0 and n_tot % pair == 0 and batch % SCAN_BATCH == 0

    def fwd(p):
        return jnp.where(p < ctx_pairs, p + lat_pairs, p - ctx_pairs)

    def bwd(p):
        return n_pairs - 1 - p

    def p_in(g):
        return jnp.minimum(g, n_pairs - 1)

    def p_out(g):
        return jnp.maximum(g - 1, 0)

    def in_specs(where):
        return [pl.BlockSpec((SCAN_BATCH, pair, hk), lambda b, g, col=col: (b, where(p_in(g)), col))
                for col in range(3)] + [
            pl.BlockSpec((SCAN_BATCH, pair, LANES), lambda b, g: (b, where(p_in(g)), 0)),
            pl.BlockSpec((SCAN_BATCH, 2, LANES, LANES), lambda b, g: (b, where(p_in(g)), 0, 0)),
        ]

    out_spec = lambda where: pl.BlockSpec((SCAN_BATCH, pair, hk), lambda b, g: (b, where(p_out(g)), 0))
    out_shape = jax.ShapeDtypeStruct((batch, n_tot, hk), F32)
    slot = [pltpu.VMEM((2, SCAN_UNITS, DN_CHUNK, 2 * DN_DV), F32),
            pltpu.VMEM((2, SCAN_UNITS, 2 * DN_CHUNK, 2 * DN_DK), BF16),
            pltpu.VMEM((2, SCAN_UNITS, DN_CHUNK + DN_DK, 2 * DN_CHUNK), BF16),
            pltpu.VMEM((2, SCAN_UNITS, SUBLANES, 2 * DN_DV), F32)]
    per_dir = (qkv, qkv, qkv, gcol, grow)
    return pl.pallas_call(
        _dn_scan_kernel,
        grid=(batch // SCAN_BATCH, n_pairs + 1),
        in_specs=in_specs(fwd) + in_specs(bwd),
        out_specs=[out_spec(fwd), out_spec(bwd)],
        out_shape=[out_shape, out_shape],
        scratch_shapes=[pltpu.VMEM((2 * SCAN_UNITS, DN_DK, DN_DV), F32)] + slot + slot,
        compiler_params=_cparams("dn_scan", 2),
        name="dn_scan",
    )(*per_dir, *per_dir)


def _sublayers_kernel(*refs, alpha, gated_norm, split_residual, n_lat_tiles):
    if gated_norm:
        y_ref, y2_ref, z_ref, ng_ref = refs[:4]
    else:
        y_ref = refs[0]
    (gt1_ref, sc2_ref, sh2_ref, gt2_ref, wo_ref, l1g_ref, l1b_ref,
     wg_ref, wu_ref, wd_ref, l2g_ref, l2b_ref, o_ref, yb_ref) = refs[-14:]
    if split_residual:
        x_ref, c_ref = refs[-16:-14]
        residual = jnp.where(pl.program_id(1) < n_lat_tiles, x_ref[0], c_ref[0])
    else:
        residual = refs[-15][0]
    if gated_norm:
        for h0 in range(0, y_ref.shape[2], DN_DV):
            yh = y_ref[0, :, h0:h0 + DN_DV] + y2_ref[0, :, h0:h0 + DN_DV]
            ms = jnp.mean(yh * yh, axis=-1, keepdims=True)
            yh = yh * lax.rsqrt(ms + RMS_EPS) * ng_ref[...]
            yb_ref[:, h0:h0 + DN_DV] = (yh * _silu(z_ref[0, :, h0:h0 + DN_DV])).astype(BF16)
        mixed = yb_ref[...]
    else:
        mixed = y_ref[0]
    x1 = _layernorm(alpha * residual + gt1_ref[0] * _dot(mixed, wo_ref[...]), l1g_ref[...], l1b_ref[...])
    h = (x1 * (1.0 + sc2_ref[0]) + sh2_ref[0]).astype(BF16)
    gate = _dot(h, wg_ref[...])
    up = _dot(h, wu_ref[...])
    act = (_silu(gate) * up).astype(BF16)
    o_ref[0] = _layernorm(alpha * x1 + gt2_ref[0] * _dot(act, wd_ref[...]), l2g_ref[...], l2b_ref[...])


def _sublayers(ys, residual, mod, w_out16, ln1, ffn16, ln2, alpha, n_lat, n_out, gated_norm):
    batch, _, d = residual[0].shape
    hv = ys[0].shape[2]
    split = len(residual) == 2
    tm = WIDE_ROW_TILE if not split and n_out % WIDE_ROW_TILE == 0 and n_lat % WIDE_ROW_TILE == 0 else ROW_TILE
    n_lat_tiles = n_lat // tm
    kern = functools.partial(_sublayers_kernel, alpha=alpha, gated_norm=gated_norm, split_residual=split,
                             n_lat_tiles=n_lat_tiles)
    tok = lambda width: pl.BlockSpec((1, tm, width), lambda b, t: (b, t, 0))
    res_specs = [tok(d)] if not split else [
        pl.BlockSpec((1, tm, d), lambda b, t: (b, jnp.minimum(t, n_lat_tiles - 1), 0)),
        pl.BlockSpec((1, tm, d), lambda b, t: (b, jnp.maximum(t - n_lat_tiles, 0), 0))]
    mod_spec = lambda idx: pl.BlockSpec((1, 1, d), lambda b, t: (jnp.where(t < n_lat_tiles, b, batch), 0, idx))
    resident = pl.BlockSpec(memory_space=pltpu.VMEM)
    vec = pl.BlockSpec((1, d), lambda b, t: (0, 0))
    return pl.pallas_call(
        kern,
        grid=(batch, n_out // tm),
        in_specs=([tok(hv)] * 3 + [pl.BlockSpec((1, ys[3].shape[1]), lambda b, t: (0, 0))]
                  if gated_norm else [tok(hv)]) + res_specs
        + [mod_spec(2), mod_spec(4), mod_spec(3), mod_spec(5), resident, vec, vec, resident, resident, resident,
           vec, vec],
        out_specs=tok(d),
        out_shape=jax.ShapeDtypeStruct((batch, n_out, d), F32),
        scratch_shapes=[pltpu.VMEM((tm, hv), BF16)],
        compiler_params=_cparams("sublayers", 2),
        name="sublayers",
    )(*ys, *residual, mod, mod, mod, mod, w_out16, *ln1, *ffn16, *ln2)


def _da_proj_kernel(x_ref, sc_l, sh_l, sc_c, sh_c, w_ref, cos_ref, sin_a_ref, sin_b_ref,
                    o_ref, xb_ref, *, n_lat, n_qk_tiles):
    j = pl.program_id(1)

    @pl.when(j == 0)
    def _():
        _modulate_into(xb_ref, x_ref[0, :n_lat, :], x_ref[0, n_lat:, :], sc_l, sh_l, sc_c, sh_c)

    blocks = _row_blocks(n_lat, x_ref.shape[1])

    def projector():
        w16 = w_ref[0].astype(BF16)
        return lambda r0, rows: _dot(xb_ref[r0:r0 + rows, :], w16)

    @pl.when(j < n_qk_tiles)
    def _():
        project = projector()
        scale = jnp.where(j < n_qk_tiles // 2, DA_DIM ** -0.5 * LOG2E, 1.0)
        quarter = DA_DIM // 4

        def rope(r0, rows, acc):
            cos, sin_a, sin_b = (t[r0:r0 + rows, :] for t in (cos_ref, sin_a_ref, sin_b_ref))
            for h0 in range(0, DA_COL_TILE, LANES):
                xh = acc[:, h0:h0 + LANES]
                rot = pltpu.roll(xh, quarter, 1) * sin_a + pltpu.roll(xh, LANES - quarter, 1) * sin_b
                o_ref[0, r0:r0 + rows, h0:h0 + LANES] = ((xh * cos + rot) * scale).astype(BF16)

        _pipelined(blocks, project, rope)

    @pl.when(j >= n_qk_tiles)
    def _():
        project = projector()
        for r0, rows in blocks:
            o_ref[0, r0:r0 + rows, :] = project(r0, rows).astype(BF16)


def _da_project(xc, mod, w_in, cos_t, sin_a, sin_b, n_lat):
    batch, n_tot, d = xc.shape
    n_cols = w_in.shape[2]
    n_tiles = n_cols // DA_COL_TILE
    n_qk_tiles = 2 * n_tiles // 3
    kern = functools.partial(_da_proj_kernel, n_lat=n_lat, n_qk_tiles=n_qk_tiles)
    table = pl.BlockSpec((n_tot, LANES), lambda b, j: (0, 0))
    return pl.pallas_call(
        kern,
        grid=(batch, n_tiles),
        in_specs=[pl.BlockSpec((1, n_tot, d), lambda b, j: (b, 0, 0))]
        + _mod_specs(batch, d, 1, 0)
        + [pl.BlockSpec((1, d, DA_COL_TILE), lambda b, j: (0, 0, j)), table, table, table],
        out_specs=pl.BlockSpec((1, n_tot, DA_COL_TILE), lambda b, j: (b, 0, j)),
        out_shape=jax.ShapeDtypeStruct((batch, n_tot, n_cols), BF16),
        scratch_shapes=[pltpu.VMEM((n_tot, d), BF16)],
        compiler_params=_cparams("da_project", 2),
        name="da_project",
    )(xc, mod, mod, mod, mod, w_in, cos_t, sin_a, sin_b)


def _da_attn_kernel(q_ref, k_ref, v_ref, lam_ref, g_ref, o_ref, *, lambda_init):
    lam = lam_ref[...]
    lam_val = (jnp.exp(jnp.sum(lam[0:1] * lam[1:2], axis=-1, keepdims=True))
               - jnp.exp(jnp.sum(lam[2:3] * lam[3:4], axis=-1, keepdims=True)) + lambda_init)
    k = k_ref[0]
    n_sub = q_ref.shape[1] // ATT_SUB_ROWS
    lane = lax.broadcasted_iota(jnp.int32, (ATT_SUB_ROWS, 2 * DA_DIM), 1)

    def scores(r):
        q = q_ref[0, r * ATT_SUB_ROWS:(r + 1) * ATT_SUB_ROWS, :]
        zero = jnp.zeros_like(q)
        return [_dot_nt(jnp.where((lane < DA_DIM) if comp == 0 else (lane >= DA_DIM), q, zero), k)
                for comp in range(2)]

    def weights(s):
        e0, e1 = (jnp.exp2(sc - jnp.max(sc, axis=-1, keepdims=True)) for sc in s)
        l0 = jnp.sum(e0, axis=-1, keepdims=True)
        l1 = jnp.sum(e1, axis=-1, keepdims=True)
        return (e0 - e1 * (lam_val * l0 / l1)).astype(BF16), 1.0 / l0

    def values(r, a16, inv_l0):
        o = _dot(a16, v_ref[0]) * inv_l0
        ms = jnp.mean(o * o, axis=-1, keepdims=True)
        o_ref[0, r * ATT_SUB_ROWS:(r + 1) * ATT_SUB_ROWS, :] = (
            o * lax.rsqrt(ms + RMS_EPS) * g_ref[...] * (1.0 - lambda_init)).astype(BF16)

    pending = [scores(r) for r in range(min(ATT_LOOKAHEAD, n_sub))]
    for r in range(n_sub):
        if r + ATT_LOOKAHEAD < n_sub:
            pending.append(scores(r + ATT_LOOKAHEAD))
        values(r, *weights(pending.pop(0)))


def _da_attention(qkv16, lam, subln_g, lambda_init, n_lat):
    batch, n_tot, _ = qkv16.shape
    hw = 2 * DA_DIM
    kern = functools.partial(_da_attn_kernel, lambda_init=lambda_init)
    tq = ATT_Q_TILE if n_lat % ATT_Q_TILE == 0 else ROW_TILE
    return pl.pallas_call(
        kern,
        grid=(batch, DA_HEADS, n_lat // tq),
        in_specs=[
            pl.BlockSpec((1, tq, hw), lambda b, h, i: (b, i, h)),
            pl.BlockSpec((1, n_tot, hw), lambda b, h, i: (b, 0, DA_HEADS + h)),
            pl.BlockSpec((1, n_tot, hw), lambda b, h, i: (b, 0, 2 * DA_HEADS + h)),
            pl.BlockSpec(lam.shape, lambda b, h, i: (0, 0)),
            pl.BlockSpec((1, hw), lambda b, h, i: (0, 0)),
        ],
        out_specs=pl.BlockSpec((1, tq, hw), lambda b, h, i: (b, i, h)),
        out_shape=jax.ShapeDtypeStruct((batch, n_lat, DA_HEADS * hw), BF16),
        compiler_params=_cparams("da_attention", 3),
        name="da_attention",
    )(qkv16, qkv16, qkv16, lam, subln_g)


def _rope_tables(n_lat, n_tot):
    quarter = DA_DIM // 4
    inv_freq = ROPE_BASE ** (-jnp.arange(quarter, dtype=F32) / quarter)
    rows = n_lat // GRID_W
    row = jnp.repeat(jnp.arange(rows, dtype=F32), GRID_W)
    col = jnp.tile(jnp.arange(GRID_W, dtype=F32), rows)
    ang_r = row[:, None] * inv_freq
    ang_c = col[:, None] * inv_freq
    ang = jnp.concatenate([ang_r, ang_r, ang_c, ang_c], axis=-1)
    reps = LANES // DA_DIM
    cos = jnp.tile(jnp.cos(ang), (1, reps))
    sin = jnp.tile(jnp.sin(ang), (1, reps))
    upper = (jnp.arange(LANES) % (2 * quarter)) >= quarter
    sin_a = jnp.where(upper, sin, 0.0)
    sin_b = jnp.where(upper, 0.0, -sin)
    n_ctx = n_tot - n_lat
    ext = lambda t, fill: jnp.concatenate([t, jnp.full((n_ctx, LANES), fill, F32)], axis=0)
    return ext(cos, 1.0), ext(sin_a, 0.0), ext(sin_b, 0.0)


def kernel(x, c, ctx, c_ctx, ada_w, ada_b, ln1_g, ln1_b, ln2_g, ln2_b, ffn_w_gate, ffn_w_up, ffn_w_down,
           dn_w_in, dn_conv, dn_a_log, dn_dt_bias, dn_norm_g, dn_w_out, da_w_in, da_lambda, da_subln_g,
           da_w_out):
    batch, n_lat, d = x.shape
    n_ctx = ctx.shape[1]
    n_tot = n_lat + n_ctx
    depth = ada_w.shape[0]
    assert depth == 2, "layer 0 is gated DeltaNet, layer 1 (the last) differential attention"
    assert batch < MOD_ROWS and n_lat % ROW_TILE == 0 and n_ctx % ROW_TILE == 0
    alpha = (2.0 * depth) ** 0.25
    hk = DN_HEADS * DN_DK

    cs = jnp.concatenate([c, c_ctx[None, :], jnp.zeros((MOD_ROWS - batch - 1, d), F32)], axis=0)
    mods = _ada_modulation(cs, ada_w, ada_b).reshape(depth, MOD_ROWS, 1, 6 * d)

    mod = mods[0]
    gate_cols = dn_w_in[0][:, 4 * hk:]
    w_gates = jnp.concatenate([gate_cols, jnp.zeros((d, LANES - gate_cols.shape[1]), F32)], axis=1)
    lane_pad = lambda v: jnp.concatenate([v.reshape(1, -1), jnp.zeros((1, LANES - v.size), F32)], axis=1)
    qkv, z, gcol, grow = _dn_project(x, ctx, mod, dn_w_in[0][:, :4 * hk].astype(BF16), w_gates, dn_conv[0],
                                     lane_pad(dn_a_log[0]), lane_pad(dn_dt_bias[0]))
    o_fwd, o_bwd = _dn_scan(qkv, gcol, grow, n_lat)
    vec = lambda v: v.reshape(1, -1)
    ffn16 = lambda i: tuple(_to_bf16(w, i) for w in (ffn_w_gate, ffn_w_up, ffn_w_down))
    xc = _sublayers((o_fwd, o_bwd, z, vec(dn_norm_g[0])), (x, ctx), mod, _to_bf16(dn_w_out, 0),
                    (vec(ln1_g[0]), vec(ln1_b[0])), ffn16(0), (vec(ln2_g[0]), vec(ln2_b[0])),
                    alpha, n_lat, n_tot, True)

    mod = mods[1]
    lambda_init = 0.8 - 0.6 * math.exp(-0.3 * 1)
    cos_t, sin_a, sin_b = _rope_tables(n_lat, n_tot)
    qkv16 = _da_project(xc, mod, da_w_in, cos_t, sin_a, sin_b, n_lat)
    y = _da_attention(qkv16, da_lambda[0], da_subln_g[0].reshape(1, -1), lambda_init, n_lat)
    return _sublayers((y,), (xc,), mod, _to_bf16(da_w_out, 0), (vec(ln1_g[1]), vec(ln1_b[1])), ffn16(1),
                      (vec(ln2_g[1]), vec(ln2_b[1])), alpha, n_lat, n_lat, False)
```

```python
import functools
import math

import jax
import jax.numpy as jnp
from jax import lax
from jax.experimental import pallas as pl
from jax.experimental.pallas import tpu as pltpu

F32 = jnp.float32
BF16 = jnp.bfloat16

DN_HEADS = 8
DN_DK = 128
DN_DV = 128
DN_CONV = 5
DN_CHUNK = 64
DA_HEADS = 8
DA_DIM = 64
GRID_W = 64
ROPE_BASE = 10000.0
LN_EPS = 1e-5
RMS_EPS = 1e-6

LANES = 128
SUBLANES = 8
MOD_ROWS = 16
ROW_TILE = 256
WIDE_ROW_TILE = 512
PROJ_ROW_BLOCK = 256
COL_TILE = 256
DA_COL_TILE = 512
ADA_COL_TILE = 1536
ATT_Q_TILE = 2048
ATT_SUB_ROWS = 256
ATT_LOOKAHEAD = 2
LOG2E = math.log2(math.e)


VMEM_LIMIT_MIB = {
    "to_bf16": 40,
    "ada_modulation": 40,
    "dn_project": 56,
    "dn_scan": 48,
    "sublayers": 56,
    "da_project": 56,
    "da_attention": 48,
}


def _cparams(name, n_grid_axes):
    return pltpu.CompilerParams(dimension_semantics=("arbitrary",) * n_grid_axes,
                                vmem_limit_bytes=VMEM_LIMIT_MIB[name] * 1024 * 1024)


def _silu(x):
    return x * jax.nn.sigmoid(x)


def _dot(a, b):
    return jnp.dot(a, b, preferred_element_type=F32)


def _dot_nt(a, b):
    return lax.dot_general(a, b, (((1,), (1,)), ((), ())), preferred_element_type=F32)


def _layernorm(x, g, b):
    mu = jnp.mean(x, axis=-1, keepdims=True)
    xc = x - mu
    var = jnp.mean(xc * xc, axis=-1, keepdims=True)
    return xc * lax.rsqrt(var + LN_EPS) * g + b


def _cast_kernel(w_ref, o_ref):
    o_ref[...] = w_ref[...].astype(BF16)


def _to_bf16(w, layer):
    _, rows, cols = w.shape
    tr = rows // 2
    return pl.pallas_call(
        _cast_kernel,
        grid=(rows // tr,),
        in_specs=[pl.BlockSpec((1, tr, cols), lambda r: (layer, r, 0))],
        out_specs=pl.BlockSpec((1, tr, cols), lambda r: (0, r, 0)),
        out_shape=jax.ShapeDtypeStruct((1, rows, cols), BF16),
        compiler_params=_cparams("to_bf16", 1),
        name="to_bf16",
    )(w).reshape(rows, cols)


def _ada_kernel(cs_ref, w_ref, b_ref, o_ref):
    s = _silu(cs_ref[...]).astype(BF16)
    o_ref[0] = _dot(s, w_ref[0].astype(BF16)) + b_ref[0]


def _ada_modulation(cs, ada_w, ada_b):
    depth, d, n = ada_w.shape
    tn = ADA_COL_TILE if n % ADA_COL_TILE == 0 else n
    return pl.pallas_call(
        _ada_kernel,
        grid=(depth, n // tn),
        in_specs=[
            pl.BlockSpec((MOD_ROWS, d), lambda i, j: (0, 0)),
            pl.BlockSpec((1, d, tn), lambda i, j: (i, 0, j)),
            pl.BlockSpec((1, 1, tn), lambda i, j: (i, 0, j)),
        ],
        out_specs=pl.BlockSpec((1, MOD_ROWS, tn), lambda i, j: (i, 0, j)),
        out_shape=jax.ShapeDtypeStruct((depth, MOD_ROWS, n), F32),
        compiler_params=_cparams("ada_modulation", 2),
        name="ada_modulation",
    )(cs, ada_w, ada_b.reshape(depth, 1, n))


def _modulate_into(xb_ref, lat, ctx, sc_l, sh_l, sc_c, sh_c):
    n_lat = lat.shape[0]
    xb_ref[:n_lat, :] = (lat * (1.0 + sc_l[0]) + sh_l[0]).astype(BF16)
    xb_ref[n_lat:, :] = (ctx * (1.0 + sc_c[0]) + sh_c[0]).astype(BF16)


def _row_blocks(n_lat, n_tot):
    blocks = []
    for seg0, seg in ((0, n_lat), (n_lat, n_tot - n_lat)):
        size = PROJ_ROW_BLOCK if seg % PROJ_ROW_BLOCK == 0 else seg
        blocks += [(seg0 + r, size) for r in range(0, seg, size)]
    return blocks


def _pipelined(blocks, produce, consume):
    pending = None
    for blk in blocks:
        cur = produce(*blk)
        if pending is not None:
            consume(*pending)
        pending = blk + (cur,)
    consume(*pending)


def _mod_specs(batch, d, sc_idx, sh_idx):
    return [
        pl.BlockSpec((1, 1, d), lambda b, j: (b, 0, sc_idx)),
        pl.BlockSpec((1, 1, d), lambda b, j: (b, 0, sh_idx)),
        pl.BlockSpec((1, 1, d), lambda b, j: (batch, 0, sc_idx)),
        pl.BlockSpec((1, 1, d), lambda b, j: (batch, 0, sh_idx)),
    ]


def _dn_proj_kernel(x_ref, c_ref, sc_l, sh_l, sc_c, sh_c, w_ref, wg_ref, conv_ref, alog_ref, dt_ref,
                    qkv_ref, z_ref, gcol_ref, grow_ref, xb_ref, pad_ref,
                    *, n_lat, n_tot, n_qkv_tiles, n_z_tiles):
    j = pl.program_id(1)
    n_ctx = n_tot - n_lat
    halo = SUBLANES
    pad = (DN_CONV - 1) // 2
    lat0 = halo
    ctx0 = 2 * halo + n_lat

    @pl.when(j == 0)
    def _():
        _modulate_into(xb_ref, x_ref[0], c_ref[0], sc_l, sh_l, sc_c, sh_c)

    blocks = _row_blocks(n_lat, n_tot)

    def projector():
        w16 = w_ref[0].astype(BF16)
        return lambda r0, rows: _dot(xb_ref[r0:r0 + rows, :], w16)

    @pl.when(j < n_qkv_tiles)
    def _():
        project = projector()
        zeros = jnp.zeros((halo, COL_TILE), F32)
        pad_ref[0:halo, :] = zeros
        pad_ref[lat0 + n_lat:ctx0, :] = zeros
        pad_ref[ctx0 + n_ctx:ctx0 + n_ctx + halo, :] = zeros
        cw = conv_ref[...]
        n_qk_tiles = 2 * n_qkv_tiles // 3
        is_q = j < n_qk_tiles // 2
        is_qk = j < n_qk_tiles

        def padded(r0):
            return r0 + (lat0 if r0 < n_lat else ctx0 - n_lat)

        def stage(r0, rows):
            pad_ref[padded(r0):padded(r0) + rows, :] = project(r0, rows)

        def conv_norm(r0, rows, _):
            base = padded(r0)
            ext = pad_ref[base - halo:base + rows + halo, :]
            y = None
            for t in range(DN_CONV):
                shifted = ext if t == pad else pltpu.roll(ext, (pad - t) % ext.shape[0], 0)
                term = shifted[halo:halo + rows, :] * cw[t:t + 1, :]
                y = term if y is None else y + term
            y = _silu(y)
            for h0 in range(0, COL_TILE, DN_DK):
                yh = y[:, h0:h0 + DN_DK]
                ss = jnp.sum(yh * yh, axis=-1, keepdims=True)
                fac = lax.rsqrt(ss + RMS_EPS)
                fac = jnp.where(is_q, fac * (DN_DK ** -0.5), fac)
                fac = jnp.where(is_qk, fac, jnp.ones_like(fac))
                qkv_ref[0, r0:r0 + rows, h0:h0 + DN_DK] = yh * fac

        _pipelined(blocks, stage, conv_norm)

    @pl.when(jnp.logical_and(j >= n_qkv_tiles, j < n_qkv_tiles + n_z_tiles))
    def _():
        project = projector()
        for r0, rows in blocks:
            z_ref[0, r0:r0 + rows, :] = project(r0, rows)

    @pl.when(j == n_qkv_tiles + n_z_tiles)
    def _():
        t = _dot(xb_ref[...], wg_ref[...].astype(BF16))
        lane = lax.broadcasted_iota(jnp.int32, t.shape, 1)
        x = t + dt_ref[...]
        softplus = jnp.maximum(x, 0.0) + jnp.log1p(jnp.exp(-jnp.abs(x)))
        decay = -jnp.exp(alog_ref[...]) * softplus
        gates = jnp.where(lane < 2 * DN_HEADS, decay, jax.nn.sigmoid(t))
        ri = lax.broadcasted_iota(jnp.int32, (DN_CHUNK, DN_CHUNK), 0)
        ci = lax.broadcasted_iota(jnp.int32, (DN_CHUNK, DN_CHUNK), 1)
        tri = jnp.concatenate([ri >= ci, ri <= ci], axis=0).astype(BF16)
        hi = gates.astype(BF16)
        rest = gates - hi.astype(F32)
        mid = rest.astype(BF16)
        low = (rest - mid.astype(F32)).astype(BF16)
        parts = jnp.concatenate([hi, mid, low], axis=1)
        lane_c = lax.broadcasted_iota(jnp.int32, (DN_CHUNK, LANES), 1)
        for c in range(n_tot // DN_CHUNK):
            gc = gates[c * DN_CHUNK:(c + 1) * DN_CHUNK]
            cum3 = _dot(tri, parts[c * DN_CHUNK:(c + 1) * DN_CHUNK])
            cum = cum3[:, :LANES] + cum3[:, LANES:2 * LANES] + cum3[:, 2 * LANES:]
            col = jnp.where(lane_c < DN_HEADS, cum[:DN_CHUNK],
                            jnp.where(lane_c < 2 * DN_HEADS, cum[DN_CHUNK:], gc))
            gcol_ref[0, c * DN_CHUNK:(c + 1) * DN_CHUNK, :] = col
            grow_ref[0, c] = jnp.concatenate([col, col], axis=0).T


def _dn_project(x, ctx, mod, w_in, w_gates, conv_w, alog_row, dt_row):
    batch, n_lat, d = x.shape
    n_tot = n_lat + ctx.shape[1]
    hk = DN_HEADS * DN_DK
    n_qkv_tiles = 3 * hk // COL_TILE
    n_z_tiles = hk // COL_TILE
    n_tiles = n_qkv_tiles + n_z_tiles + 1
    nc = n_tot // DN_CHUNK
    kern = functools.partial(_dn_proj_kernel, n_lat=n_lat, n_tot=n_tot,
                             n_qkv_tiles=n_qkv_tiles, n_z_tiles=n_z_tiles)
    return pl.pallas_call(
        kern,
        grid=(batch, n_tiles),
        in_specs=[pl.BlockSpec((1, n_lat, d), lambda b, j: (b, 0, 0)),
                  pl.BlockSpec((1, n_tot - n_lat, d), lambda b, j: (b, 0, 0))]
        + _mod_specs(batch, d, 1, 0)
        + [
            pl.BlockSpec((1, d, COL_TILE), lambda b, j: (0, 0, jnp.minimum(j, n_tiles - 2))),
            pl.BlockSpec((d, LANES), lambda b, j: (0, 0)),
            pl.BlockSpec((DN_CONV, COL_TILE), lambda b, j: (0, jnp.minimum(j, n_qkv_tiles - 1))),
            pl.BlockSpec((1, LANES), lambda b, j: (0, 0)),
            pl.BlockSpec((1, LANES), lambda b, j: (0, 0)),
        ],
        out_specs=[
            pl.BlockSpec((1, n_tot, COL_TILE), lambda b, j: (b, 0, jnp.minimum(j, n_qkv_tiles - 1))),
            pl.BlockSpec((1, n_tot, COL_TILE),
                         lambda b, j: (b, 0, jnp.clip(j - n_qkv_tiles, 0, n_z_tiles - 1))),
            pl.BlockSpec((1, n_tot, LANES), lambda b, j: (b, 0, 0)),
            pl.BlockSpec((1, nc, LANES, LANES), lambda b, j: (b, 0, 0, 0)),
        ],
        out_shape=[
            jax.ShapeDtypeStruct((batch, n_tot, 3 * hk), F32),
            jax.ShapeDtypeStruct((batch, n_tot, hk), F32),
            jax.ShapeDtypeStruct((batch, n_tot, LANES), F32),
            jax.ShapeDtypeStruct((batch, nc, LANES, LANES), F32),
        ],
        scratch_shapes=[
            pltpu.VMEM((n_tot, d), BF16),
            pltpu.VMEM((n_tot + 3 * SUBLANES, COL_TILE), F32),
        ],
        compiler_params=_cparams("dn_project", 2),
        name="dn_project",
    )(x, ctx, mod, mod, mod, mod, w_in, w_gates, conv_w, alog_row, dt_row)


INV_BASE = 16
SCAN_BATCH = 2
SCAN_UNITS = SCAN_BATCH * DN_HEADS


def _side_by_side(fwd, bwd):
    return jnp.concatenate([fwd, bwd], axis=1)


def _block_diag16(top, bottom):
    top, bottom = top.astype(BF16), bottom.astype(BF16)
    return jnp.concatenate([_side_by_side(top, jnp.zeros((top.shape[0], bottom.shape[1]), BF16)),
                            _side_by_side(jnp.zeros((bottom.shape[0], top.shape[1]), BF16), bottom)], axis=0)


def _dn_local_stages(ins, slot, sidx, half):
    u_ref, wq_ref, m_ref, e_ref = slot
    c = DN_CHUNK
    st = [dict() for _ in range(SCAN_UNITS)]

    def each(fn):
        def run():
            for n, t in enumerate(st):
                fn(t, n)
        return run

    def grids():
        ri = lax.broadcasted_iota(jnp.int32, (c, 2 * c), 0)
        lane = lax.broadcasted_iota(jnp.int32, (c, 2 * c), 1)
        return ri, lane & (c - 1), lane < c

    def pair_dot(x, y):
        _, _, fwd_half = grids()
        rhs = jnp.concatenate([jnp.where(fwd_half, y, 0.0).astype(BF16),
                               jnp.where(fwd_half, 0.0, y).astype(BF16)], axis=0)
        return _dot(x.astype(BF16), rhs)

    def load(t, n):
        bb, hh = divmod(n, DN_HEADS)
        ri, cj, fwd_half = grids()
        lane = lax.broadcasted_iota(jnp.int32, (c, LANES), 1)
        cols = slice(hh * DN_DK, (hh + 1) * DN_DK)
        per_dir = []
        for d in range(2):
            q_ref, k_ref, v_ref, gcol_ref, grow_ref = ins[d]
            part = half if d == 0 else 1 - half
            rows = slice(part * c, (part + 1) * c)
            gates = gcol_ref[bb, rows, :]
            g_lane = d * DN_HEADS + hh
            gcum = jnp.sum(jnp.where(lane == g_lane, gates, 0.0), axis=-1, keepdims=True)
            beta = jnp.sum(jnp.where(lane == 2 * DN_HEADS + g_lane, gates, 0.0), axis=-1, keepdims=True)
            grow = grow_ref[bb, part, g_lane:g_lane + 1, :]
            g_last = grow[:, c - 1:c] if d == 0 else grow[:, 0:1]
            per_dir.append((q_ref[bb, rows, cols], k_ref[bb, rows, cols], v_ref[bb, rows, cols],
                            gcum, beta, grow, g_last))
        (qf, kf, vf, gcf, bf, grf, glf), (qb, kb, vb, gcb, bb, grb, glb) = per_dir
        incl = jnp.logical_or(ri == cj, (ri > cj) == fwd_half)
        diff = jnp.where(fwd_half, gcf, gcb) - jnp.where(fwd_half[0:1], grf, grb)
        t["gamma"] = jnp.where(incl, jnp.exp(jnp.where(incl, diff, 0.0)), 0.0)
        kbf, kbb = kf * bf, kb * bb
        dec_f, dec_b = jnp.exp(gcf), jnp.exp(gcb)
        t["lhs16"] = _side_by_side(jnp.concatenate([kbf, qf], axis=0).astype(BF16),
                                   jnp.concatenate([kbb, qb], axis=0).astype(BF16))
        t["keys16"] = _block_diag16(kf, kb)
        t["rhs16"] = _block_diag16(_side_by_side(vf * bf, kbf * dec_f), _side_by_side(vb * bb, kbb * dec_b))
        wq_ref[sidx, n, c:, :DN_DK] = (qf * dec_f).astype(BF16)
        wq_ref[sidx, n, c:, DN_DK:] = (qb * dec_b).astype(BF16)
        k_dec = jnp.concatenate([kf * jnp.exp(glf - gcf), kb * jnp.exp(glb - gcb)], axis=0)
        m_ref[sidx, n, c:, :] = k_dec.T.astype(BF16)
        e_ref[sidx, n] = _side_by_side(jnp.broadcast_to(jnp.exp(glf), (SUBLANES, DN_DV)),
                                        jnp.broadcast_to(jnp.exp(glb), (SUBLANES, DN_DV)))

    def gram(t, n):
        t["kq"] = _dot_nt(t.pop("lhs16"), t.pop("keys16"))

    def split(t, n):
        ri, cj, fwd_half = grids()
        strict = jnp.logical_and(ri != cj, (ri > cj) == fwd_half)
        kq = t.pop("kq")
        gamma = t.pop("gamma")
        a = jnp.where(strict, kq[:c] * gamma, 0.0)
        m_ref[sidx, n, :c, :] = (kq[c:] * gamma).astype(BF16)
        a_diag = jnp.where((ri // INV_BASE) == (cj // INV_BASE), a, 0.0)
        t["a"] = a
        t["inv"] = (ri == cj).astype(F32) - a_diag
        t["pw"] = pair_dot(a_diag, a_diag)

    def series(t, n):
        both = pair_dot(jnp.concatenate([t["pw"], t["inv"]], axis=0), t["pw"])
        t["pw"] = both[:c]
        t["inv"] = t["inv"] + both[c:]

    def series_last(t, n):
        t["inv"] = t["inv"] + pair_dot(t["inv"], t.pop("pw"))

    def merge_a(blk):
        def fn(t, n):
            ri, cj, _ = grids()
            off = jnp.logical_and((ri // (2 * blk)) == (cj // (2 * blk)), (ri // blk) != (cj // blk))
            t["y"] = pair_dot(jnp.where(off, t["a"], 0.0), t["inv"])
        return fn

    def merge_b(t, n):
        t["inv"] = t["inv"] - pair_dot(t["inv"], t.pop("y"))

    def solve(t, n):
        t["uw"] = _dot(t.pop("inv").astype(BF16), t.pop("rhs16"))
        t.pop("a")

    def store(t, n):
        uw = t.pop("uw")
        for d in range(2):
            base = d * (DN_DV + DN_DK)
            u_ref[sidx, n, :, d * DN_DV:(d + 1) * DN_DV] = uw[:, base:base + DN_DV]
            wq_ref[sidx, n, :c, d * DN_DK:(d + 1) * DN_DK] = uw[:, base + DN_DV:base + DN_DV + DN_DK].astype(BF16)

    stages = [load, gram, split] + [series] * (int(math.log2(INV_BASE)) - 2) + [series_last]
    blk = INV_BASE
    while blk < c:
        stages += [merge_a(blk), merge_b]
        blk *= 2
    stages += [solve, store]
    return [each(fn) for fn in stages]


def _dn_serial_stages(outs, s_ref, slot, sidx, half):
    u_ref, wq_ref, m_ref, e_ref = slot
    c = DN_CHUNK
    st = [dict() for _ in range(SCAN_UNITS)]

    def each(fn):
        def run():
            for n, t in enumerate(st):
                fn(t, n)
        return run

    def state_dot(t, n):
        t["s"] = (s_ref[n], s_ref[SCAN_UNITS + n])
        t["ws"] = _dot(wq_ref[sidx, n], _block_diag16(*t["s"]))

    def value_dot(t, n):
        v_new = u_ref[sidx, n] - t["ws"][:c]
        t["mv"] = _dot(m_ref[sidx, n], _block_diag16(v_new[:, :DN_DV], v_new[:, DN_DV:]))

    def update(t, n):
        bb, hh = divmod(n, DN_HEADS)
        mv, ws, s = t.pop("mv"), t.pop("ws"), t.pop("s")
        gain = e_ref[sidx, n][0:1, :]
        for d in range(2):
            lanes = slice(d * DN_DV, (d + 1) * DN_DV)
            s_ref[d * SCAN_UNITS + n] = s[d] * gain[:, lanes] + mv[c:, lanes]
            part = half if d == 0 else 1 - half
            outs[d][bb, part * c:(part + 1) * c, hh * DN_DV:(hh + 1) * DN_DV] = ws[c:, lanes] + mv[:c, lanes]

    return [each(fn) for fn in (state_dot, value_dot, update)]


def _dn_scan_kernel(*refs):
    ins = (refs[0:5], refs[5:10])
    outs = refs[10:12]
    s_ref = refs[12]
    slot_sets = (refs[13:17], refs[17:21])
    g = pl.program_id(1)

    @pl.when(g == 0)
    def _():
        for ref in (s_ref,) + tuple(slot_sets[0]) + tuple(slot_sets[1]):
            ref[...] = jnp.zeros(ref.shape, ref.dtype)

    def step(written, read):
        loc = [_dn_local_stages(ins, written, half, half) for half in range(2)]
        ser = [stage for half in range(2) for stage in _dn_serial_stages(outs, s_ref, read, half, half)]
        gap = len(loc[0]) // len(ser)
        for n, stages in enumerate(zip(*loc)):
            if n % gap == 0 and n // gap < len(ser):
                ser[n // gap]()
            for stage in stages:
                stage()

    for parity in range(2):
        pl.when(g % 2 == parity)(functools.partial(step, slot_sets[parity], slot_sets[1 - parity]))


def _dn_scan(qkv, gcol, grow, n_lat):
    batch, n_tot, _ = qkv.shape
    hk = DN_HEADS * DN_DK
    pair = 2 * DN_CHUNK
    n_pairs = n_tot // pair
    lat_pairs = n_lat // pair
    ctx_pairs = n_pairs - lat_pairs
    assert n_lat % pair == 0 and n_tot % pair == 0 and batch % SCAN_BATCH == 0

    def fwd(p):
        return jnp.where(p < ctx_pairs, p + lat_pairs, p - ctx_pairs)

    def bwd(p):
        return n_pairs - 1 - p

    def p_in(g):
        return jnp.minimum(g, n_pairs - 1)

    def p_out(g):
        return jnp.maximum(g - 1, 0)

    def in_specs(where):
        return [pl.BlockSpec((SCAN_BATCH, pair, hk), lambda b, g, col=col: (b, where(p_in(g)), col))
                for col in range(3)] + [
            pl.BlockSpec((SCAN_BATCH, pair, LANES), lambda b, g: (b, where(p_in(g)), 0)),
            pl.BlockSpec((SCAN_BATCH, 2, LANES, LANES), lambda b, g: (b, where(p_in(g)), 0, 0)),
        ]

    out_spec = lambda where: pl.BlockSpec((SCAN_BATCH, pair, hk), lambda b, g: (b, where(p_out(g)), 0))
    out_shape = jax.ShapeDtypeStruct((batch, n_tot, hk), F32)
    slot = [pltpu.VMEM((2, SCAN_UNITS, DN_CHUNK, 2 * DN_DV), F32),
            pltpu.VMEM((2, SCAN_UNITS, 2 * DN_CHUNK, 2 * DN_DK), BF16),
            pltpu.VMEM((2, SCAN_UNITS, DN_CHUNK + DN_DK, 2 * DN_CHUNK), BF16),
            pltpu.VMEM((2, SCAN_UNITS, SUBLANES, 2 * DN_DV), F32)]
    per_dir = (qkv, qkv, qkv, gcol, grow)
    return pl.pallas_call(
        _dn_scan_kernel,
        grid=(batch // SCAN_BATCH, n_pairs + 1),
        in_specs=in_specs(fwd) + in_specs(bwd),
        out_specs=[out_spec(fwd), out_spec(bwd)],
        out_shape=[out_shape, out_shape],
        scratch_shapes=[pltpu.VMEM((2 * SCAN_UNITS, DN_DK, DN_DV), F32)] + slot + slot,
        compiler_params=_cparams("dn_scan", 2),
        name="dn_scan",
    )(*per_dir, *per_dir)


def _sublayers_kernel(*refs, alpha, gated_norm, split_residual, n_lat_tiles):
    if gated_norm:
        y_ref, y2_ref, z_ref, ng_ref = refs[:4]
    else:
        y_ref = refs[0]
    (gt1_ref, sc2_ref, sh2_ref, gt2_ref, wo_ref, l1g_ref, l1b_ref,
     wg_ref, wu_ref, wd_ref, l2g_ref, l2b_ref, o_ref, yb_ref) = refs[-14:]
    if split_residual:
        x_ref, c_ref = refs[-16:-14]
        residual = jnp.where(pl.program_id(1) < n_lat_tiles, x_ref[0], c_ref[0])
    else:
        residual = refs[-15][0]
    if gated_norm:
        for h0 in range(0, y_ref.shape[2], DN_DV):
            yh = y_ref[0, :, h0:h0 + DN_DV] + y2_ref[0, :, h0:h0 + DN_DV]
            ms = jnp.mean(yh * yh, axis=-1, keepdims=True)
            yh = yh * lax.rsqrt(ms + RMS_EPS) * ng_ref[...]
            yb_ref[:, h0:h0 + DN_DV] = (yh * _silu(z_ref[0, :, h0:h0 + DN_DV])).astype(BF16)
        mixed = yb_ref[...]
    else:
        mixed = y_ref[0]
    x1 = _layernorm(alpha * residual + gt1_ref[0] * _dot(mixed, wo_ref[...]), l1g_ref[...], l1b_ref[...])
    h = (x1 * (1.0 + sc2_ref[0]) + sh2_ref[0]).astype(BF16)
    gate = _dot(h, wg_ref[...])
    up = _dot(h, wu_ref[...])
    act = (_silu(gate) * up).astype(BF16)
    o_ref[0] = _layernorm(alpha * x1 + gt2_ref[0] * _dot(act, wd_ref[...]), l2g_ref[...], l2b_ref[...])


def _sublayers(ys, residual, mod, w_out16, ln1, ffn16, ln2, alpha, n_lat, n_out, gated_norm):
    batch, _, d = residual[0].shape
    hv = ys[0].shape[2]
    split = len(residual) == 2
    tm = WIDE_ROW_TILE if not split and n_out % WIDE_ROW_TILE == 0 and n_lat % WIDE_ROW_TILE == 0 else ROW_TILE
    n_lat_tiles = n_lat // tm
    kern = functools.partial(_sublayers_kernel, alpha=alpha, gated_norm=gated_norm, split_residual=split,
                             n_lat_tiles=n_lat_tiles)
    tok = lambda width: pl.BlockSpec((1, tm, width), lambda b, t: (b, t, 0))
    res_specs = [tok(d)] if not split else [
        pl.BlockSpec((1, tm, d), lambda b, t: (b, jnp.minimum(t, n_lat_tiles - 1), 0)),
        pl.BlockSpec((1, tm, d), lambda b, t: (b, jnp.maximum(t - n_lat_tiles, 0), 0))]
    mod_spec = lambda idx: pl.BlockSpec((1, 1, d), lambda b, t: (jnp.where(t < n_lat_tiles, b, batch), 0, idx))
    resident = pl.BlockSpec(memory_space=pltpu.VMEM)
    vec = pl.BlockSpec((1, d), lambda b, t: (0, 0))
    return pl.pallas_call(
        kern,
        grid=(batch, n_out // tm),
        in_specs=([tok(hv)] * 3 + [pl.BlockSpec((1, ys[3].shape[1]), lambda b, t: (0, 0))]
                  if gated_norm else [tok(hv)]) + res_specs
        + [mod_spec(2), mod_spec(4), mod_spec(3), mod_spec(5), resident, vec, vec, resident, resident, resident,
           vec, vec],
        out_specs=tok(d),
        out_shape=jax.ShapeDtypeStruct((batch, n_out, d), F32),
        scratch_shapes=[pltpu.VMEM((tm, hv), BF16)],
        compiler_params=_cparams("sublayers", 2),
        name="sublayers",
    )(*ys, *residual, mod, mod, mod, mod, w_out16, *ln1, *ffn16, *ln2)


def _da_proj_kernel(x_ref, sc_l, sh_l, sc_c, sh_c, w_ref, cos_ref, sin_a_ref, sin_b_ref,
                    o_ref, xb_ref, *, n_lat, n_qk_tiles):
    j = pl.program_id(1)

    @pl.when(j == 0)
    def _():
        _modulate_into(xb_ref, x_ref[0, :n_lat, :], x_ref[0, n_lat:, :], sc_l, sh_l, sc_c, sh_c)

    blocks = _row_blocks(n_lat, x_ref.shape[1])

    def projector():
        w16 = w_ref[0].astype(BF16)
        return lambda r0, rows: _dot(xb_ref[r0:r0 + rows, :], w16)

    @pl.when(j < n_qk_tiles)
    def _():
        project = projector()
        scale = jnp.where(j < n_qk_tiles // 2, DA_DIM ** -0.5 * LOG2E, 1.0)
        quarter = DA_DIM // 4

        def rope(r0, rows, acc):
            cos, sin_a, sin_b = (t[r0:r0 + rows, :] for t in (cos_ref, sin_a_ref, sin_b_ref))
            for h0 in range(0, DA_COL_TILE, LANES):
                xh = acc[:, h0:h0 + LANES]
                rot = pltpu.roll(xh, quarter, 1) * sin_a + pltpu.roll(xh, LANES - quarter, 1) * sin_b
                o_ref[0, r0:r0 + rows, h0:h0 + LANES] = ((xh * cos + rot) * scale).astype(BF16)

        _pipelined(blocks, project, rope)

    @pl.when(j >= n_qk_tiles)
    def _():
        project = projector()
        for r0, rows in blocks:
            o_ref[0, r0:r0 + rows, :] = project(r0, rows).astype(BF16)


def _da_project(xc, mod, w_in, cos_t, sin_a, sin_b, n_lat):
    batch, n_tot, d = xc.shape
    n_cols = w_in.shape[2]
    n_tiles = n_cols // DA_COL_TILE
    n_qk_tiles = 2 * n_tiles // 3
    kern = functools.partial(_da_proj_kernel, n_lat=n_lat, n_qk_tiles=n_qk_tiles)
    table = pl.BlockSpec((n_tot, LANES), lambda b, j: (0, 0))
    return pl.pallas_call(
        kern,
        grid=(batch, n_tiles),
        in_specs=[pl.BlockSpec((1, n_tot, d), lambda b, j: (b, 0, 0))]
        + _mod_specs(batch, d, 1, 0)
        + [pl.BlockSpec((1, d, DA_COL_TILE), lambda b, j: (0, 0, j)), table, table, table],
        out_specs=pl.BlockSpec((1, n_tot, DA_COL_TILE), lambda b, j: (b, 0, j)),
        out_shape=jax.ShapeDtypeStruct((batch, n_tot, n_cols), BF16),
        scratch_shapes=[pltpu.VMEM((n_tot, d), BF16)],
        compiler_params=_cparams("da_project", 2),
        name="da_project",
    )(xc, mod, mod, mod, mod, w_in, cos_t, sin_a, sin_b)


def _da_attn_kernel(q_ref, k_ref, v_ref, lam_ref, g_ref, o_ref, *, lambda_init):
    lam = lam_ref[...]
    lam_val = (jnp.exp(jnp.sum(lam[0:1] * lam[1:2], axis=-1, keepdims=True))
               - jnp.exp(jnp.sum(lam[2:3] * lam[3:4], axis=-1, keepdims=True)) + lambda_init)
    k = k_ref[0]
    n_sub = q_ref.shape[1] // ATT_SUB_ROWS
    lane = lax.broadcasted_iota(jnp.int32, (ATT_SUB_ROWS, 2 * DA_DIM), 1)

    def scores(r):
        q = q_ref[0, r * ATT_SUB_ROWS:(r + 1) * ATT_SUB_ROWS, :]
        zero = jnp.zeros_like(q)
        return [_dot_nt(jnp.where((lane < DA_DIM) if comp == 0 else (lane >= DA_DIM), q, zero), k)
                for comp in range(2)]

    def weights(s):
        e0, e1 = (jnp.exp2(sc - jnp.max(sc, axis=-1, keepdims=True)) for sc in s)
        l0 = jnp.sum(e0, axis=-1, keepdims=True)
        l1 = jnp.sum(e1, axis=-1, keepdims=True)
        return (e0 - e1 * (lam_val * l0 / l1)).astype(BF16), 1.0 / l0

    def values(r, a16, inv_l0):
        o = _dot(a16, v_ref[0]) * inv_l0
        ms = jnp.mean(o * o, axis=-1, keepdims=True)
        o_ref[0, r * ATT_SUB_ROWS:(r + 1) * ATT_SUB_ROWS, :] = (
            o * lax.rsqrt(ms + RMS_EPS) * g_ref[...] * (1.0 - lambda_init)).astype(BF16)

    pending = [scores(r) for r in range(min(ATT_LOOKAHEAD, n_sub))]
    for r in range(n_sub):
        if r + ATT_LOOKAHEAD < n_sub:
            pending.append(scores(r + ATT_LOOKAHEAD))
        values(r, *weights(pending.pop(0)))


def _da_attention(qkv16, lam, subln_g, lambda_init, n_lat):
    batch, n_tot, _ = qkv16.shape
    hw = 2 * DA_DIM
    kern = functools.partial(_da_attn_kernel, lambda_init=lambda_init)
    tq = ATT_Q_TILE if n_lat % ATT_Q_TILE == 0 else ROW_TILE
    return pl.pallas_call(
        kern,
        grid=(batch, DA_HEADS, n_lat // tq),
        in_specs=[
            pl.BlockSpec((1, tq, hw), lambda b, h, i: (b, i, h)),
            pl.BlockSpec((1, n_tot, hw), lambda b, h, i: (b, 0, DA_HEADS + h)),
            pl.BlockSpec((1, n_tot, hw), lambda b, h, i: (b, 0, 2 * DA_HEADS + h)),
            pl.BlockSpec(lam.shape, lambda b, h, i: (0, 0)),
            pl.BlockSpec((1, hw), lambda b, h, i: (0, 0)),
        ],
        out_specs=pl.BlockSpec((1, tq, hw), lambda b, h, i: (b, i, h)),
        out_shape=jax.ShapeDtypeStruct((batch, n_lat, DA_HEADS * hw), BF16),
        compiler_params=_cparams("da_attention", 3),
        name="da_attention",
    )(qkv16, qkv16, qkv16, lam, subln_g)


def _rope_tables(n_lat, n_tot):
    quarter = DA_DIM // 4
    inv_freq = ROPE_BASE ** (-jnp.arange(quarter, dtype=F32) / quarter)
    rows = n_lat // GRID_W
    row = jnp.repeat(jnp.arange(rows, dtype=F32), GRID_W)
    col = jnp.tile(jnp.arange(GRID_W, dtype=F32), rows)
    ang_r = row[:, None] * inv_freq
    ang_c = col[:, None] * inv_freq
    ang = jnp.concatenate([ang_r, ang_r, ang_c, ang_c], axis=-1)
    reps = LANES // DA_DIM
    cos = jnp.tile(jnp.cos(ang), (1, reps))
    sin = jnp.tile(jnp.sin(ang), (1, reps))
    upper = (jnp.arange(LANES) % (2 * quarter)) >= quarter
    sin_a = jnp.where(upper, sin, 0.0)
    sin_b = jnp.where(upper, 0.0, -sin)
    n_ctx = n_tot - n_lat
    ext = lambda t, fill: jnp.concatenate([t, jnp.full((n_ctx, LANES), fill, F32)], axis=0)
    return ext(cos, 1.0), ext(sin_a, 0.0), ext(sin_b, 0.0)


def kernel(x, c, ctx, c_ctx, ada_w, ada_b, ln1_g, ln1_b, ln2_g, ln2_b, ffn_w_gate, ffn_w_up, ffn_w_down,
           dn_w_in, dn_conv, dn_a_log, dn_dt_bias, dn_norm_g, dn_w_out, da_w_in, da_lambda, da_subln_g,
           da_w_out):
    batch, n_lat, d = x.shape
    n_ctx = ctx.shape[1]
    n_tot = n_lat + n_ctx
    depth = ada_w.shape[0]
    assert depth == 2, "layer 0 is gated DeltaNet, layer 1 (the last) differential attention"
    assert batch < MOD_ROWS and n_lat % ROW_TILE == 0 and n_ctx % ROW_TILE == 0
    alpha = (2.0 * depth) ** 0.25
    hk = DN_HEADS * DN_DK

    cs = jnp.concatenate([c, c_ctx[None, :], jnp.zeros((MOD_ROWS - batch - 1, d), F32)], axis=0)
    mods = _ada_modulation(cs, ada_w, ada_b).reshape(depth, MOD_ROWS, 1, 6 * d)

    mod = mods[0]
    gate_cols = dn_w_in[0][:, 4 * hk:]
    w_gates = jnp.concatenate([gate_cols, jnp.zeros((d, LANES - gate_cols.shape[1]), F32)], axis=1)
    lane_pad = lambda v: jnp.concatenate([v.reshape(1, -1), jnp.zeros((1, LANES - v.size), F32)], axis=1)
    qkv, z, gcol, grow = _dn_project(x, ctx, mod, dn_w_in, w_gates, dn_conv[0],
                                     lane_pad(dn_a_log[0]), lane_pad(dn_dt_bias[0]))
    o_fwd, o_bwd = _dn_scan(qkv, gcol, grow, n_lat)
    vec = lambda v: v.reshape(1, -1)
    ffn16 = lambda i: tuple(_to_bf16(w, i) for w in (ffn_w_gate, ffn_w_up, ffn_w_down))
    xc = _sublayers((o_fwd, o_bwd, z, vec(dn_norm_g[0])), (x, ctx), mod, _to_bf16(dn_w_out, 0),
                    (vec(ln1_g[0]), vec(ln1_b[0])), ffn16(0), (vec(ln2_g[0]), vec(ln2_b[0])),
                    alpha, n_lat, n_tot, True)

    mod = mods[1]
    lambda_init = 0.8 - 0.6 * math.exp(-0.3 * 1)
    cos_t, sin_a, sin_b = _rope_tables(n_lat, n_tot)
    qkv16 = _da_project(xc, mod, da_w_in, cos_t, sin_a, sin_b, n_lat)
    y = _da_attention(qkv16, da_lambda[0], da_subln_g[0].reshape(1, -1), lambda_init, n_lat)
    return _sublayers((y,), (xc,), mod, _to_bf16(da_w_out, 0), (vec(ln1_g[1]), vec(ln1_b[1])), ffn16(1),
                      (vec(ln2_g[1]), vec(ln2_b[1])), alpha, n_lat, n_lat, False)
```

```python
import functools
import math

import jax
import jax.numpy as jnp
from jax import lax
from jax.experimental import pallas as pl
from jax.experimental.pallas import tpu as pltpu

F32 = jnp.float32
BF16 = jnp.bfloat16

DN_HEADS = 8
DN_DK = 128
DN_DV = 128
DN_CONV = 5
DN_CHUNK = 64
DA_HEADS = 8
DA_DIM = 64
GRID_W = 64
ROPE_BASE = 10000.0
LN_EPS = 1e-5
RMS_EPS = 1e-6

LANES = 128
SUBLANES = 8
MOD_ROWS = 16
ROW_TILE = 256
WIDE_ROW_TILE = 512
PROJ_ROW_BLOCK = 128
COL_TILE = 256
DA_COL_TILE = 512
ADA_COL_TILE = 1536
ATT_Q_TILE = 2048
ATT_SUB_ROWS = 256
ATT_LOOKAHEAD = 2
LOG2E = math.log2(math.e)


VMEM_LIMIT_MIB = {
    "to_bf16": 40,
    "ada_modulation": 40,
    "dn_project": 56,
    "dn_scan": 48,
    "sublayers": 56,
    "da_project": 56,
    "da_attention": 48,
}


def _cparams(name, n_grid_axes):
    return pltpu.CompilerParams(dimension_semantics=("arbitrary",) * n_grid_axes,
                                vmem_limit_bytes=VMEM_LIMIT_MIB[name] * 1024 * 1024)


def _silu(x):
    return x * jax.nn.sigmoid(x)


def _dot(a, b):
    return jnp.dot(a, b, preferred_element_type=F32)


def _dot_nt(a, b):
    return lax.dot_general(a, b, (((1,), (1,)), ((), ())), preferred_element_type=F32)


def _layernorm(x, g, b):
    mu = jnp.mean(x, axis=-1, keepdims=True)
    xc = x - mu
    var = jnp.mean(xc * xc, axis=-1, keepdims=True)
    return xc * lax.rsqrt(var + LN_EPS) * g + b


def _cast_kernel(w_ref, o_ref):
    o_ref[...] = w_ref[...].astype(BF16)


def _to_bf16(w, layer):
    _, rows, cols = w.shape
    tr = rows // 2
    return pl.pallas_call(
        _cast_kernel,
        grid=(rows // tr,),
        in_specs=[pl.BlockSpec((1, tr, cols), lambda r: (layer, r, 0))],
        out_specs=pl.BlockSpec((1, tr, cols), lambda r: (0, r, 0)),
        out_shape=jax.ShapeDtypeStruct((1, rows, cols), BF16),
        compiler_params=_cparams("to_bf16", 1),
        name="to_bf16",
    )(w).reshape(rows, cols)


def _ada_kernel(cs_ref, w_ref, b_ref, o_ref):
    s = _silu(cs_ref[...]).astype(BF16)
    o_ref[0] = _dot(s, w_ref[0].astype(BF16)) + b_ref[0]


def _ada_modulation(cs, ada_w, ada_b):
    depth, d, n = ada_w.shape
    tn = ADA_COL_TILE if n % ADA_COL_TILE == 0 else n
    return pl.pallas_call(
        _ada_kernel,
        grid=(depth, n // tn),
        in_specs=[
            pl.BlockSpec((MOD_ROWS, d), lambda i, j: (0, 0)),
            pl.BlockSpec((1, d, tn), lambda i, j: (i, 0, j)),
            pl.BlockSpec((1, 1, tn), lambda i, j: (i, 0, j)),
        ],
        out_specs=pl.BlockSpec((1, MOD_ROWS, tn), lambda i, j: (i, 0, j)),
        out_shape=jax.ShapeDtypeStruct((depth, MOD_ROWS, n), F32),
        compiler_params=_cparams("ada_modulation", 2),
        name="ada_modulation",
    )(cs, ada_w, ada_b.reshape(depth, 1, n))


def _modulate_into(xb_ref, lat, ctx, sc_l, sh_l, sc_c, sh_c):
    n_lat = lat.shape[0]
    xb_ref[:n_lat, :] = (lat * (1.0 + sc_l[0]) + sh_l[0]).astype(BF16)
    xb_ref[n_lat:, :] = (ctx * (1.0 + sc_c[0]) + sh_c[0]).astype(BF16)


def _row_blocks(n_lat, n_tot):
    blocks = []
    for seg0, seg in ((0, n_lat), (n_lat, n_tot - n_lat)):
        size = PROJ_ROW_BLOCK if seg % PROJ_ROW_BLOCK == 0 else seg
        blocks += [(seg0 + r, size) for r in range(0, seg, size)]
    return blocks


def _pipelined(blocks, produce, consume):
    pending = None
    for blk in blocks:
        cur = produce(*blk)
        if pending is not None:
            consume(*pending)
        pending = blk + (cur,)
    consume(*pending)


def _mod_specs(batch, d, sc_idx, sh_idx):
    return [
        pl.BlockSpec((1, 1, d), lambda b, j: (b, 0, sc_idx)),
        pl.BlockSpec((1, 1, d), lambda b, j: (b, 0, sh_idx)),
        pl.BlockSpec((1, 1, d), lambda b, j: (batch, 0, sc_idx)),
        pl.BlockSpec((1, 1, d), lambda b, j: (batch, 0, sh_idx)),
    ]


def _dn_proj_kernel(x_ref, c_ref, sc_l, sh_l, sc_c, sh_c, w_ref, wg_ref, conv_ref, alog_ref, dt_ref,
                    qkv_ref, z_ref, gcol_ref, grow_ref, xb_ref, pad_ref,
                    *, n_lat, n_tot, n_qkv_tiles, n_z_tiles):
    j = pl.program_id(1)
    n_ctx = n_tot - n_lat
    halo = SUBLANES
    pad = (DN_CONV - 1) // 2
    lat0 = halo
    ctx0 = 2 * halo + n_lat

    @pl.when(j == 0)
    def _():
        _modulate_into(xb_ref, x_ref[0], c_ref[0], sc_l, sh_l, sc_c, sh_c)

    blocks = _row_blocks(n_lat, n_tot)

    def projector():
        w16 = w_ref[0].astype(BF16)
        return lambda r0, rows: _dot(xb_ref[r0:r0 + rows, :], w16)

    @pl.when(j < n_qkv_tiles)
    def _():
        project = projector()
        zeros = jnp.zeros((halo, COL_TILE), F32)
        pad_ref[0:halo, :] = zeros
        pad_ref[lat0 + n_lat:ctx0, :] = zeros
        pad_ref[ctx0 + n_ctx:ctx0 + n_ctx + halo, :] = zeros
        cw = conv_ref[...]
        n_qk_tiles = 2 * n_qkv_tiles // 3
        is_q = j < n_qk_tiles // 2
        is_qk = j < n_qk_tiles

        def padded(r0):
            return r0 + (lat0 if r0 < n_lat else ctx0 - n_lat)

        def stage(r0, rows):
            pad_ref[padded(r0):padded(r0) + rows, :] = project(r0, rows)

        def conv_norm(r0, rows, _):
            base = padded(r0)
            ext = pad_ref[base - halo:base + rows + halo, :]
            y = None
            for t in range(DN_CONV):
                shifted = ext if t == pad else pltpu.roll(ext, (pad - t) % ext.shape[0], 0)
                term = shifted[halo:halo + rows, :] * cw[t:t + 1, :]
                y = term if y is None else y + term
            y = _silu(y)
            for h0 in range(0, COL_TILE, DN_DK):
                yh = y[:, h0:h0 + DN_DK]
                ss = jnp.sum(yh * yh, axis=-1, keepdims=True)
                fac = lax.rsqrt(ss + RMS_EPS)
                fac = jnp.where(is_q, fac * (DN_DK ** -0.5), fac)
                fac = jnp.where(is_qk, fac, jnp.ones_like(fac))
                qkv_ref[0, r0:r0 + rows, h0:h0 + DN_DK] = yh * fac

        _pipelined(blocks, stage, conv_norm)

    @pl.when(jnp.logical_and(j >= n_qkv_tiles, j < n_qkv_tiles + n_z_tiles))
    def _():
        project = projector()
        for r0, rows in blocks:
            z_ref[0, r0:r0 + rows, :] = project(r0, rows)

    @pl.when(j == n_qkv_tiles + n_z_tiles)
    def _():
        t = _dot(xb_ref[...], wg_ref[...].astype(BF16))
        lane = lax.broadcasted_iota(jnp.int32, t.shape, 1)
        x = t + dt_ref[...]
        softplus = jnp.maximum(x, 0.0) + jnp.log1p(jnp.exp(-jnp.abs(x)))
        decay = -jnp.exp(alog_ref[...]) * softplus
        gates = jnp.where(lane < 2 * DN_HEADS, decay, jax.nn.sigmoid(t))
        ri = lax.broadcasted_iota(jnp.int32, (DN_CHUNK, DN_CHUNK), 0)
        ci = lax.broadcasted_iota(jnp.int32, (DN_CHUNK, DN_CHUNK), 1)
        tri = jnp.concatenate([ri >= ci, ri <= ci], axis=0).astype(BF16)
        hi = gates.astype(BF16)
        rest = gates - hi.astype(F32)
        mid = rest.astype(BF16)
        low = (rest - mid.astype(F32)).astype(BF16)
        parts = jnp.concatenate([hi, mid, low], axis=1)
        lane_c = lax.broadcasted_iota(jnp.int32, (DN_CHUNK, LANES), 1)
        for c in range(n_tot // DN_CHUNK):
            gc = gates[c * DN_CHUNK:(c + 1) * DN_CHUNK]
            cum3 = _dot(tri, parts[c * DN_CHUNK:(c + 1) * DN_CHUNK])
            cum = cum3[:, :LANES] + cum3[:, LANES:2 * LANES] + cum3[:, 2 * LANES:]
            col = jnp.where(lane_c < DN_HEADS, cum[:DN_CHUNK],
                            jnp.where(lane_c < 2 * DN_HEADS, cum[DN_CHUNK:], gc))
            gcol_ref[0, c * DN_CHUNK:(c + 1) * DN_CHUNK, :] = col
            grow_ref[0, c] = jnp.concatenate([col, col], axis=0).T


def _dn_project(x, ctx, mod, w_in, w_gates, conv_w, alog_row, dt_row):
    batch, n_lat, d = x.shape
    n_tot = n_lat + ctx.shape[1]
    hk = DN_HEADS * DN_DK
    n_qkv_tiles = 3 * hk // COL_TILE
    n_z_tiles = hk // COL_TILE
    n_tiles = n_qkv_tiles + n_z_tiles + 1
    nc = n_tot // DN_CHUNK
    kern = functools.partial(_dn_proj_kernel, n_lat=n_lat, n_tot=n_tot,
                             n_qkv_tiles=n_qkv_tiles, n_z_tiles=n_z_tiles)
    return pl.pallas_call(
        kern,
        grid=(batch, n_tiles),
        in_specs=[pl.BlockSpec((1, n_lat, d), lambda b, j: (b, 0, 0)),
                  pl.BlockSpec((1, n_tot - n_lat, d), lambda b, j: (b, 0, 0))]
        + _mod_specs(batch, d, 1, 0)
        + [
            pl.BlockSpec((1, d, COL_TILE), lambda b, j: (0, 0, jnp.minimum(j, n_tiles - 2))),
            pl.BlockSpec((d, LANES), lambda b, j: (0, 0)),
            pl.BlockSpec((DN_CONV, COL_TILE), lambda b, j: (0, jnp.minimum(j, n_qkv_tiles - 1))),
            pl.BlockSpec((1, LANES), lambda b, j: (0, 0)),
            pl.BlockSpec((1, LANES), lambda b, j: (0, 0)),
        ],
        out_specs=[
            pl.BlockSpec((1, n_tot, COL_TILE), lambda b, j: (b, 0, jnp.minimum(j, n_qkv_tiles - 1))),
            pl.BlockSpec((1, n_tot, COL_TILE),
                         lambda b, j: (b, 0, jnp.clip(j - n_qkv_tiles, 0, n_z_tiles - 1))),
            pl.BlockSpec((1, n_tot, LANES), lambda b, j: (b, 0, 0)),
            pl.BlockSpec((1, nc, LANES, LANES), lambda b, j: (b, 0, 0, 0)),
        ],
        out_shape=[
            jax.ShapeDtypeStruct((batch, n_tot, 3 * hk), F32),
            jax.ShapeDtypeStruct((batch, n_tot, hk), F32),
            jax.ShapeDtypeStruct((batch, n_tot, LANES), F32),
            jax.ShapeDtypeStruct((batch, nc, LANES, LANES), F32),
        ],
        scratch_shapes=[
            pltpu.VMEM((n_tot, d), BF16),
            pltpu.VMEM((n_tot + 3 * SUBLANES, COL_TILE), F32),
        ],
        compiler_params=_cparams("dn_project", 2),
        name="dn_project",
    )(x, ctx, mod, mod, mod, mod, w_in, w_gates, conv_w, alog_row, dt_row)


INV_BASE = 16
SCAN_BATCH = 2
SCAN_UNITS = SCAN_BATCH * DN_HEADS


def _side_by_side(fwd, bwd):
    return jnp.concatenate([fwd, bwd], axis=1)


def _block_diag16(top, bottom):
    top, bottom = top.astype(BF16), bottom.astype(BF16)
    return jnp.concatenate([_side_by_side(top, jnp.zeros((top.shape[0], bottom.shape[1]), BF16)),
                            _side_by_side(jnp.zeros((bottom.shape[0], top.shape[1]), BF16), bottom)], axis=0)


def _dn_local_stages(ins, slot, sidx, half):
    u_ref, wq_ref, m_ref, e_ref = slot
    c = DN_CHUNK
    st = [dict() for _ in range(SCAN_UNITS)]

    def each(fn):
        def run():
            for n, t in enumerate(st):
                fn(t, n)
        return run

    def grids():
        ri = lax.broadcasted_iota(jnp.int32, (c, 2 * c), 0)
        lane = lax.broadcasted_iota(jnp.int32, (c, 2 * c), 1)
        return ri, lane & (c - 1), lane < c

    def pair_dot(x, y):
        _, _, fwd_half = grids()
        rhs = jnp.concatenate([jnp.where(fwd_half, y, 0.0).astype(BF16),
                               jnp.where(fwd_half, 0.0, y).astype(BF16)], axis=0)
        return _dot(x.astype(BF16), rhs)

    def load(t, n):
        bb, hh = divmod(n, DN_HEADS)
        ri, cj, fwd_half = grids()
        lane = lax.broadcasted_iota(jnp.int32, (c, LANES), 1)
        cols = slice(hh * DN_DK, (hh + 1) * DN_DK)
        per_dir = []
        for d in range(2):
            q_ref, k_ref, v_ref, gcol_ref, grow_ref = ins[d]
            part = half if d == 0 else 1 - half
            rows = slice(part * c, (part + 1) * c)
            gates = gcol_ref[bb, rows, :]
            g_lane = d * DN_HEADS + hh
            gcum = jnp.sum(jnp.where(lane == g_lane, gates, 0.0), axis=-1, keepdims=True)
            beta = jnp.sum(jnp.where(lane == 2 * DN_HEADS + g_lane, gates, 0.0), axis=-1, keepdims=True)
            grow = grow_ref[bb, part, g_lane:g_lane + 1, :]
            g_last = grow[:, c - 1:c] if d == 0 else grow[:, 0:1]
            per_dir.append((q_ref[bb, rows, cols], k_ref[bb, rows, cols], v_ref[bb, rows, cols],
                            gcum, beta, grow, g_last))
        (qf, kf, vf, gcf, bf, grf, glf), (qb, kb, vb, gcb, bb, grb, glb) = per_dir
        incl = jnp.logical_or(ri == cj, (ri > cj) == fwd_half)
        diff = jnp.where(fwd_half, gcf, gcb) - jnp.where(fwd_half[0:1], grf, grb)
        t["gamma"] = jnp.where(incl, jnp.exp(jnp.where(incl, diff, 0.0)), 0.0)
        kbf, kbb = kf * bf, kb * bb
        dec_f, dec_b = jnp.exp(gcf), jnp.exp(gcb)
        t["lhs16"] = _side_by_side(jnp.concatenate([kbf, qf], axis=0).astype(BF16),
                                   jnp.concatenate([kbb, qb], axis=0).astype(BF16))
        t["keys16"] = _block_diag16(kf, kb)
        t["rhs16"] = _block_diag16(_side_by_side(vf * bf, kbf * dec_f), _side_by_side(vb * bb, kbb * dec_b))
        wq_ref[sidx, n, c:, :DN_DK] = (qf * dec_f).astype(BF16)
        wq_ref[sidx, n, c:, DN_DK:] = (qb * dec_b).astype(BF16)
        k_dec = jnp.concatenate([kf * jnp.exp(glf - gcf), kb * jnp.exp(glb - gcb)], axis=0)
        m_ref[sidx, n, c:, :] = k_dec.T.astype(BF16)
        e_ref[sidx, n] = _side_by_side(jnp.broadcast_to(jnp.exp(glf), (SUBLANES, DN_DV)),
                                        jnp.broadcast_to(jnp.exp(glb), (SUBLANES, DN_DV)))

    def gram(t, n):
        t["kq"] = _dot_nt(t.pop("lhs16"), t.pop("keys16"))

    def split(t, n):
        ri, cj, fwd_half = grids()
        strict = jnp.logical_and(ri != cj, (ri > cj) == fwd_half)
        kq = t.pop("kq")
        gamma = t.pop("gamma")
        a = jnp.where(strict, kq[:c] * gamma, 0.0)
        m_ref[sidx, n, :c, :] = (kq[c:] * gamma).astype(BF16)
        a_diag = jnp.where((ri // INV_BASE) == (cj // INV_BASE), a, 0.0)
        t["a"] = a
        t["inv"] = (ri == cj).astype(F32) - a_diag
        t["pw"] = pair_dot(a_diag, a_diag)

    def series(t, n):
        both = pair_dot(jnp.concatenate([t["pw"], t["inv"]], axis=0), t["pw"])
        t["pw"] = both[:c]
        t["inv"] = t["inv"] + both[c:]

    def series_last(t, n):
        t["inv"] = t["inv"] + pair_dot(t["inv"], t.pop("pw"))

    def merge_a(blk):
        def fn(t, n):
            ri, cj, _ = grids()
            off = jnp.logical_and((ri // (2 * blk)) == (cj // (2 * blk)), (ri // blk) != (cj // blk))
            t["y"] = pair_dot(jnp.where(off, t["a"], 0.0), t["inv"])
        return fn

    def merge_b(t, n):
        t["inv"] = t["inv"] - pair_dot(t["inv"], t.pop("y"))

    def solve(t, n):
        t["uw"] = _dot(t.pop("inv").astype(BF16), t.pop("rhs16"))
        t.pop("a")

    def store(t, n):
        uw = t.pop("uw")
        for d in range(2):
            base = d * (DN_DV + DN_DK)
            u_ref[sidx, n, :, d * DN_DV:(d + 1) * DN_DV] = uw[:, base:base + DN_DV]
            wq_ref[sidx, n, :c, d * DN_DK:(d + 1) * DN_DK] = uw[:, base + DN_DV:base + DN_DV + DN_DK].astype(BF16)

    stages = [load, gram, split] + [series] * (int(math.log2(INV_BASE)) - 2) + [series_last]
    blk = INV_BASE
    while blk < c:
        stages += [merge_a(blk), merge_b]
        blk *= 2
    stages += [solve, store]
    return [each(fn) for fn in stages]


def _dn_serial_stages(outs, s_ref, slot, sidx, half):
    u_ref, wq_ref, m_ref, e_ref = slot
    c = DN_CHUNK
    st = [dict() for _ in range(SCAN_UNITS)]

    def each(fn):
        def run():
            for n, t in enumerate(st):
                fn(t, n)
        return run

    def state_dot(t, n):
        t["s"] = (s_ref[n], s_ref[SCAN_UNITS + n])
        t["ws"] = _dot(wq_ref[sidx, n], _block_diag16(*t["s"]))

    def value_dot(t, n):
        v_new = u_ref[sidx, n] - t["ws"][:c]
        t["mv"] = _dot(m_ref[sidx, n], _block_diag16(v_new[:, :DN_DV], v_new[:, DN_DV:]))

    def update(t, n):
        bb, hh = divmod(n, DN_HEADS)
        mv, ws, s = t.pop("mv"), t.pop("ws"), t.pop("s")
        gain = e_ref[sidx, n][0:1, :]
        for d in range(2):
            lanes = slice(d * DN_DV, (d + 1) * DN_DV)
            s_ref[d * SCAN_UNITS + n] = s[d] * gain[:, lanes] + mv[c:, lanes]
            part = half if d == 0 else 1 - half
            outs[d][bb, part * c:(part + 1) * c, hh * DN_DV:(hh + 1) * DN_DV] = ws[c:, lanes] + mv[:c, lanes]

    return [each(fn) for fn in (state_dot, value_dot, update)]


def _dn_scan_kernel(*refs):
    ins = (refs[0:5], refs[5:10])
    outs = refs[10:12]
    s_ref = refs[12]
    slot_sets = (refs[13:17], refs[17:21])
    g = pl.program_id(1)

    @pl.when(g == 0)
    def _():
        for ref in (s_ref,) + tuple(slot_sets[0]) + tuple(slot_sets[1]):
            ref[...] = jnp.zeros(ref.shape, ref.dtype)

    def step(written, read):
        loc = [_dn_local_stages(ins, written, half, half) for half in range(2)]
        ser = [stage for half in range(2) for stage in _dn_serial_stages(outs, s_ref, read, half, half)]
        gap = len(loc[0]) // len(ser)
        for n, stages in enumerate(zip(*loc)):
            if n % gap == 0 and n // gap < len(ser):
                ser[n // gap]()
            for stage in stages:
                stage()

    for parity in range(2):
        pl.when(g % 2 == parity)(functools.partial(step, slot_sets[parity], slot_sets[1 - parity]))


def _dn_scan(qkv, gcol, grow, n_lat):
    batch, n_tot, _ = qkv.shape
    hk = DN_HEADS * DN_DK
    pair = 2 * DN_CHUNK
    n_pairs = n_tot // pair
    lat_pairs = n_lat // pair
    ctx_pairs = n_pairs - lat_pairs
    assert n_lat % pair == 0 and n_tot % pair == 0 and batch % SCAN_BATCH == 0

    def fwd(p):
        return jnp.where(p < ctx_pairs, p + lat_pairs, p - ctx_pairs)

    def bwd(p):
        return n_pairs - 1 - p

    def p_in(g):
        return jnp.minimum(g, n_pairs - 1)

    def p_out(g):
        return jnp.maximum(g - 1, 0)

    def in_specs(where):
        return [pl.BlockSpec((SCAN_BATCH, pair, hk), lambda b, g, col=col: (b, where(p_in(g)), col))
                for col in range(3)] + [
            pl.BlockSpec((SCAN_BATCH, pair, LANES), lambda b, g: (b, where(p_in(g)), 0)),
            pl.BlockSpec((SCAN_BATCH, 2, LANES, LANES), lambda b, g: (b, where(p_in(g)), 0, 0)),
        ]

    out_spec = lambda where: pl.BlockSpec((SCAN_BATCH, pair, hk), lambda b, g: (b, where(p_out(g)), 0))
    out_shape = jax.ShapeDtypeStruct((batch, n_tot, hk), F32)
    slot = [pltpu.VMEM((2, SCAN_UNITS, DN_CHUNK, 2 * DN_DV), F32),
            pltpu.VMEM((2, SCAN_UNITS, 2 * DN_CHUNK, 2 * DN_DK), BF16),
            pltpu.VMEM((2, SCAN_UNITS, DN_CHUNK + DN_DK, 2 * DN_CHUNK), BF16),
            pltpu.VMEM((2, SCAN_UNITS, SUBLANES, 2 * DN_DV), F32)]
    per_dir = (qkv, qkv, qkv, gcol, grow)
    return pl.pallas_call(
        _dn_scan_kernel,
        grid=(batch // SCAN_BATCH, n_pairs + 1),
        in_specs=in_specs(fwd) + in_specs(bwd),
        out_specs=[out_spec(fwd), out_spec(bwd)],
        out_shape=[out_shape, out_shape],
        scratch_shapes=[pltpu.VMEM((2 * SCAN_UNITS, DN_DK, DN_DV), F32)] + slot + slot,
        compiler_params=_cparams("dn_scan", 2),
        name="dn_scan",
    )(*per_dir, *per_dir)


def _sublayers_kernel(*refs, alpha, gated_norm, split_residual, n_lat_tiles):
    if gated_norm:
        y_ref, y2_ref, z_ref, ng_ref = refs[:4]
    else:
        y_ref = refs[0]
    (gt1_ref, sc2_ref, sh2_ref, gt2_ref, wo_ref, l1g_ref, l1b_ref,
     wg_ref, wu_ref, wd_ref, l2g_ref, l2b_ref, o_ref, yb_ref) = refs[-14:]
    if split_residual:
        x_ref, c_ref = refs[-16:-14]
        residual = jnp.where(pl.program_id(1) < n_lat_tiles, x_ref[0], c_ref[0])
    else:
        residual = refs[-15][0]
    if gated_norm:
        for h0 in range(0, y_ref.shape[2], DN_DV):
            yh = y_ref[0, :, h0:h0 + DN_DV] + y2_ref[0, :, h0:h0 + DN_DV]
            ms = jnp.mean(yh * yh, axis=-1, keepdims=True)
            yh = yh * lax.rsqrt(ms + RMS_EPS) * ng_ref[...]
            yb_ref[:, h0:h0 + DN_DV] = (yh * _silu(z_ref[0, :, h0:h0 + DN_DV])).astype(BF16)
        mixed = yb_ref[...]
    else:
        mixed = y_ref[0]
    x1 = _layernorm(alpha * residual + gt1_ref[0] * _dot(mixed, wo_ref[...]), l1g_ref[...], l1b_ref[...])
    h = (x1 * (1.0 + sc2_ref[0]) + sh2_ref[0]).astype(BF16)
    gate = _dot(h, wg_ref[...])
    up = _dot(h, wu_ref[...])
    act = (_silu(gate) * up).astype(BF16)
    o_ref[0] = _layernorm(alpha * x1 + gt2_ref[0] * _dot(act, wd_ref[...]), l2g_ref[...], l2b_ref[...])


def _sublayers(ys, residual, mod, w_out16, ln1, ffn16, ln2, alpha, n_lat, n_out, gated_norm):
    batch, _, d = residual[0].shape
    hv = ys[0].shape[2]
    split = len(residual) == 2
    tm = WIDE_ROW_TILE if not split and n_out % WIDE_ROW_TILE == 0 and n_lat % WIDE_ROW_TILE == 0 else ROW_TILE
    n_lat_tiles = n_lat // tm
    kern = functools.partial(_sublayers_kernel, alpha=alpha, gated_norm=gated_norm, split_residual=split,
                             n_lat_tiles=n_lat_tiles)
    tok = lambda width: pl.BlockSpec((1, tm, width), lambda b, t: (b, t, 0))
    res_specs = [tok(d)] if not split else [
        pl.BlockSpec((1, tm, d), lambda b, t: (b, jnp.minimum(t, n_lat_tiles - 1), 0)),
        pl.BlockSpec((1, tm, d), lambda b, t: (b, jnp.maximum(t - n_lat_tiles, 0), 0))]
    mod_spec = lambda idx: pl.BlockSpec((1, 1, d), lambda b, t: (jnp.where(t < n_lat_tiles, b, batch), 0, idx))
    resident = pl.BlockSpec(memory_space=pltpu.VMEM)
    vec = pl.BlockSpec((1, d), lambda b, t: (0, 0))
    return pl.pallas_call(
        kern,
        grid=(batch, n_out // tm),
        in_specs=([tok(hv)] * 3 + [pl.BlockSpec((1, ys[3].shape[1]), lambda b, t: (0, 0))]
                  if gated_norm else [tok(hv)]) + res_specs
        + [mod_spec(2), mod_spec(4), mod_spec(3), mod_spec(5), resident, vec, vec, resident, resident, resident,
           vec, vec],
        out_specs=tok(d),
        out_shape=jax.ShapeDtypeStruct((batch, n_out, d), F32),
        scratch_shapes=[pltpu.VMEM((tm, hv), BF16)],
        compiler_params=_cparams("sublayers", 2),
        name="sublayers",
    )(*ys, *residual, mod, mod, mod, mod, w_out16, *ln1, *ffn16, *ln2)


def _da_proj_kernel(x_ref, sc_l, sh_l, sc_c, sh_c, w_ref, cos_ref, sin_a_ref, sin_b_ref,
                    o_ref, xb_ref, *, n_lat, n_qk_tiles):
    j = pl.program_id(1)

    @pl.when(j == 0)
    def _():
        _modulate_into(xb_ref, x_ref[0, :n_lat, :], x_ref[0, n_lat:, :], sc_l, sh_l, sc_c, sh_c)

    blocks = _row_blocks(n_lat, x_ref.shape[1])

    def projector():
        w16 = w_ref[0].astype(BF16)
        return lambda r0, rows: _dot(xb_ref[r0:r0 + rows, :], w16)

    @pl.when(j < n_qk_tiles)
    def _():
        project = projector()
        scale = jnp.where(j < n_qk_tiles // 2, DA_DIM ** -0.5 * LOG2E, 1.0)
        quarter = DA_DIM // 4

        def rope(r0, rows, acc):
            cos, sin_a, sin_b = (t[r0:r0 + rows, :] for t in (cos_ref, sin_a_ref, sin_b_ref))
            for h0 in range(0, DA_COL_TILE, LANES):
                xh = acc[:, h0:h0 + LANES]
                rot = pltpu.roll(xh, quarter, 1) * sin_a + pltpu.roll(xh, LANES - quarter, 1) * sin_b
                o_ref[0, r0:r0 + rows, h0:h0 + LANES] = ((xh * cos + rot) * scale).astype(BF16)

        _pipelined(blocks, project, rope)

    @pl.when(j >= n_qk_tiles)
    def _():
        project = projector()
        for r0, rows in blocks:
            o_ref[0, r0:r0 + rows, :] = project(r0, rows).astype(BF16)


def _da_project(xc, mod, w_in, cos_t, sin_a, sin_b, n_lat):
    batch, n_tot, d = xc.shape
    n_cols = w_in.shape[2]
    n_tiles = n_cols // DA_COL_TILE
    n_qk_tiles = 2 * n_tiles // 3
    kern = functools.partial(_da_proj_kernel, n_lat=n_lat, n_qk_tiles=n_qk_tiles)
    table = pl.BlockSpec((n_tot, LANES), lambda b, j: (0, 0))
    return pl.pallas_call(
        kern,
        grid=(batch, n_tiles),
        in_specs=[pl.BlockSpec((1, n_tot, d), lambda b, j: (b, 0, 0))]
        + _mod_specs(batch, d, 1, 0)
        + [pl.BlockSpec((1, d, DA_COL_TILE), lambda b, j: (0, 0, j)), table, table, table],
        out_specs=pl.BlockSpec((1, n_tot, DA_COL_TILE), lambda b, j: (b, 0, j)),
        out_shape=jax.ShapeDtypeStruct((batch, n_tot, n_cols), BF16),
        scratch_shapes=[pltpu.VMEM((n_tot, d), BF16)],
        compiler_params=_cparams("da_project", 2),
        name="da_project",
    )(xc, mod, mod, mod, mod, w_in, cos_t, sin_a, sin_b)


def _da_attn_kernel(q_ref, k_ref, v_ref, lam_ref, g_ref, o_ref, *, lambda_init):
    lam = lam_ref[...]
    lam_val = (jnp.exp(jnp.sum(lam[0:1] * lam[1:2], axis=-1, keepdims=True))
               - jnp.exp(jnp.sum(lam[2:3] * lam[3:4], axis=-1, keepdims=True)) + lambda_init)
    k = k_ref[0]
    n_sub = q_ref.shape[1] // ATT_SUB_ROWS
    lane = lax.broadcasted_iota(jnp.int32, (ATT_SUB_ROWS, 2 * DA_DIM), 1)

    def scores(r):
        q = q_ref[0, r * ATT_SUB_ROWS:(r + 1) * ATT_SUB_ROWS, :]
        zero = jnp.zeros_like(q)
        return [_dot_nt(jnp.where((lane < DA_DIM) if comp == 0 else (lane >= DA_DIM), q, zero), k)
                for comp in range(2)]

    def weights(s):
        e0, e1 = (jnp.exp2(sc - jnp.max(sc, axis=-1, keepdims=True)) for sc in s)
        l0 = jnp.sum(e0, axis=-1, keepdims=True)
        l1 = jnp.sum(e1, axis=-1, keepdims=True)
        return (e0 - e1 * (lam_val * l0 / l1)).astype(BF16), 1.0 / l0

    def values(r, a16, inv_l0):
        o = _dot(a16, v_ref[0]) * inv_l0
        ms = jnp.mean(o * o, axis=-1, keepdims=True)
        o_ref[0, r * ATT_SUB_ROWS:(r + 1) * ATT_SUB_ROWS, :] = (
            o * lax.rsqrt(ms + RMS_EPS) * g_ref[...] * (1.0 - lambda_init)).astype(BF16)

    pending = [scores(r) for r in range(min(ATT_LOOKAHEAD, n_sub))]
    for r in range(n_sub):
        if r + ATT_LOOKAHEAD < n_sub:
            pending.append(scores(r + ATT_LOOKAHEAD))
        values(r, *weights(pending.pop(0)))


def _da_attention(qkv16, lam, subln_g, lambda_init, n_lat):
    batch, n_tot, _ = qkv16.shape
    hw = 2 * DA_DIM
    kern = functools.partial(_da_attn_kernel, lambda_init=lambda_init)
    tq = ATT_Q_TILE if n_lat % ATT_Q_TILE == 0 else ROW_TILE
    return pl.pallas_call(
        kern,
        grid=(batch, DA_HEADS, n_lat // tq),
        in_specs=[
            pl.BlockSpec((1, tq, hw), lambda b, h, i: (b, i, h)),
            pl.BlockSpec((1, n_tot, hw), lambda b, h, i: (b, 0, DA_HEADS + h)),
            pl.BlockSpec((1, n_tot, hw), lambda b, h, i: (b, 0, 2 * DA_HEADS + h)),
            pl.BlockSpec(lam.shape, lambda b, h, i: (0, 0)),
            pl.BlockSpec((1, hw), lambda b, h, i: (0, 0)),
        ],
        out_specs=pl.BlockSpec((1, tq, hw), lambda b, h, i: (b, i, h)),
        out_shape=jax.ShapeDtypeStruct((batch, n_lat, DA_HEADS * hw), BF16),
        compiler_params=_cparams("da_attention", 3),
        name="da_attention",
    )(qkv16, qkv16, qkv16, lam, subln_g)


def _rope_tables(n_lat, n_tot):
    quarter = DA_DIM // 4
    inv_freq = ROPE_BASE ** (-jnp.arange(quarter, dtype=F32) / quarter)
    rows = n_lat // GRID_W
    row = jnp.repeat(jnp.arange(rows, dtype=F32), GRID_W)
    col = jnp.tile(jnp.arange(GRID_W, dtype=F32), rows)
    ang_r = row[:, None] * inv_freq
    ang_c = col[:, None] * inv_freq
    ang = jnp.concatenate([ang_r, ang_r, ang_c, ang_c], axis=-1)
    reps = LANES // DA_DIM
    cos = jnp.tile(jnp.cos(ang), (1, reps))
    sin = jnp.tile(jnp.sin(ang), (1, reps))
    upper = (jnp.arange(LANES) % (2 * quarter)) >= quarter
    sin_a = jnp.where(upper, sin, 0.0)
    sin_b = jnp.where(upper, 0.0, -sin)
    n_ctx = n_tot - n_lat
    ext = lambda t, fill: jnp.concatenate([t, jnp.full((n_ctx, LANES), fill, F32)], axis=0)
    return ext(cos, 1.0), ext(sin_a, 0.0), ext(sin_b, 0.0)


def kernel(x, c, ctx, c_ctx, ada_w, ada_b, ln1_g, ln1_b, ln2_g, ln2_b, ffn_w_gate, ffn_w_up, ffn_w_down,
           dn_w_in, dn_conv, dn_a_log, dn_dt_bias, dn_norm_g, dn_w_out, da_w_in, da_lambda, da_subln_g,
           da_w_out):
    batch, n_lat, d = x.shape
    n_ctx = ctx.shape[1]
    n_tot = n_lat + n_ctx
    depth = ada_w.shape[0]
    assert depth == 2, "layer 0 is gated DeltaNet, layer 1 (the last) differential attention"
    assert batch < MOD_ROWS and n_lat % ROW_TILE == 0 and n_ctx % ROW_TILE == 0
    alpha = (2.0 * depth) ** 0.25
    hk = DN_HEADS * DN_DK

    cs = jnp.concatenate([c, c_ctx[None, :], jnp.zeros((MOD_ROWS - batch - 1, d), F32)], axis=0)
    mods = _ada_modulation(cs, ada_w, ada_b).reshape(depth, MOD_ROWS, 1, 6 * d)

    mod = mods[0]
    gate_cols = dn_w_in[0][:, 4 * hk:]
    w_gates = jnp.concatenate([gate_cols, jnp.zeros((d, LANES - gate_cols.shape[1]), F32)], axis=1)
    lane_pad = lambda v: jnp.concatenate([v.reshape(1, -1), jnp.zeros((1, LANES - v.size), F32)], axis=1)
    qkv, z, gcol, grow = _dn_project(x, ctx, mod, dn_w_in, w_gates, dn_conv[0],
                                     lane_pad(dn_a_log[0]), lane_pad(dn_dt_bias[0]))
    o_fwd, o_bwd = _dn_scan(qkv, gcol, grow, n_lat)
    vec = lambda v: v.reshape(1, -1)
    ffn16 = lambda i: tuple(_to_bf16(w, i) for w in (ffn_w_gate, ffn_w_up, ffn_w_down))
    xc = _sublayers((o_fwd, o_bwd, z, vec(dn_norm_g[0])), (x, ctx), mod, _to_bf16(dn_w_out, 0),
                    (vec(ln1_g[0]), vec(ln1_b[0])), ffn16(0), (vec(ln2_g[0]), vec(ln2_b[0])),
                    alpha, n_lat, n_tot, True)

    mod = mods[1]
    lambda_init = 0.8 - 0.6 * math.exp(-0.3 * 1)
    cos_t, sin_a, sin_b = _rope_tables(n_lat, n_tot)
    qkv16 = _da_project(xc, mod, da_w_in, cos_t, sin_a, sin_b, n_lat)
    y = _da_attention(qkv16, da_lambda[0], da_subln_g[0].reshape(1, -1), lambda_init, n_lat)
    return _sublayers((y,), (xc,), mod, _to_bf16(da_w_out, 0), (vec(ln1_g[1]), vec(ln1_b[1])), ffn16(1),
                      (vec(ln2_g[1]), vec(ln2_b[1])), alpha, n_lat, n_lat, False)
```
